```python
import math
import jax, jax.numpy as jnp
from jax import lax
import numpy as np

D_MODEL = 4096
BATCH = 1
SEQ = 8192
DEPTH = 2

N_A_LAYERS = DEPTH // 2
N_B_LAYERS = DEPTH - N_A_LAYERS
EPS = 1e-6

SSM_EXPAND = 2
D_INNER = SSM_EXPAND * D_MODEL
SSM_HEAD_DIM = 64
SSM_HEADS = D_INNER // SSM_HEAD_DIM
SSM_GROUPS = 8
SSM_HEADS_PER_GROUP = SSM_HEADS // SSM_GROUPS
D_STATE = 128
SSM_CONV = 4
SSM_CHUNK = 128
SSM_CONV_DIM = D_INNER + 2 * SSM_GROUPS * D_STATE
SSM_IN_DIM = D_INNER + SSM_CONV_DIM + SSM_HEADS
SSM_NORM_EPS = 1e-5

ATTN_HEAD_DIM = 64
N_Q_HEADS = D_MODEL // ATTN_HEAD_DIM
N_KV_HEADS = 8
Q_PER_KV = N_Q_HEADS // N_KV_HEADS
Q_DIM = N_Q_HEADS * ATTN_HEAD_DIM
KV_DIM = N_KV_HEADS * ATTN_HEAD_DIM
WINDOW = 128
ATTN_BLOCK = 128

N_BUCKETS = 32
MAX_DISTANCE = 128

D_FF = 256 * ((8 * D_MODEL // 3 + 255) // 256)
FFN_CONV = 3

kernel_name = 'hybrid_ssd_swa_sink_yoco_block'


def rmsnorm(x, g, eps=EPS):
    xf = x.astype(jnp.float32)
    y = xf * lax.rsqrt(jnp.mean(xf * xf, axis=-1, keepdims=True) + eps)
    return (y * g.astype(jnp.float32)).astype(x.dtype)


def causal_dwconv(x, w, b):
    k_width, seq = w.shape[0], x.shape[1]
    xp = jnp.pad(x, ((0, 0), (k_width - 1, 0), (0, 0)))
    out = b + xp[:, 0:seq] * w[0]
    for k in range(1, k_width):
        out = out + xp[:, k:k + seq] * w[k]
    return out


def ssd_chunked(x, a, bm, cm):
    bsz, seq = x.shape[0], x.shape[1]
    n_chunks = seq // SSM_CHUNK
    x_c = x.reshape(bsz, n_chunks, SSM_CHUNK, SSM_GROUPS, SSM_HEADS_PER_GROUP, SSM_HEAD_DIM).transpose(1, 0, 2, 3, 4, 5)
    a_c = a.reshape(bsz, n_chunks, SSM_CHUNK, SSM_GROUPS, SSM_HEADS_PER_GROUP).transpose(1, 0, 3, 4, 2)
    b_c = bm.reshape(bsz, n_chunks, SSM_CHUNK, SSM_GROUPS, D_STATE).transpose(1, 0, 2, 3, 4)
    c_c = cm.reshape(bsz, n_chunks, SSM_CHUNK, SSM_GROUPS, D_STATE).transpose(1, 0, 2, 3, 4)
    tril = jnp.tril(jnp.ones((SSM_CHUNK, SSM_CHUNK), dtype=bool))

    def step(state, inp):
        xc, ac, bc, cc = inp
        a_cum = jnp.cumsum(ac, axis=-1)
        seg = a_cum[..., :, None] - a_cum[..., None, :]
        decay = jnp.exp(jnp.where(tril, seg, -jnp.inf))
        cb = jnp.einsum('blgn,bsgn->bgls', cc, bc)
        y_diag = jnp.einsum('bgrls,bsgrp->blgrp', cb[:, :, None] * decay, xc)
        y_off = jnp.einsum('blgn,bgrpn->blgrp', cc, state) * jnp.exp(a_cum).transpose(0, 3, 1, 2)[..., None]
        decay_to_end = jnp.exp(a_cum[..., -1:] - a_cum)
        new_state = state * jnp.exp(a_cum[..., -1])[..., None, None] + jnp.einsum('bsgn,bgrs,bsgrp->bgrpn', bc, decay_to_end, xc)
        return new_state, y_diag + y_off

    state0 = jnp.zeros((bsz, SSM_GROUPS, SSM_HEADS_PER_GROUP, SSM_HEAD_DIM, D_STATE), jnp.float32)
    _, ys = lax.scan(step, state0, (x_c, a_c, b_c, c_c))
    return ys.transpose(1, 0, 2, 3, 4, 5).reshape(bsz, seq, SSM_GROUPS, SSM_HEADS_PER_GROUP, SSM_HEAD_DIM)


def mamba2_mixer(u, w_in, conv_w, conv_b, dt_bias, a_log, d_skip, g_norm, w_out):
    bsz, seq, _ = u.shape
    zxbcdt = u @ w_in
    z, xbc, dt = jnp.split(zxbcdt, [D_INNER, D_INNER + SSM_CONV_DIM], axis=-1)
    xbc = jax.nn.silu(causal_dwconv(xbc, conv_w, conv_b))
    xs, bm, cm = jnp.split(xbc, [D_INNER, D_INNER + SSM_GROUPS * D_STATE], axis=-1)
    dt = jax.nn.softplus(dt.astype(jnp.float32) + dt_bias.astype(jnp.float32))
    a = -jnp.exp(a_log.astype(jnp.float32))
    dt = dt.reshape(bsz, seq, SSM_GROUPS, SSM_HEADS_PER_GROUP)
    x_h = xs.astype(jnp.float32).reshape(bsz, seq, SSM_GROUPS, SSM_HEADS_PER_GROUP, SSM_HEAD_DIM)
    y = ssd_chunked(x_h * dt[..., None], dt * a.reshape(SSM_GROUPS, SSM_HEADS_PER_GROUP),
                    bm.astype(jnp.float32).reshape(bsz, seq, SSM_GROUPS, D_STATE),
                    cm.astype(jnp.float32).reshape(bsz, seq, SSM_GROUPS, D_STATE))
    y = y + d_skip.astype(jnp.float32).reshape(SSM_GROUPS, SSM_HEADS_PER_GROUP, 1) * x_h
    y = y.reshape(bsz, seq, D_INNER) * jax.nn.silu(z.astype(jnp.float32))
    yg = y.reshape(bsz, seq, SSM_GROUPS, D_INNER // SSM_GROUPS)
    yg = yg * lax.rsqrt(jnp.mean(yg * yg, axis=-1, keepdims=True) + SSM_NORM_EPS)
    y = yg.reshape(bsz, seq, D_INNER) * g_norm.astype(jnp.float32)
    return y.astype(u.dtype) @ w_out


def t5_bucket(rel):
    max_exact = N_BUCKETS // 2
    relf = jnp.maximum(rel, 1).astype(jnp.float32)
    large = max_exact + (jnp.log(relf / max_exact) / math.log(MAX_DISTANCE / max_exact) * (N_BUCKETS - max_exact)).astype(jnp.int32)
    large = jnp.minimum(large, N_BUCKETS - 1)
    return jnp.where(rel < max_exact, rel, large)


def shared_kv(h, kv_norm, w_kv, b_kv):
    bsz, seq, _ = h.shape
    kv = rmsnorm(h, kv_norm) @ w_kv + b_kv
    k, v = jnp.split(kv, 2, axis=-1)
    return (k.reshape(bsz, seq, N_KV_HEADS, ATTN_HEAD_DIM), v.reshape(bsz, seq, N_KV_HEADS, ATTN_HEAD_DIM))


def swa_sink_attention(u, k_sh, v_sh, w_q, b_q, sinks, rel_bias, w_o, b_o):
    bsz, seq, _ = u.shape
    nb = seq // ATTN_BLOCK
    q = (u @ w_q + b_q).reshape(bsz, nb, ATTN_BLOCK, N_KV_HEADS, Q_PER_KV, ATTN_HEAD_DIM)
    pad = ((0, 0), (ATTN_BLOCK, 0), (0, 0), (0, 0))
    kb = jnp.pad(k_sh, pad).reshape(bsz, nb + 1, ATTN_BLOCK, N_KV_HEADS, ATTN_HEAD_DIM)
    vb = jnp.pad(v_sh, pad).reshape(bsz, nb + 1, ATTN_BLOCK, N_KV_HEADS, ATTN_HEAD_DIM)
    k2 = jnp.concatenate([kb[:, :-1], kb[:, 1:]], axis=2)
    v2 = jnp.concatenate([vb[:, :-1], vb[:, 1:]], axis=2)
    s = jnp.einsum('bnqgrd,bnkgd->bngrqk', q, k2).astype(jnp.float32) * (ATTN_HEAD_DIM ** -0.5)
    q_idx = jnp.arange(ATTN_BLOCK)
    k_idx = jnp.arange(2 * ATTN_BLOCK)
    rel = q_idx[:, None] - k_idx[None, :] + ATTN_BLOCK
    bias = rel_bias.astype(jnp.float32)[t5_bucket(jnp.maximum(rel, 0))]
    bias = bias.transpose(2, 0, 1).reshape(N_KV_HEADS, Q_PER_KV, ATTN_BLOCK, 2 * ATTN_BLOCK)
    key_pos = jnp.arange(nb)[:, None] * ATTN_BLOCK - ATTN_BLOCK + k_idx[None, :]
    mask = (rel >= 0)[None] & (rel < WINDOW)[None] & (key_pos >= 0)[:, None, :]
    s = jnp.where(mask[None, :, None, None], s + bias, -jnp.inf)
    sink = sinks.astype(jnp.float32).reshape(N_KV_HEADS, Q_PER_KV)[..., None, None]
    m = jnp.maximum(jnp.max(s, axis=-1, keepdims=True), sink)
    p = jnp.exp(s - m)
    p = p / (jnp.sum(p, axis=-1, keepdims=True) + jnp.exp(sink - m))
    o = jnp.einsum('bngrqk,bnkgd->bnqgrd', p.astype(v2.dtype), v2).reshape(bsz, seq, Q_DIM)
    return o @ w_o + b_o


def conv_ffn(u, w_up, conv_w, conv_b, w_down):
    gu = causal_dwconv(u @ w_up, conv_w, conv_b)
    g, v = jnp.split(gu, 2, axis=-1)
    return (jax.nn.silu(g) * v) @ w_down


def setup_inputs(seed: int = 0) -> dict:
    key = jax.random.key(seed)
    ks = jax.random.split(key, 26)

    def nrm(k, shape, scale):
        return jax.random.normal(k, shape, jnp.float32) * scale

    def gain(k, shape):
        return 1.0 + nrm(k, shape, 0.05)

    dt = jnp.exp(jax.random.uniform(ks[10], (N_A_LAYERS, SSM_HEADS), jnp.float32, math.log(1e-3), math.log(1e-1)))
    return {
        'x': nrm(ks[0], (BATCH, SEQ, D_MODEL), 1.0),
        'norm_mix_pre': gain(ks[1], (DEPTH, D_MODEL)),
        'norm_mix_post': gain(ks[2], (DEPTH, D_MODEL)),
        'norm_ffn_pre': gain(ks[3], (DEPTH, D_MODEL)),
        'norm_ffn_post': gain(ks[4], (DEPTH, D_MODEL)),
        'ssm_w_in': nrm(ks[5], (N_A_LAYERS, D_MODEL, SSM_IN_DIM), D_MODEL ** -0.5),
        'ssm_conv_w': nrm(ks[6], (N_A_LAYERS, SSM_CONV, SSM_CONV_DIM), SSM_CONV ** -0.5),
        'ssm_conv_b': nrm(ks[7], (N_A_LAYERS, SSM_CONV_DIM), 0.02),
        'ssm_dt_bias': dt + jnp.log(-jnp.expm1(-dt)),
        'ssm_a_log': jnp.log(jax.random.uniform(ks[8], (N_A_LAYERS, SSM_HEADS), jnp.float32, 1.0, 16.0)),
        'ssm_d': gain(ks[9], (N_A_LAYERS, SSM_HEADS)),
        'ssm_norm': gain(ks[11], (N_A_LAYERS, D_INNER)),
        'ssm_w_out': nrm(ks[12], (N_A_LAYERS, D_INNER, D_MODEL), D_INNER ** -0.5),
        'kv_norm': gain(ks[13], (D_MODEL,)),
        'w_kv': nrm(ks[14], (D_MODEL, 2 * KV_DIM), D_MODEL ** -0.5),
        'b_kv': nrm(ks[15], (2 * KV_DIM,), 0.02),
        'attn_w_q': nrm(ks[16], (N_B_LAYERS, D_MODEL, Q_DIM), D_MODEL ** -0.5),
        'attn_b_q': nrm(ks[17], (N_B_LAYERS, Q_DIM), 0.02),
        'attn_sinks': nrm(ks[18], (N_B_LAYERS, N_Q_HEADS), 0.5),
        'attn_w_o': nrm(ks[19], (N_B_LAYERS, Q_DIM, D_MODEL), Q_DIM ** -0.5),
        'attn_b_o': nrm(ks[20], (N_B_LAYERS, D_MODEL), 0.02),
        'rel_bias': nrm(ks[21], (N_BUCKETS, N_Q_HEADS), 0.5),
        'ffn_w_up': nrm(ks[22], (DEPTH, D_MODEL, 2 * D_FF), D_MODEL ** -0.5),
        'ffn_conv_w': nrm(ks[23], (DEPTH, FFN_CONV, 2 * D_FF), FFN_CONV ** -0.5),
        'ffn_conv_b': nrm(ks[24], (DEPTH, 2 * D_FF), 0.02),
        'ffn_w_down': nrm(ks[25], (DEPTH, D_FF, D_MODEL), D_FF ** -0.5),
    }


def reference(x, norm_mix_pre, norm_mix_post, norm_ffn_pre, norm_ffn_post,
              ssm_w_in, ssm_conv_w, ssm_conv_b, ssm_dt_bias, ssm_a_log, ssm_d, ssm_norm, ssm_w_out,
              kv_norm, w_kv, b_kv,
              attn_w_q, attn_b_q, attn_sinks, attn_w_o, attn_b_o, rel_bias,
              ffn_w_up, ffn_conv_w, ffn_conv_b, ffn_w_down):
    h = x
    k_sh, v_sh = None, None
    for i in range(DEPTH):
        if i == N_A_LAYERS:
            k_sh, v_sh = shared_kv(h, kv_norm, w_kv, b_kv)
        u = rmsnorm(h, norm_mix_pre[i])
        if i < N_A_LAYERS:
            mix = mamba2_mixer(u, ssm_w_in[i], ssm_conv_w[i], ssm_conv_b[i], ssm_dt_bias[i],
                               ssm_a_log[i], ssm_d[i], ssm_norm[i], ssm_w_out[i])
        else:
            j = i - N_A_LAYERS
            mix = swa_sink_attention(u, k_sh, v_sh, attn_w_q[j], attn_b_q[j], attn_sinks[j],
                                     rel_bias, attn_w_o[j], attn_b_o[j])
        h = h + rmsnorm(mix, norm_mix_post[i])
        f = conv_ffn(rmsnorm(h, norm_ffn_pre[i]), ffn_w_up[i], ffn_conv_w[i], ffn_conv_b[i], ffn_w_down[i])
        h = h + rmsnorm(f, norm_ffn_post[i])
    return h
```

```python
import functools
import math

import numpy as np
import jax
import jax.numpy as jnp
from jax import lax
from jax.experimental import pallas as pl
from jax.experimental.pallas import tpu as pltpu

EPS = 1e-6
SSM_NORM_EPS = 1e-5
SSM_HEAD_DIM = 64
SSM_GROUPS = 8
D_STATE = 128
SSM_CONV = 4
SSM_CHUNK = 128
ATTN_HEAD_DIM = 64
N_KV_HEADS = 8
WINDOW = 128
ATTN_BLOCK = 128
N_BUCKETS = 32
MAX_DISTANCE = 128
FFN_CONV = 3

LANES = 128
SUBLANES = 8
VMEM_CAP_BYTES = 60 * 1024 * 1024

F32 = jnp.float32
BF16 = jnp.bfloat16
HIGHEST = lax.Precision.HIGHEST


def _vmem_limit(est_bytes):
    return int(min(VMEM_CAP_BYTES, max(32 * 1024 * 1024, est_bytes * 5 // 4 + (4 << 20))))


def _params(semantics, est_bytes):
    return pltpu.CompilerParams(dimension_semantics=semantics, vmem_limit_bytes=_vmem_limit(est_bytes))


def _tile(dim, pref, align):
    if dim <= pref:
        return dim
    t = (pref // align) * align
    while t >= align:
        if dim % t == 0:
            return t
        t -= align
    raise ValueError(f"no tile for {dim} (pref {pref}, align {align})")


def _sigmoid(x):
    return 1.0 / (1.0 + jnp.exp(-x))


def _softplus(x):
    return jnp.maximum(x, 0.0) + jnp.log1p(jnp.exp(-jnp.abs(x)))


def _rms(x, g, eps):
    ms = jnp.mean(x * x, axis=-1, keepdims=True)
    return (x * lax.rsqrt(ms + eps)) * g


def _prenorm_kernel(x_ref, g_ref, u_ref):
    u_ref[...] = _rms(x_ref[...], g_ref[...], EPS).astype(u_ref.dtype)


def _prenorm(x, g):
    s, d = x.shape
    tr = _tile(s, 256, SUBLANES)
    est = 2 * tr * d * (4 + 2)
    return pl.pallas_call(
        _prenorm_kernel,
        grid=(s // tr,),
        in_specs=[pl.BlockSpec((tr, d), lambda i: (i, 0)), pl.BlockSpec((1, d), lambda i: (0, 0))],
        out_specs=pl.BlockSpec((tr, d), lambda i: (i, 0)),
        out_shape=jax.ShapeDtypeStruct((s, d), BF16),
        compiler_params=_params(("parallel",), est),
        name="prenorm",
    )(x, g.reshape(1, d))


def _resnorm_kernel(r_ref, m_ref, gpost_ref, *rest, n_u):
    g_refs = rest[:n_u]
    h_ref = rest[n_u]
    u_refs = rest[n_u + 1:]
    h = r_ref[...] + _rms(m_ref[...], gpost_ref[...], EPS)
    h_ref[...] = h
    if n_u:
        ms = jnp.mean(h * h, axis=-1, keepdims=True)
        hn = h * lax.rsqrt(ms + EPS)
        for g_ref, u_ref in zip(g_refs, u_refs):
            u_ref[...] = (hn * g_ref[...]).astype(u_ref.dtype)


def _resnorm(r, m, gpost, gains):
    s, d = r.shape
    n_u = len(gains)
    tr = _tile(s, 128, SUBLANES)
    est = 2 * tr * d * (4 * 3 + 2 * n_u)
    row = pl.BlockSpec((tr, d), lambda i: (i, 0))
    vec = pl.BlockSpec((1, d), lambda i: (0, 0))
    outs = pl.pallas_call(
        functools.partial(_resnorm_kernel, n_u=n_u),
        grid=(s // tr,),
        in_specs=[row, row, vec] + [vec] * n_u,
        out_specs=[row] + [row] * n_u,
        out_shape=[jax.ShapeDtypeStruct((s, d), F32)] + [jax.ShapeDtypeStruct((s, d), BF16)] * n_u,
        compiler_params=_params(("parallel",), est),
        name="resnorm",
    )(r, m, gpost.reshape(1, d), *[g.reshape(1, d) for g in gains])
    return outs


def _matmul_kernel(a_ref, w_ref, *rest, has_bias):
    o_ref = rest[-1]
    acc = jnp.dot(a_ref[...], w_ref[...], preferred_element_type=F32)
    if has_bias:
        acc = acc + rest[0][...]
    o_ref[...] = acc.astype(o_ref.dtype)


def _matmul(a, w, bias, out_dtype, tm_pref, tn_pref, name):
    m, k = a.shape
    n = w.shape[1]
    tm = _tile(m, tm_pref, SUBLANES)
    tn = _tile(n, tn_pref, LANES)
    osz = jnp.dtype(out_dtype).itemsize
    est = 2 * (tm * k * 2 + k * tn * 2 + tm * tn * osz) + tm * tn * 4
    in_specs = [pl.BlockSpec((tm, k), lambda i, j: (i, 0)), pl.BlockSpec((k, tn), lambda i, j: (0, j))]
    args = [a, w]
    if bias is not None:
        in_specs.append(pl.BlockSpec((1, tn), lambda i, j: (0, j)))
        args.append(bias.reshape(1, n).astype(F32))
    return pl.pallas_call(
        functools.partial(_matmul_kernel, has_bias=bias is not None),
        grid=(m // tm, n // tn),
        in_specs=in_specs,
        out_specs=pl.BlockSpec((tm, tn), lambda i, j: (i, j)),
        out_shape=jax.ShapeDtypeStruct((m, n), out_dtype),
        compiler_params=_params(("parallel", "parallel"), est),
        name=name,
    )(*args)


def _ffn_up_kernel(u_ref, wg_ref, wv_ref, cwg_ref, cwv_ref, cbg_ref, cbv_ref, o_ref,
                   halo_g, halo_v, ext_g, ext_v):
    i = pl.program_id(0)
    j = pl.program_id(1)
    tm = u_ref.shape[0]
    u = u_ref[...]

    def conv(w_ref, cw_ref, cb_ref, halo, ext):
        @pl.when(i == 0)
        def _():
            halo[j] = jnp.zeros(halo.shape[1:], F32)

        gu = jnp.dot(u, w_ref[...], preferred_element_type=F32)
        ext[0:SUBLANES, :] = halo[j]
        ext[SUBLANES:SUBLANES + tm, :] = gu
        halo[j] = gu[tm - SUBLANES:tm, :]
        acc = cb_ref[...] + ext[pl.ds(SUBLANES - 2, tm), :] * cw_ref[0:1, :]
        acc = acc + ext[pl.ds(SUBLANES - 1, tm), :] * cw_ref[1:2, :]
        return acc + gu * cw_ref[2:3, :]

    g = conv(wg_ref, cwg_ref, cbg_ref, halo_g, ext_g)
    v = conv(wv_ref, cwv_ref, cbv_ref, halo_v, ext_v)
    o_ref[...] = ((g * _sigmoid(g)) * v).astype(o_ref.dtype)


def _ffn_up(u, w_up, conv_w, conv_b):
    s, d = u.shape
    f = w_up.shape[1] // 2
    tm = _tile(s, 1024, SUBLANES)
    tn = _tile(f, 256, LANES)
    nj = f // tn
    est = 2 * (tm * d * 2 + 2 * d * tn * 2 + tm * tn * 2) + 2 * (nj * SUBLANES * tn * 4) \
        + 2 * (tm + SUBLANES) * tn * 4 + 6 * tm * tn * 4
    cb = conv_b.reshape(1, 2 * f)
    return pl.pallas_call(
        _ffn_up_kernel,
        grid=(s // tm, nj),
        in_specs=[
            pl.BlockSpec((tm, d), lambda i, j: (i, 0)),
            pl.BlockSpec((d, tn), lambda i, j: (0, j)),
            pl.BlockSpec((d, tn), lambda i, j: (0, j + nj)),
            pl.BlockSpec((FFN_CONV, tn), lambda i, j: (0, j)),
            pl.BlockSpec((FFN_CONV, tn), lambda i, j: (0, j + nj)),
            pl.BlockSpec((1, tn), lambda i, j: (0, j)),
            pl.BlockSpec((1, tn), lambda i, j: (0, j + nj)),
        ],
        out_specs=pl.BlockSpec((tm, tn), lambda i, j: (i, j)),
        out_shape=jax.ShapeDtypeStruct((s, f), BF16),
        scratch_shapes=[
            pltpu.VMEM((nj, SUBLANES, tn), F32), pltpu.VMEM((nj, SUBLANES, tn), F32),
            pltpu.VMEM((tm + SUBLANES, tn), F32), pltpu.VMEM((tm + SUBLANES, tn), F32),
        ],
        compiler_params=_params(("arbitrary", "arbitrary"), est),
        name="ffn_up",
    )(u, w_up, w_up, conv_w, conv_w, cb, cb)


def _ssd_kernel(z_ref, x_ref, b_ref, c_ref, dt_ref, dtt_ref,
                cwx_ref, cwb_ref, cwc_ref, cbx_ref, cbb_ref, cbc_ref,
                dtb_row_ref, dtb_col_ref, alog_row_ref, alog_col_ref, dskip_ref, gnorm_ref,
                y_ref, state_ref, ext_x, ext_b, ext_c, ydiag_ref, *, heads):
    chunk = pl.program_id(1)
    cl = SSM_CHUNK
    width = heads * SSM_HEAD_DIM

    @pl.when(chunk == 0)
    def _():
        state_ref[...] = jnp.zeros(state_ref.shape, F32)
        ext_x[0:SUBLANES, :] = jnp.zeros((SUBLANES, ext_x.shape[1]), F32)
        ext_b[0:SUBLANES, :] = jnp.zeros((SUBLANES, ext_b.shape[1]), F32)
        ext_c[0:SUBLANES, :] = jnp.zeros((SUBLANES, ext_c.shape[1]), F32)

    def conv_silu(cur_ref, ext, cw_ref, cb_ref):
        cur = cur_ref[...]
        ext[SUBLANES:SUBLANES + cl, :] = cur
        acc = cb_ref[...] + ext[pl.ds(SUBLANES - 3, cl), :] * cw_ref[0:1, :]
        acc = acc + ext[pl.ds(SUBLANES - 2, cl), :] * cw_ref[1:2, :]
        acc = acc + ext[pl.ds(SUBLANES - 1, cl), :] * cw_ref[2:3, :]
        acc = acc + cur * cw_ref[3:4, :]
        ext[0:SUBLANES, :] = cur[cl - SUBLANES:cl, :]
        return acc * _sigmoid(acc)

    xs = conv_silu(x_ref, ext_x, cwx_ref, cbx_ref)
    bc = conv_silu(b_ref, ext_b, cwb_ref, cbb_ref)
    cc = conv_silu(c_ref, ext_c, cwc_ref, cbc_ref)

    li = lax.broadcasted_iota(jnp.int32, (cl, cl), 0)
    si = lax.broadcasted_iota(jnp.int32, (cl, cl), 1)
    tril = li >= si

    dtv = _softplus(dt_ref[...] + dtb_row_ref[...])
    a = dtv * (-jnp.exp(alog_row_ref[...]))
    acum = jnp.dot(tril.astype(F32), a, precision=HIGHEST, preferred_element_type=F32)
    dtv_t = _softplus(dtt_ref[...] + dtb_col_ref[...])
    a_t = dtv_t * (-jnp.exp(alog_col_ref[...]))
    acum_t = jnp.dot(a_t, (li <= si).astype(F32), precision=HIGHEST, preferred_element_type=F32)
    alast = acum[cl - 1:cl, :]
    exp_acum = jnp.exp(acum)
    decay_end = jnp.exp(alast - acum)

    er = lax.broadcasted_iota(jnp.int32, (LANES, width), 0)
    ec = lax.broadcasted_iota(jnp.int32, (LANES, width), 1)
    expand = (er == (ec >> int(math.log2(SSM_HEAD_DIM)))).astype(F32)

    def widen(v):
        return jnp.dot(v, expand, precision=HIGHEST, preferred_element_type=F32)

    dt_w = widen(dtv)
    exp_acum_w = widen(exp_acum)
    decay_end_w = widen(decay_end)

    xdt = xs * dt_w
    cb = lax.dot_general(cc.astype(BF16), bc.astype(BF16), (((1,), (1,)), ((), ())),
                         preferred_element_type=F32)

    lane = lax.broadcasted_iota(jnp.int32, (cl, LANES), 1)
    low = lane < SSM_HEAD_DIM
    for p in range(heads // 2):
        ms = []
        for hp in range(2):
            r = 2 * p + hp
            seg = acum[:, r:r + 1] - acum_t[r:r + 1, :]
            decay = jnp.exp(jnp.where(tril, seg, -jnp.inf))
            ms.append((cb * decay).astype(BF16))
        m_pair = jnp.concatenate(ms, axis=1)
        xp = xdt[:, p * LANES:(p + 1) * LANES]
        rhs = jnp.concatenate([jnp.where(low, xp, 0.0), jnp.where(low, 0.0, xp)], axis=0).astype(BF16)
        ydiag_ref[:, p * LANES:(p + 1) * LANES] = jnp.dot(m_pair, rhs, preferred_element_type=F32)

    state = state_ref[...]
    y_off = jnp.dot(cc.astype(BF16), state.astype(BF16), preferred_element_type=F32) * exp_acum_w
    y = ydiag_ref[...] + y_off + dskip_ref[...] * xs

    xw = (xdt * decay_end_w).astype(BF16)
    upd = jnp.dot(jnp.transpose(bc).astype(BF16), xw, preferred_element_type=F32)
    state_ref[...] = state * exp_acum_w[cl - 1:cl, :] + upd

    zb = z_ref[...]
    yz = y * (zb * _sigmoid(zb))
    ms2 = jnp.mean(yz * yz, axis=-1, keepdims=True)
    y_ref[...] = ((yz * lax.rsqrt(ms2 + SSM_NORM_EPS)) * gnorm_ref[...]).astype(y_ref.dtype)


def _ssd(zxbc, dt_raw, conv_w, conv_b, dt_bias, a_log, d_skip, g_norm, d_inner, n_heads):
    s = zxbc.shape[0]
    g = SSM_GROUPS
    heads = n_heads // g
    width = d_inner // g
    assert width == heads * SSM_HEAD_DIM and heads % 2 == 0 and heads <= LANES and width % LANES == 0
    assert D_STATE == LANES and s % SSM_CHUNK == 0
    hp = max(SUBLANES, heads)
    nc = s // SSM_CHUNK
    cl = SSM_CHUNK

    dt_g = dt_raw.reshape(s, g, heads).transpose(1, 0, 2)
    dt_row = jnp.pad(dt_g, ((0, 0), (0, 0), (0, LANES - heads)))
    dt_col = jnp.pad(dt_g.transpose(0, 2, 1), ((0, 0), (0, hp - heads), (0, 0)))

    def rowvec(v):
        return jnp.pad(v.astype(F32).reshape(g, 1, heads), ((0, 0), (0, 0), (0, LANES - heads)))

    def colvec(v):
        return jnp.pad(v.astype(F32).reshape(g, heads, 1), ((0, 0), (0, hp - heads), (0, 0)))

    dskip_w = jnp.repeat(d_skip.astype(F32), SSM_HEAD_DIM).reshape(1, d_inner)
    cbias = conv_b.reshape(1, -1)
    wb = width // LANES
    xoff = d_inner // width
    boff = 2 * d_inner // LANES
    coff = boff + g * D_STATE // LANES
    cwb_off = d_inner // LANES
    cwc_off = cwb_off + g * D_STATE // LANES

    est = 2 * (2 * cl * width * 4 + 2 * cl * LANES * 4 + cl * LANES * 4 + hp * cl * 4 + cl * width * 2) \
        + D_STATE * width * 4 + 3 * (cl + SUBLANES) * width * 4 + cl * width * 4 + 16 * cl * width * 4
    return pl.pallas_call(
        functools.partial(_ssd_kernel, heads=heads),
        grid=(g, nc),
        in_specs=[
            pl.BlockSpec((cl, width), lambda gi, c: (c, gi)),
            pl.BlockSpec((cl, width), lambda gi, c: (c, xoff + gi)),
            pl.BlockSpec((cl, D_STATE), lambda gi, c: (c, boff + gi)),
            pl.BlockSpec((cl, D_STATE), lambda gi, c: (c, coff + gi)),
            pl.BlockSpec((None, cl, LANES), lambda gi, c: (gi, c, 0)),
            pl.BlockSpec((None, hp, cl), lambda gi, c: (gi, 0, c)),
            pl.BlockSpec((SSM_CONV, width), lambda gi, c: (0, gi)),
            pl.BlockSpec((SSM_CONV, D_STATE), lambda gi, c: (0, cwb_off + gi)),
            pl.BlockSpec((SSM_CONV, D_STATE), lambda gi, c: (0, cwc_off + gi)),
            pl.BlockSpec((1, width), lambda gi, c: (0, gi)),
            pl.BlockSpec((1, D_STATE), lambda gi, c: (0, cwb_off + gi)),
            pl.BlockSpec((1, D_STATE), lambda gi, c: (0, cwc_off + gi)),
            pl.BlockSpec((None, 1, LANES), lambda gi, c: (gi, 0, 0)),
            pl.BlockSpec((None, hp, 1), lambda gi, c: (gi, 0, 0)),
            pl.BlockSpec((None, 1, LANES), lambda gi, c: (gi, 0, 0)),
            pl.BlockSpec((None, hp, 1), lambda gi, c: (gi, 0, 0)),
            pl.BlockSpec((1, width), lambda gi, c: (0, gi)),
            pl.BlockSpec((1, width), lambda gi, c: (0, gi)),
        ],
        out_specs=pl.BlockSpec((cl, width), lambda gi, c: (c, gi)),
        out_shape=jax.ShapeDtypeStruct((s, d_inner), BF16),
        scratch_shapes=[
            pltpu.VMEM((D_STATE, width), F32),
            pltpu.VMEM((cl + SUBLANES, width), F32),
            pltpu.VMEM((cl + SUBLANES, D_STATE), F32),
            pltpu.VMEM((cl + SUBLANES, D_STATE), F32),
            pltpu.VMEM((cl, width), F32),
        ],
        compiler_params=_params(("parallel", "arbitrary"), est),
        name="ssd_scan",
    )(zxbc, zxbc, zxbc, zxbc, dt_row, dt_col, conv_w, conv_w, conv_w, cbias, cbias, cbias,
      rowvec(dt_bias), colvec(dt_bias), rowvec(a_log), colvec(a_log), dskip_w, g_norm.reshape(1, d_inner))


def _t5_bucket_table():
    q = np.arange(ATTN_BLOCK)[:, None]
    k = np.arange(2 * ATTN_BLOCK)[None, :]
    rel = np.maximum(q - k + ATTN_BLOCK, 0)
    max_exact = N_BUCKETS // 2
    relf = np.maximum(rel, 1).astype(np.float32)
    large = max_exact + (np.log(relf / np.float32(max_exact)) / np.float32(math.log(MAX_DISTANCE / max_exact))
                         * np.float32(N_BUCKETS - max_exact)).astype(np.int32)
    large = np.minimum(large, N_BUCKETS - 1)
    return np.where(rel < max_exact, rel, large).astype(np.int32)


def _attn_kernel(relb_ref, sink_ref, bucket_ref, q_ref, kp_ref, kc_ref, vp_ref, vc_ref, o_ref, bias_ref,
                 *, qpk, n_q_heads):
    gp = pl.program_id(0)
    n = pl.program_id(1)
    blk = ATTN_BLOCK
    heads_here = 2 * qpk
    head0 = gp * heads_here

    @pl.when(n == 0)
    def _():
        bucket = bucket_ref[...]

        def body(hh, carry):
            acc = jnp.zeros((blk, 2 * blk), F32)
            for b in range(N_BUCKETS):
                acc = jnp.where(bucket == b, relb_ref[b * n_q_heads + head0 + hh], acc)
            bias_ref[hh] = acc
            return carry

        lax.fori_loop(0, heads_here, body, 0)

    qi = lax.broadcasted_iota(jnp.int32, (blk, 2 * blk), 0)
    ci = lax.broadcasted_iota(jnp.int32, (blk, 2 * blk), 1)
    rel = qi - ci + blk
    valid = (rel >= 0) & (rel < WINDOW) & ((ci >= blk) | (n > 0))

    kk = jnp.concatenate([kp_ref[...], kc_ref[...]], axis=0)
    vv = jnp.concatenate([vp_ref[...], vc_ref[...]], axis=0)
    lane = lax.broadcasted_iota(jnp.int32, (2 * blk, LANES), 1)
    low = lane < ATTN_HEAD_DIM
    scale = ATTN_HEAD_DIM ** -0.5

    def pair_operand(t, kvh, mult):
        sel = jnp.where(low if kvh == 0 else jnp.logical_not(low), t, 0.0)
        dup = sel + pltpu.roll(sel, ATTN_HEAD_DIM, axis=1)
        if mult is not None:
            dup = dup * mult
        return jnp.concatenate([jnp.where(low, dup, 0.0), jnp.where(low, 0.0, dup)], axis=0).astype(BF16)

    for kvh in range(2):
        kbd = pair_operand(kk, kvh, scale)
        vbd = pair_operand(vv, kvh, None)
        for qp in range(qpk // 2):
            pidx = kvh * (qpk // 2) + qp
            qpair = q_ref[:, pidx * LANES:(pidx + 1) * LANES]
            s = lax.dot_general(qpair, kbd, (((1,), (1,)), ((), ())), preferred_element_type=F32)
            ps = []
            for hp in range(2):
                hh = pidx * 2 + hp
                sink = sink_ref[head0 + hh]
                sh = jnp.where(valid, s[:, hp * 2 * blk:(hp + 1) * 2 * blk] + bias_ref[hh], -jnp.inf)
                m = jnp.maximum(jnp.max(sh, axis=-1, keepdims=True), sink)
                e = jnp.exp(sh - m)
                denom = jnp.sum(e, axis=-1, keepdims=True) + jnp.exp(sink - m)
                ps.append((e / denom).astype(BF16))
            p = jnp.concatenate(ps, axis=1)
            o_ref[:, pidx * LANES:(pidx + 1) * LANES] = jnp.dot(
                p, vbd, preferred_element_type=F32).astype(o_ref.dtype)


def _attention(q, kv, sinks, rel_bias):
    s, qd = q.shape
    n_q_heads = qd // ATTN_HEAD_DIM
    qpk = n_q_heads // N_KV_HEADS
    assert qpk % 2 == 0 and N_KV_HEADS % 2 == 0 and 2 * ATTN_HEAD_DIM == LANES and s % ATTN_BLOCK == 0
    blk = ATTN_BLOCK
    nb = s // blk
    ngp = N_KV_HEADS // 2
    qw = 2 * qpk * ATTN_HEAD_DIM
    voff = N_KV_HEADS * ATTN_HEAD_DIM // LANES
    bucket = jnp.asarray(_t5_bucket_table())
    est = 2 * (2 * blk * qw * 2 + 4 * blk * LANES * 4 + blk * 2 * blk * 4) + 2 * qpk * blk * 2 * blk * 4 \
        + 24 * blk * 4 * blk * 4
    smem = pl.BlockSpec(memory_space=pltpu.SMEM)
    return pl.pallas_call(
        functools.partial(_attn_kernel, qpk=qpk, n_q_heads=n_q_heads),
        grid=(ngp, nb),
        in_specs=[
            smem, smem,
            pl.BlockSpec((blk, 2 * blk), lambda g, n: (0, 0)),
            pl.BlockSpec((blk, qw), lambda g, n: (n, g)),
            pl.BlockSpec((blk, LANES), lambda g, n: (jnp.maximum(n - 1, 0), g)),
            pl.BlockSpec((blk, LANES), lambda g, n: (n, g)),
            pl.BlockSpec((blk, LANES), lambda g, n: (jnp.maximum(n - 1, 0), voff + g)),
            pl.BlockSpec((blk, LANES), lambda g, n: (n, voff + g)),
        ],
        out_specs=pl.BlockSpec((blk, qw), lambda g, n: (n, g)),
        out_shape=jax.ShapeDtypeStruct((s, qd), BF16),
        scratch_shapes=[pltpu.VMEM((2 * qpk, blk, 2 * blk), F32)],
        compiler_params=_params(("parallel", "arbitrary"), est),
        name="swa_attention",
    )(rel_bias.astype(F32).reshape(-1), sinks.astype(F32).reshape(-1), bucket, q, kv, kv, kv, kv)


def kernel(x, norm_mix_pre, norm_mix_post, norm_ffn_pre, norm_ffn_post, ssm_w_in, ssm_conv_w, ssm_conv_b, ssm_dt_bias, ssm_a_log, ssm_d, ssm_norm, ssm_w_out, kv_norm, w_kv, b_kv, attn_w_q, attn_b_q, attn_sinks, attn_w_o, attn_b_o, rel_bias, ffn_w_up, ffn_conv_w, ffn_conv_b, ffn_w_down):
    bsz, s, d = x.shape
    assert bsz == 1 and norm_mix_pre.shape[0] == 2
    d_inner = ssm_norm.shape[-1]
    n_heads = ssm_dt_bias.shape[-1]
    zxbc_dim = ssm_w_in.shape[-1] - n_heads

    def ffn(h_in_u, layer):
        hff = _ffn_up(h_in_u, ffn_w_up[layer].astype(BF16), ffn_conv_w[layer], ffn_conv_b[layer])
        return _matmul(hff, ffn_w_down[layer].astype(BF16), None, F32, 512, 256, "ffn_down")

    h0 = x.reshape(s, d)

    u = _prenorm(h0, norm_mix_pre[0])
    w_in = ssm_w_in[0]
    zxbc = _matmul(u, w_in[:, :zxbc_dim].astype(BF16), None, F32, 1024, 1024, "in_proj")
    dt_raw = _matmul(u, w_in[:, zxbc_dim:].astype(BF16), None, F32, 1024, 128, "dt_proj")
    y = _ssd(zxbc, dt_raw, ssm_conv_w[0], ssm_conv_b[0], ssm_dt_bias[0], ssm_a_log[0], ssm_d[0], ssm_norm[0],
             d_inner, n_heads)
    mix = _matmul(y, ssm_w_out[0].astype(BF16), None, F32, 512, 512, "out_proj")
    h1, u = _resnorm(h0, mix, norm_mix_post[0], [norm_ffn_pre[0]])
    f = ffn(u, 0)

    h2, ukv, uq = _resnorm(h1, f, norm_ffn_post[0], [kv_norm, norm_mix_pre[1]])
    kv = _matmul(ukv, w_kv.astype(BF16), b_kv, F32, 1024, 1024, "kv_proj")
    q = _matmul(uq, attn_w_q[0].astype(BF16), attn_b_q[0], BF16, 1024, 1024, "q_proj")
    o = _attention(q, kv, attn_sinks[0], rel_bias)
    mix = _matmul(o, attn_w_o[0].astype(BF16), attn_b_o[0], F32, 1024, 1024, "o_proj")
    h3, u = _resnorm(h2, mix, norm_mix_post[1], [norm_ffn_pre[1]])
    f = ffn(u, 1)
    (h4,) = _resnorm(h3, f, norm_ffn_post[1], [])
    return h4.reshape(bsz, s, d)
```

```python
import functools
import math

import numpy as np
import jax
import jax.numpy as jnp
from jax import lax
from jax.experimental import pallas as pl
from jax.experimental.pallas import tpu as pltpu

EPS = 1e-6
SSM_NORM_EPS = 1e-5
SSM_HEAD_DIM = 64
SSM_GROUPS = 8
D_STATE = 128
SSM_CONV = 4
SSM_CHUNK = 128
ATTN_HEAD_DIM = 64
N_KV_HEADS = 8
WINDOW = 128
ATTN_BLOCK = 128
N_BUCKETS = 32
MAX_DISTANCE = 128
FFN_CONV = 3

LANES = 128
SUBLANES = 8
VMEM_CAP_BYTES = 60 * 1024 * 1024

F32 = jnp.float32
BF16 = jnp.bfloat16
HIGHEST = lax.Precision.HIGHEST


def _vmem_limit(est_bytes):
    return int(min(VMEM_CAP_BYTES, max(32 * 1024 * 1024, est_bytes * 5 // 4 + (4 << 20))))


def _params(semantics, est_bytes, flags=None):
    return pltpu.CompilerParams(dimension_semantics=semantics, vmem_limit_bytes=_vmem_limit(est_bytes), flags=flags)


def _tile(dim, pref, align):
    if dim <= pref:
        return dim
    t = (pref // align) * align
    while t >= align:
        if dim % t == 0:
            return t
        t -= align
    raise ValueError(f"no tile for {dim} (pref {pref}, align {align})")


def _sigmoid(x):
    return 1.0 / (1.0 + jnp.exp(-x))


def _softplus(x):
    return jnp.maximum(x, 0.0) + jnp.log1p(jnp.exp(-jnp.abs(x)))


def _rms(x, g, eps):
    ms = jnp.mean(x * x, axis=-1, keepdims=True)
    return (x * lax.rsqrt(ms + eps)) * g


def _prenorm_kernel(x_ref, g_ref, u_ref):
    u_ref[...] = _rms(x_ref[...], g_ref[...], EPS).astype(u_ref.dtype)


def _prenorm(x, g):
    s, d = x.shape
    tr = _tile(s, 256, SUBLANES)
    est = 2 * tr * d * (4 + 2)
    return pl.pallas_call(
        _prenorm_kernel,
        grid=(s // tr,),
        in_specs=[pl.BlockSpec((tr, d), lambda i: (i, 0)), pl.BlockSpec((1, d), lambda i: (0, 0))],
        out_specs=pl.BlockSpec((tr, d), lambda i: (i, 0)),
        out_shape=jax.ShapeDtypeStruct((s, d), BF16),
        compiler_params=_params(("parallel",), est),
        name="prenorm",
    )(x, g.reshape(1, d))


def _resnorm_kernel(r_ref, m_ref, gpost_ref, *rest, n_u):
    g_refs = rest[:n_u]
    h_ref = rest[n_u]
    u_refs = rest[n_u + 1:]
    h = r_ref[...] + _rms(m_ref[...], gpost_ref[...], EPS)
    h_ref[...] = h
    if n_u:
        ms = jnp.mean(h * h, axis=-1, keepdims=True)
        hn = h * lax.rsqrt(ms + EPS)
        for g_ref, u_ref in zip(g_refs, u_refs):
            u_ref[...] = (hn * g_ref[...]).astype(u_ref.dtype)


def _resnorm(r, m, gpost, gains):
    s, d = r.shape
    n_u = len(gains)
    tr = _tile(s, 128, SUBLANES)
    est = 2 * tr * d * (4 * 3 + 2 * n_u)
    row = pl.BlockSpec((tr, d), lambda i: (i, 0))
    vec = pl.BlockSpec((1, d), lambda i: (0, 0))
    outs = pl.pallas_call(
        functools.partial(_resnorm_kernel, n_u=n_u),
        grid=(s // tr,),
        in_specs=[row, row, vec] + [vec] * n_u,
        out_specs=[row] + [row] * n_u,
        out_shape=[jax.ShapeDtypeStruct((s, d), F32)] + [jax.ShapeDtypeStruct((s, d), BF16)] * n_u,
        compiler_params=_params(("parallel",), est),
        name="resnorm",
    )(r, m, gpost.reshape(1, d), *[g.reshape(1, d) for g in gains])
    return outs


def _matmul_kernel(a_ref, w_ref, *rest, has_bias):
    o_ref = rest[-1]
    acc = jnp.dot(a_ref[...], w_ref[...].astype(BF16), preferred_element_type=F32)
    if has_bias:
        acc = acc + rest[0][...]
    o_ref[...] = acc.astype(o_ref.dtype)


def _matmul(a, w, bias, out_dtype, tm_pref, tn_pref, name, a_buffers=2, col0=0, n=None):
    m, k = a.shape
    n = w.shape[1] if n is None else n
    tm = _tile(m, tm_pref, SUBLANES)
    tn = _tile(math.gcd(n, col0) if col0 else n, tn_pref, LANES)
    assert n % tn == 0 and col0 % tn == 0
    joff = col0 // tn
    osz = jnp.dtype(out_dtype).itemsize
    wsz = jnp.dtype(w.dtype).itemsize
    est = a_buffers * tm * k * 2 + 2 * (k * tn * wsz + tm * tn * osz) + tm * tn * 4 + (k * tn * 2 if wsz != 2 else 0)
    a_mode = {} if a_buffers == 2 else {"pipeline_mode": pl.Buffered(a_buffers)}
    in_specs = [pl.BlockSpec((tm, k), lambda i, j: (i, 0), **a_mode), pl.BlockSpec((k, tn), lambda i, j: (0, j + joff))]
    args = [a, w]
    if bias is not None:
        in_specs.append(pl.BlockSpec((1, tn), lambda i, j: (0, j)))
        args.append(bias.reshape(1, n).astype(F32))
    return pl.pallas_call(
        functools.partial(_matmul_kernel, has_bias=bias is not None),
        grid=(m // tm, n // tn),
        in_specs=in_specs,
        out_specs=pl.BlockSpec((tm, tn), lambda i, j: (i, j)),
        out_shape=jax.ShapeDtypeStruct((m, n), out_dtype),
        compiler_params=_params(("parallel", "parallel"), est),
        name=name,
    )(*args)


FFN_EPILOGUE_ROWS = 128


FFN_ROW_SPLIT = 1


def _ffn_up_kernel(u_ref, wg_ref, wv_ref, cwg_ref, cwv_ref, cbg_ref, cbv_ref, o_ref, halo, wcat):
    i = pl.program_id(0)
    j = pl.program_id(1)
    tm = u_ref.shape[0]
    tn = wg_ref.shape[1]
    part = tm // FFN_ROW_SPLIT
    rows = min(FFN_EPILOGUE_ROWS, part)

    @pl.when(i == 0)
    def _():
        halo[j] = jnp.zeros(halo.shape[1:], F32)

    wcat[:, 0:tn] = wg_ref[...].astype(BF16)
    wcat[:, tn:2 * tn] = wv_ref[...].astype(BF16)
    parts = [jnp.dot(u_ref[p * part:(p + 1) * part, :], wcat[...], preferred_element_type=F32)
             for p in range(FFN_ROW_SPLIT)]

    cw = jnp.concatenate([cwg_ref[...], cwv_ref[...]], axis=1)
    cb = jnp.concatenate([cbg_ref[...], cbv_ref[...]], axis=1)
    prev = halo[j]
    for p in range(FFN_ROW_SPLIT):
        for c in range(part // rows):
            cur = parts[p][c * rows:(c + 1) * rows, :]
            both = jnp.concatenate([prev, cur], axis=0)
            acc = cb + both[SUBLANES - 2:SUBLANES - 2 + rows, :] * cw[0:1, :]
            acc = acc + both[SUBLANES - 1:SUBLANES - 1 + rows, :] * cw[1:2, :]
            acc = acc + cur * cw[2:3, :]
            g = acc[:, 0:tn]
            v = acc[:, tn:2 * tn]
            r0 = p * part + c * rows
            o_ref[r0:r0 + rows, :] = ((g * _sigmoid(g)) * v).astype(o_ref.dtype)
            prev = cur[rows - SUBLANES:rows, :]
    halo[j] = prev


def _ffn_up(u, w_up, conv_w, conv_b):
    s, d = u.shape
    f = w_up.shape[1] // 2
    tm = _tile(s, 1024, FFN_EPILOGUE_ROWS)
    tn = _tile(f, 256, LANES)
    nj = f // tn
    wsz = jnp.dtype(w_up.dtype).itemsize
    est = 2 * (tm * d * 2 + 2 * d * tn * wsz + tm * tn * 2) + d * 2 * tn * 2 + nj * SUBLANES * 2 * tn * 4 \
        + 3 * tm * 2 * tn * 4
    cb = conv_b.reshape(1, 2 * f)
    return pl.pallas_call(
        _ffn_up_kernel,
        grid=(s // tm, nj),
        in_specs=[
            pl.BlockSpec((tm, d), lambda i, j: (i, 0)),
            pl.BlockSpec((d, tn), lambda i, j: (0, j)),
            pl.BlockSpec((d, tn), lambda i, j: (0, j + nj)),
            pl.BlockSpec((FFN_CONV, tn), lambda i, j: (0, j)),
            pl.BlockSpec((FFN_CONV, tn), lambda i, j: (0, j + nj)),
            pl.BlockSpec((1, tn), lambda i, j: (0, j)),
            pl.BlockSpec((1, tn), lambda i, j: (0, j + nj)),
        ],
        out_specs=pl.BlockSpec((tm, tn), lambda i, j: (i, j)),
        out_shape=jax.ShapeDtypeStruct((s, f), BF16),
        scratch_shapes=[
            pltpu.VMEM((nj, SUBLANES, 2 * tn), F32),
            pltpu.VMEM((d, 2 * tn), BF16),
        ],
        compiler_params=_params(("arbitrary", "arbitrary"), est),
        name="ffn_up",
    )(u, w_up, w_up, conv_w, conv_w, cb, cb)


def _ssd_kernel(z_ref, x_ref, b_ref, c_ref, dt_ref, dtt_ref,
                cwx_ref, cwb_ref, cwc_ref, cbx_ref, cbb_ref, cbc_ref,
                dtb_row_ref, dtb_col_ref, alog_row_ref, alog_col_ref, dskip_ref, gnorm_ref,
                y_ref, state_ref, ext_x, ext_b, ext_c, ydiag_ref, *, heads):
    chunk = pl.program_id(1)
    cl = SSM_CHUNK
    width = heads * SSM_HEAD_DIM

    @pl.when(chunk == 0)
    def _():
        state_ref[...] = jnp.zeros(state_ref.shape, F32)
        ext_x[0:SUBLANES, :] = jnp.zeros((SUBLANES, ext_x.shape[1]), F32)
        ext_b[0:SUBLANES, :] = jnp.zeros((SUBLANES, ext_b.shape[1]), F32)
        ext_c[0:SUBLANES, :] = jnp.zeros((SUBLANES, ext_c.shape[1]), F32)

    def conv_silu(cur_ref, ext, cw_ref, cb_ref):
        cur = cur_ref[...]
        ext[SUBLANES:SUBLANES + cl, :] = cur
        acc = cb_ref[...] + ext[pl.ds(SUBLANES - 3, cl), :] * cw_ref[0:1, :]
        acc = acc + ext[pl.ds(SUBLANES - 2, cl), :] * cw_ref[1:2, :]
        acc = acc + ext[pl.ds(SUBLANES - 1, cl), :] * cw_ref[2:3, :]
        acc = acc + cur * cw_ref[3:4, :]
        ext[0:SUBLANES, :] = cur[cl - SUBLANES:cl, :]
        return acc * _sigmoid(acc)

    xs = conv_silu(x_ref, ext_x, cwx_ref, cbx_ref)
    bc = conv_silu(b_ref, ext_b, cwb_ref, cbb_ref)
    cc = conv_silu(c_ref, ext_c, cwc_ref, cbc_ref)

    li = lax.broadcasted_iota(jnp.int32, (cl, cl), 0)
    si = lax.broadcasted_iota(jnp.int32, (cl, cl), 1)
    tril = li >= si

    dtv = _softplus(dt_ref[...] + dtb_row_ref[...])
    a = dtv * (-jnp.exp(alog_row_ref[...]))
    acum = jnp.dot(tril.astype(F32), a, precision=HIGHEST, preferred_element_type=F32)
    dtv_t = _softplus(dtt_ref[...] + dtb_col_ref[...])
    a_t = dtv_t * (-jnp.exp(alog_col_ref[...]))
    acum_t = jnp.dot(a_t, (li <= si).astype(F32), precision=HIGHEST, preferred_element_type=F32)
    alast = acum[cl - 1:cl, :]
    exp_acum = jnp.exp(acum)
    decay_end = jnp.exp(alast - acum)

    er = lax.broadcasted_iota(jnp.int32, (LANES, width), 0)
    ec = lax.broadcasted_iota(jnp.int32, (LANES, width), 1)
    expand = (er == (ec >> int(math.log2(SSM_HEAD_DIM)))).astype(F32)

    def widen(v):
        return jnp.dot(v, expand, precision=HIGHEST, preferred_element_type=F32)

    dt_w = widen(dtv)
    exp_acum_w = widen(exp_acum)
    decay_end_w = widen(decay_end)

    xdt = xs * dt_w
    cb = lax.dot_general(cc.astype(BF16), bc.astype(BF16), (((1,), (1,)), ((), ())),
                         preferred_element_type=F32)

    lane = lax.broadcasted_iota(jnp.int32, (cl, LANES), 1)
    low = lane < SSM_HEAD_DIM
    for p in range(heads // 2):
        ms = []
        for hp in range(2):
            r = 2 * p + hp
            seg = acum[:, r:r + 1] - acum_t[r:r + 1, :]
            decay = jnp.exp(jnp.where(tril, seg, -jnp.inf))
            ms.append((cb * decay).astype(BF16))
        m_pair = jnp.concatenate(ms, axis=1)
        xp = xdt[:, p * LANES:(p + 1) * LANES]
        rhs = jnp.concatenate([jnp.where(low, xp, 0.0), jnp.where(low, 0.0, xp)], axis=0).astype(BF16)
        ydiag_ref[:, p * LANES:(p + 1) * LANES] = jnp.dot(m_pair, rhs, preferred_element_type=F32)

    state = state_ref[...]
    y_off = jnp.dot(cc.astype(BF16), state.astype(BF16), preferred_element_type=F32) * exp_acum_w
    y = ydiag_ref[...] + y_off + dskip_ref[...] * xs

    xw = (xdt * decay_end_w).astype(BF16)
    upd = jnp.dot(jnp.transpose(bc).astype(BF16), xw, preferred_element_type=F32)
    state_ref[...] = state * exp_acum_w[cl - 1:cl, :] + upd

    zb = z_ref[...]
    yz = y * (zb * _sigmoid(zb))
    ms2 = jnp.mean(yz * yz, axis=-1, keepdims=True)
    y_ref[...] = ((yz * lax.rsqrt(ms2 + SSM_NORM_EPS)) * gnorm_ref[...]).astype(y_ref.dtype)


def _ssd(zxbc, dt_raw, conv_w, conv_b, dt_bias, a_log, d_skip, g_norm, d_inner, n_heads):
    s = zxbc.shape[0]
    g = SSM_GROUPS
    heads = n_heads // g
    width = d_inner // g
    assert width == heads * SSM_HEAD_DIM and heads % 2 == 0 and heads <= LANES and width % LANES == 0
    assert D_STATE == LANES and s % SSM_CHUNK == 0
    hp = max(SUBLANES, heads)
    nc = s // SSM_CHUNK
    cl = SSM_CHUNK

    dt_g = dt_raw.reshape(s, g, heads).transpose(1, 0, 2)
    dt_row = jnp.pad(dt_g, ((0, 0), (0, 0), (0, LANES - heads)))
    dt_col = jnp.pad(dt_g.transpose(0, 2, 1), ((0, 0), (0, hp - heads), (0, 0)))

    def rowvec(v):
        return jnp.pad(v.astype(F32).reshape(g, 1, heads), ((0, 0), (0, 0), (0, LANES - heads)))

    def colvec(v):
        return jnp.pad(v.astype(F32).reshape(g, heads, 1), ((0, 0), (0, hp - heads), (0, 0)))

    dskip_w = jnp.repeat(d_skip.astype(F32), SSM_HEAD_DIM).reshape(1, d_inner)
    cbias = conv_b.reshape(1, -1)
    wb = width // LANES
    xoff = d_inner // width
    boff = 2 * d_inner // LANES
    coff = boff + g * D_STATE // LANES
    cwb_off = d_inner // LANES
    cwc_off = cwb_off + g * D_STATE // LANES

    est = 2 * (2 * cl * width * 4 + 2 * cl * LANES * 4 + cl * LANES * 4 + hp * cl * 4 + cl * width * 2) \
        + D_STATE * width * 4 + 3 * (cl + SUBLANES) * width * 4 + cl * width * 4 + 16 * cl * width * 4
    return pl.pallas_call(
        functools.partial(_ssd_kernel, heads=heads),
        grid=(g, nc),
        in_specs=[
            pl.BlockSpec((cl, width), lambda gi, c: (c, gi)),
            pl.BlockSpec((cl, width), lambda gi, c: (c, xoff + gi)),
            pl.BlockSpec((cl, D_STATE), lambda gi, c: (c, boff + gi)),
            pl.BlockSpec((cl, D_STATE), lambda gi, c: (c, coff + gi)),
            pl.BlockSpec((None, cl, LANES), lambda gi, c: (gi, c, 0)),
            pl.BlockSpec((None, hp, cl), lambda gi, c: (gi, 0, c)),
            pl.BlockSpec((SSM_CONV, width), lambda gi, c: (0, gi)),
            pl.BlockSpec((SSM_CONV, D_STATE), lambda gi, c: (0, cwb_off + gi)),
            pl.BlockSpec((SSM_CONV, D_STATE), lambda gi, c: (0, cwc_off + gi)),
            pl.BlockSpec((1, width), lambda gi, c: (0, gi)),
            pl.BlockSpec((1, D_STATE), lambda gi, c: (0, cwb_off + gi)),
            pl.BlockSpec((1, D_STATE), lambda gi, c: (0, cwc_off + gi)),
            pl.BlockSpec((None, 1, LANES), lambda gi, c: (gi, 0, 0)),
            pl.BlockSpec((None, hp, 1), lambda gi, c: (gi, 0, 0)),
            pl.BlockSpec((None, 1, LANES), lambda gi, c: (gi, 0, 0)),
            pl.BlockSpec((None, hp, 1), lambda gi, c: (gi, 0, 0)),
            pl.BlockSpec((1, width), lambda gi, c: (0, gi)),
            pl.BlockSpec((1, width), lambda gi, c: (0, gi)),
        ],
        out_specs=pl.BlockSpec((cl, width), lambda gi, c: (c, gi)),
        out_shape=jax.ShapeDtypeStruct((s, d_inner), BF16),
        scratch_shapes=[
            pltpu.VMEM((D_STATE, width), F32),
            pltpu.VMEM((cl + SUBLANES, width), F32),
            pltpu.VMEM((cl + SUBLANES, D_STATE), F32),
            pltpu.VMEM((cl + SUBLANES, D_STATE), F32),
            pltpu.VMEM((cl, width), F32),
        ],
        compiler_params=_params(("parallel", "arbitrary"), est),
        name="ssd_scan",
    )(zxbc, zxbc, zxbc, zxbc, dt_row, dt_col, conv_w, conv_w, conv_w, cbias, cbias, cbias,
      rowvec(dt_bias), colvec(dt_bias), rowvec(a_log), colvec(a_log), dskip_w, g_norm.reshape(1, d_inner))


def _t5_bucket_table():
    q = np.arange(ATTN_BLOCK)[:, None]
    k = np.arange(2 * ATTN_BLOCK)[None, :]
    rel = np.maximum(q - k + ATTN_BLOCK, 0)
    max_exact = N_BUCKETS // 2
    relf = np.maximum(rel, 1).astype(np.float32)
    large = max_exact + (np.log(relf / np.float32(max_exact)) / np.float32(math.log(MAX_DISTANCE / max_exact))
                         * np.float32(N_BUCKETS - max_exact)).astype(np.int32)
    large = np.minimum(large, N_BUCKETS - 1)
    return np.where(rel < max_exact, rel, large).astype(np.int32)


def _attn_kernel(relb_ref, sink_ref, bucket_ref, q_ref, kp_ref, kc_ref, vp_ref, vc_ref, o_ref, bias_ref,
                 *, qpk, n_q_heads):
    gp = pl.program_id(0)
    n = pl.program_id(1)
    blk = ATTN_BLOCK
    heads_here = 2 * qpk
    head0 = gp * heads_here

    @pl.when(n == 0)
    def _():
        bucket = bucket_ref[...]

        def body(hh, carry):
            acc = jnp.zeros((blk, 2 * blk), F32)
            for b in range(N_BUCKETS):
                acc = jnp.where(bucket == b, relb_ref[b * n_q_heads + head0 + hh], acc)
            bias_ref[hh] = acc
            return carry

        lax.fori_loop(0, heads_here, body, 0)

    qi = lax.broadcasted_iota(jnp.int32, (blk, 2 * blk), 0)
    ci = lax.broadcasted_iota(jnp.int32, (blk, 2 * blk), 1)
    rel = qi - ci + blk
    valid = (rel >= 0) & (rel < WINDOW) & ((ci >= blk) | (n > 0))

    kk = jnp.concatenate([kp_ref[...], kc_ref[...]], axis=0)
    vv = jnp.concatenate([vp_ref[...], vc_ref[...]], axis=0)
    lane = lax.broadcasted_iota(jnp.int32, (2 * blk, LANES), 1)
    low = lane < ATTN_HEAD_DIM
    scale = ATTN_HEAD_DIM ** -0.5

    def pair_operand(t, kvh, mult):
        sel = jnp.where(low if kvh == 0 else jnp.logical_not(low), t, 0.0)
        dup = sel + pltpu.roll(sel, ATTN_HEAD_DIM, axis=1)
        if mult is not None:
            dup = dup * mult
        return jnp.concatenate([jnp.where(low, dup, 0.0), jnp.where(low, 0.0, dup)], axis=0).astype(BF16)

    for kvh in range(2):
        kbd = pair_operand(kk, kvh, scale)
        vbd = pair_operand(vv, kvh, None)
        for qp in range(qpk // 2):
            pidx = kvh * (qpk // 2) + qp
            qpair = q_ref[:, pidx * LANES:(pidx + 1) * LANES]
            s = lax.dot_general(qpair, kbd, (((1,), (1,)), ((), ())), preferred_element_type=F32)
            ps = []
            for hp in range(2):
                hh = pidx * 2 + hp
                sink = sink_ref[head0 + hh]
                sh = jnp.where(valid, s[:, hp * 2 * blk:(hp + 1) * 2 * blk] + bias_ref[hh], -jnp.inf)
                m = jnp.maximum(jnp.max(sh, axis=-1, keepdims=True), sink)
                e = jnp.exp(sh - m)
                denom = jnp.sum(e, axis=-1, keepdims=True) + jnp.exp(sink - m)
                ps.append((e / denom).astype(BF16))
            p = jnp.concatenate(ps, axis=1)
            o_ref[:, pidx * LANES:(pidx + 1) * LANES] = jnp.dot(
                p, vbd, preferred_element_type=F32).astype(o_ref.dtype)


def _attention(q, kv, sinks, rel_bias):
    s, qd = q.shape
    n_q_heads = qd // ATTN_HEAD_DIM
    qpk = n_q_heads // N_KV_HEADS
    assert qpk % 2 == 0 and N_KV_HEADS % 2 == 0 and 2 * ATTN_HEAD_DIM == LANES and s % ATTN_BLOCK == 0
    blk = ATTN_BLOCK
    nb = s // blk
    ngp = N_KV_HEADS // 2
    qw = 2 * qpk * ATTN_HEAD_DIM
    voff = N_KV_HEADS * ATTN_HEAD_DIM // LANES
    bucket = jnp.asarray(_t5_bucket_table())
    est = 2 * (2 * blk * qw * 2 + 4 * blk * LANES * 4 + blk * 2 * blk * 4) + 2 * qpk * blk * 2 * blk * 4 \
        + 24 * blk * 4 * blk * 4
    smem = pl.BlockSpec(memory_space=pltpu.SMEM)
    return pl.pallas_call(
        functools.partial(_attn_kernel, qpk=qpk, n_q_heads=n_q_heads),
        grid=(ngp, nb),
        in_specs=[
            smem, smem,
            pl.BlockSpec((blk, 2 * blk), lambda g, n: (0, 0)),
            pl.BlockSpec((blk, qw), lambda g, n: (n, g)),
            pl.BlockSpec((blk, LANES), lambda g, n: (jnp.maximum(n - 1, 0), g)),
            pl.BlockSpec((blk, LANES), lambda g, n: (n, g)),
            pl.BlockSpec((blk, LANES), lambda g, n: (jnp.maximum(n - 1, 0), voff + g)),
            pl.BlockSpec((blk, LANES), lambda g, n: (n, voff + g)),
        ],
        out_specs=pl.BlockSpec((blk, qw), lambda g, n: (n, g)),
        out_shape=jax.ShapeDtypeStruct((s, qd), BF16),
        scratch_shapes=[pltpu.VMEM((2 * qpk, blk, 2 * blk), F32)],
        compiler_params=_params(("parallel", "arbitrary"), est),
        name="swa_attention",
    )(rel_bias.astype(F32).reshape(-1), sinks.astype(F32).reshape(-1), bucket, q, kv, kv, kv, kv)


def kernel(x, norm_mix_pre, norm_mix_post, norm_ffn_pre, norm_ffn_post, ssm_w_in, ssm_conv_w, ssm_conv_b, ssm_dt_bias, ssm_a_log, ssm_d, ssm_norm, ssm_w_out, kv_norm, w_kv, b_kv, attn_w_q, attn_b_q, attn_sinks, attn_w_o, attn_b_o, rel_bias, ffn_w_up, ffn_conv_w, ffn_conv_b, ffn_w_down):
    bsz, s, d = x.shape
    assert bsz == 1 and norm_mix_pre.shape[0] == 2
    d_inner = ssm_norm.shape[-1]
    n_heads = ssm_dt_bias.shape[-1]
    zxbc_dim = ssm_w_in.shape[-1] - n_heads

    def ffn(h_in_u, layer):
        hff = _ffn_up(h_in_u, ffn_w_up[layer], ffn_conv_w[layer], ffn_conv_b[layer])
        return _matmul(hff, ffn_w_down[layer].astype(BF16), None, F32, 512, 256, "ffn_down")

    h0 = x.reshape(s, d)

    u = _prenorm(h0, norm_mix_pre[0])
    w_in = ssm_w_in[0]
    zxbc = _matmul(u, w_in, None, F32, 1024, 512, "in_proj", n=zxbc_dim)
    dt_raw = _matmul(u, w_in, None, F32, 1024, 128, "dt_proj", col0=zxbc_dim, n=n_heads)
    y = _ssd(zxbc, dt_raw, ssm_conv_w[0], ssm_conv_b[0], ssm_dt_bias[0], ssm_a_log[0], ssm_d[0], ssm_norm[0],
             d_inner, n_heads)
    mix = _matmul(y, ssm_w_out[0], None, F32, 1024, 256, "out_proj", a_buffers=1)
    h1, u = _resnorm(h0, mix, norm_mix_post[0], [norm_ffn_pre[0]])
    f = ffn(u, 0)

    h2, ukv, uq = _resnorm(h1, f, norm_ffn_post[0], [kv_norm, norm_mix_pre[1]])
    kv = _matmul(ukv, w_kv, b_kv, F32, 1024, 512, "kv_proj")
    q = _matmul(uq, attn_w_q[0], attn_b_q[0], BF16, 1024, 512, "q_proj")
    o = _attention(q, kv, attn_sinks[0], rel_bias)
    mix = _matmul(o, attn_w_o[0], attn_b_o[0], F32, 1024, 512, "o_proj")
    h3, u = _resnorm(h2, mix, norm_mix_post[1], [norm_ffn_pre[1]])
    f = ffn(u, 1)
    (h4,) = _resnorm(h3, f, norm_ffn_post[1], [])
    return h4.reshape(bsz, s, d)
```

```python
import functools
import math

import numpy as np
import jax
import jax.numpy as jnp
from jax import lax
from jax.experimental import pallas as pl
from jax.experimental.pallas import tpu as pltpu

EPS = 1e-6
SSM_NORM_EPS = 1e-5
SSM_HEAD_DIM = 64
SSM_GROUPS = 8
D_STATE = 128
SSM_CONV = 4
SSM_CHUNK = 128
ATTN_HEAD_DIM = 64
N_KV_HEADS = 8
WINDOW = 128
ATTN_BLOCK = 128
N_BUCKETS = 32
MAX_DISTANCE = 128
FFN_CONV = 3

LANES = 128
SUBLANES = 8
VMEM_CAP_BYTES = 60 * 1024 * 1024

F32 = jnp.float32
BF16 = jnp.bfloat16
HIGHEST = lax.Precision.HIGHEST


def _vmem_limit(est_bytes):
    return int(min(VMEM_CAP_BYTES, max(32 * 1024 * 1024, est_bytes * 5 // 4 + (4 << 20))))


def _params(semantics, est_bytes, flags=None):
    return pltpu.CompilerParams(dimension_semantics=semantics, vmem_limit_bytes=_vmem_limit(est_bytes), flags=flags)


def _tile(dim, pref, align):
    if dim <= pref:
        return dim
    t = (pref // align) * align
    while t >= align:
        if dim % t == 0:
            return t
        t -= align
    raise ValueError(f"no tile for {dim} (pref {pref}, align {align})")


def _sigmoid(x):
    return 1.0 / (1.0 + jnp.exp(-x))


def _softplus(x):
    return jnp.maximum(x, 0.0) + jnp.log1p(jnp.exp(-jnp.abs(x)))


def _rms(x, g, eps):
    ms = jnp.mean(x * x, axis=-1, keepdims=True)
    return (x * lax.rsqrt(ms + eps)) * g


def _prenorm_kernel(x_ref, g_ref, u_ref):
    u_ref[...] = _rms(x_ref[...], g_ref[...], EPS).astype(u_ref.dtype)


def _prenorm(x, g):
    s, d = x.shape
    tr = _tile(s, 256, SUBLANES)
    est = 2 * tr * d * (4 + 2)
    return pl.pallas_call(
        _prenorm_kernel,
        grid=(s // tr,),
        in_specs=[pl.BlockSpec((tr, d), lambda i: (i, 0)), pl.BlockSpec((1, d), lambda i: (0, 0))],
        out_specs=pl.BlockSpec((tr, d), lambda i: (i, 0)),
        out_shape=jax.ShapeDtypeStruct((s, d), BF16),
        compiler_params=_params(("parallel",), est),
        name="prenorm",
    )(x, g.reshape(1, d))


def _resnorm_kernel(r_ref, m_ref, gpost_ref, *rest, n_u):
    g_refs = rest[:n_u]
    h_ref = rest[n_u]
    u_refs = rest[n_u + 1:]
    h = r_ref[...] + _rms(m_ref[...], gpost_ref[...], EPS)
    h_ref[...] = h
    if n_u:
        ms = jnp.mean(h * h, axis=-1, keepdims=True)
        hn = h * lax.rsqrt(ms + EPS)
        for g_ref, u_ref in zip(g_refs, u_refs):
            u_ref[...] = (hn * g_ref[...]).astype(u_ref.dtype)


def _resnorm(r, m, gpost, gains):
    s, d = r.shape
    n_u = len(gains)
    tr = _tile(s, 128, SUBLANES)
    est = 2 * tr * d * (4 * 3 + 2 * n_u)
    row = pl.BlockSpec((tr, d), lambda i: (i, 0))
    vec = pl.BlockSpec((1, d), lambda i: (0, 0))
    outs = pl.pallas_call(
        functools.partial(_resnorm_kernel, n_u=n_u),
        grid=(s // tr,),
        in_specs=[row, row, vec] + [vec] * n_u,
        out_specs=[row] + [row] * n_u,
        out_shape=[jax.ShapeDtypeStruct((s, d), F32)] + [jax.ShapeDtypeStruct((s, d), BF16)] * n_u,
        compiler_params=_params(("parallel",), est),
        name="resnorm",
    )(r, m, gpost.reshape(1, d), *[g.reshape(1, d) for g in gains])
    return outs


def _matmul_kernel(a_ref, w_ref, *rest, has_bias):
    o_ref = rest[-1]
    acc = jnp.dot(a_ref[...], w_ref[...].astype(BF16), preferred_element_type=F32)
    if has_bias:
        acc = acc + rest[0][...]
    o_ref[...] = acc.astype(o_ref.dtype)


def _matmul(a, w, bias, out_dtype, tm_pref, tn_pref, name, a_buffers=2, col0=0, n=None, layer=None):
    m, k = a.shape
    assert (w.ndim == 3) == (layer is not None)
    n = w.shape[-1] if n is None else n
    tm = _tile(m, tm_pref, SUBLANES)
    tn = _tile(math.gcd(n, col0) if col0 else n, tn_pref, LANES)
    assert n % tn == 0 and col0 % tn == 0
    joff = col0 // tn
    osz = jnp.dtype(out_dtype).itemsize
    wsz = jnp.dtype(w.dtype).itemsize
    est = a_buffers * tm * k * 2 + 2 * (k * tn * wsz + tm * tn * osz) + tm * tn * 4 + (k * tn * 2 if wsz != 2 else 0)
    a_mode = {} if a_buffers == 2 else {"pipeline_mode": pl.Buffered(a_buffers)}
    if layer is None:
        w_spec = pl.BlockSpec((k, tn), lambda i, j: (0, j + joff))
    else:
        w_spec = pl.BlockSpec((None, k, tn), lambda i, j: (layer, 0, j + joff))
    in_specs = [pl.BlockSpec((tm, k), lambda i, j: (i, 0), **a_mode), w_spec]
    args = [a, w]
    if bias is not None:
        in_specs.append(pl.BlockSpec((1, tn), lambda i, j: (0, j)))
        args.append(bias.reshape(1, n).astype(F32))
    return pl.pallas_call(
        functools.partial(_matmul_kernel, has_bias=bias is not None),
        grid=(m // tm, n // tn),
        in_specs=in_specs,
        out_specs=pl.BlockSpec((tm, tn), lambda i, j: (i, j)),
        out_shape=jax.ShapeDtypeStruct((m, n), out_dtype),
        compiler_params=_params(("parallel", "parallel"), est),
        name=name,
    )(*args)


FFN_EPILOGUE_ROWS = 128


FFN_ROW_SPLIT = 1


def _ffn_up_kernel(u_ref, wg_ref, wv_ref, cwg_ref, cwv_ref, cbg_ref, cbv_ref, o_ref, halo, wcat):
    i = pl.program_id(0)
    j = pl.program_id(1)
    tm = u_ref.shape[0]
    tn = wg_ref.shape[1]
    part = tm // FFN_ROW_SPLIT
    rows = min(FFN_EPILOGUE_ROWS, part)

    @pl.when(i == 0)
    def _():
        halo[j] = jnp.zeros(halo.shape[1:], F32)

    wcat[:, 0:tn] = wg_ref[...].astype(BF16)
    wcat[:, tn:2 * tn] = wv_ref[...].astype(BF16)
    parts = [jnp.dot(u_ref[p * part:(p + 1) * part, :], wcat[...], preferred_element_type=F32)
             for p in range(FFN_ROW_SPLIT)]

    cw = jnp.concatenate([cwg_ref[...], cwv_ref[...]], axis=1)
    cb = jnp.concatenate([cbg_ref[...], cbv_ref[...]], axis=1)
    prev = halo[j]
    for p in range(FFN_ROW_SPLIT):
        for c in range(part // rows):
            cur = parts[p][c * rows:(c + 1) * rows, :]
            both = jnp.concatenate([prev, cur], axis=0)
            acc = cb + both[SUBLANES - 2:SUBLANES - 2 + rows, :] * cw[0:1, :]
            acc = acc + both[SUBLANES - 1:SUBLANES - 1 + rows, :] * cw[1:2, :]
            acc = acc + cur * cw[2:3, :]
            g = acc[:, 0:tn]
            v = acc[:, tn:2 * tn]
            r0 = p * part + c * rows
            o_ref[r0:r0 + rows, :] = ((g * _sigmoid(g)) * v).astype(o_ref.dtype)
            prev = cur[rows - SUBLANES:rows, :]
    halo[j] = prev


def _ffn_up(u, w_up, conv_w, conv_b, layer):
    s, d = u.shape
    f = w_up.shape[-1] // 2
    tm = _tile(s, 1024, FFN_EPILOGUE_ROWS)
    tn = _tile(f, 256, LANES)
    nj = f // tn
    wsz = jnp.dtype(w_up.dtype).itemsize
    est = 2 * (tm * d * 2 + 2 * d * tn * wsz + tm * tn * 2) + d * 2 * tn * 2 + nj * SUBLANES * 2 * tn * 4 \
        + 3 * tm * 2 * tn * 4
    cb = conv_b.reshape(conv_b.shape[0], 1, 2 * f)
    return pl.pallas_call(
        _ffn_up_kernel,
        grid=(s // tm, nj),
        in_specs=[
            pl.BlockSpec((tm, d), lambda i, j: (i, 0)),
            pl.BlockSpec((None, d, tn), lambda i, j: (layer, 0, j)),
            pl.BlockSpec((None, d, tn), lambda i, j: (layer, 0, j + nj)),
            pl.BlockSpec((None, FFN_CONV, tn), lambda i, j: (layer, 0, j)),
            pl.BlockSpec((None, FFN_CONV, tn), lambda i, j: (layer, 0, j + nj)),
            pl.BlockSpec((None, 1, tn), lambda i, j: (layer, 0, j)),
            pl.BlockSpec((None, 1, tn), lambda i, j: (layer, 0, j + nj)),
        ],
        out_specs=pl.BlockSpec((tm, tn), lambda i, j: (i, j)),
        out_shape=jax.ShapeDtypeStruct((s, f), BF16),
        scratch_shapes=[
            pltpu.VMEM((nj, SUBLANES, 2 * tn), F32),
            pltpu.VMEM((d, 2 * tn), BF16),
        ],
        compiler_params=_params(("arbitrary", "arbitrary"), est),
        name="ffn_up",
    )(u, w_up, w_up, conv_w, conv_w, cb, cb)


def _ssd_kernel(z_ref, x_ref, b_ref, c_ref, dt_ref, dtt_ref,
                cwx_ref, cwb_ref, cwc_ref, cbx_ref, cbb_ref, cbc_ref,
                dtb_row_ref, dtb_col_ref, alog_row_ref, alog_col_ref, dskip_ref, gnorm_ref,
                y_ref, state_ref, ext_x, ext_b, ext_c, ydiag_ref, *, heads):
    chunk = pl.program_id(1)
    cl = SSM_CHUNK
    width = heads * SSM_HEAD_DIM

    @pl.when(chunk == 0)
    def _():
        state_ref[...] = jnp.zeros(state_ref.shape, F32)
        ext_x[0:SUBLANES, :] = jnp.zeros((SUBLANES, ext_x.shape[1]), F32)
        ext_b[0:SUBLANES, :] = jnp.zeros((SUBLANES, ext_b.shape[1]), F32)
        ext_c[0:SUBLANES, :] = jnp.zeros((SUBLANES, ext_c.shape[1]), F32)

    def conv_silu(cur_ref, ext, cw_ref, cb_ref):
        cur = cur_ref[...]
        ext[SUBLANES:SUBLANES + cl, :] = cur
        acc = cb_ref[...] + ext[pl.ds(SUBLANES - 3, cl), :] * cw_ref[0:1, :]
        acc = acc + ext[pl.ds(SUBLANES - 2, cl), :] * cw_ref[1:2, :]
        acc = acc + ext[pl.ds(SUBLANES - 1, cl), :] * cw_ref[2:3, :]
        acc = acc + cur * cw_ref[3:4, :]
        ext[0:SUBLANES, :] = cur[cl - SUBLANES:cl, :]
        return acc * _sigmoid(acc)

    xs = conv_silu(x_ref, ext_x, cwx_ref, cbx_ref)
    bc = conv_silu(b_ref, ext_b, cwb_ref, cbb_ref)
    cc = conv_silu(c_ref, ext_c, cwc_ref, cbc_ref)

    li = lax.broadcasted_iota(jnp.int32, (cl, cl), 0)
    si = lax.broadcasted_iota(jnp.int32, (cl, cl), 1)
    tril = li >= si

    dtv = _softplus(dt_ref[...] + dtb_row_ref[...])
    a = dtv * (-jnp.exp(alog_row_ref[...]))
    acum = jnp.dot(tril.astype(F32), a, precision=HIGHEST, preferred_element_type=F32)
    dtv_t = _softplus(dtt_ref[...] + dtb_col_ref[...])
    a_t = dtv_t * (-jnp.exp(alog_col_ref[...]))
    acum_t = jnp.dot(a_t, (li <= si).astype(F32), precision=HIGHEST, preferred_element_type=F32)
    alast = acum[cl - 1:cl, :]
    exp_acum = jnp.exp(acum)
    decay_end = jnp.exp(alast - acum)

    er = lax.broadcasted_iota(jnp.int32, (LANES, width), 0)
    ec = lax.broadcasted_iota(jnp.int32, (LANES, width), 1)
    expand = (er == (ec >> int(math.log2(SSM_HEAD_DIM)))).astype(F32)

    def widen(v):
        return jnp.dot(v, expand, precision=HIGHEST, preferred_element_type=F32)

    dt_w = widen(dtv)
    exp_acum_w = widen(exp_acum)
    decay_end_w = widen(decay_end)

    xdt = xs * dt_w
    cb = lax.dot_general(cc.astype(BF16), bc.astype(BF16), (((1,), (1,)), ((), ())),
                         preferred_element_type=F32)

    lane = lax.broadcasted_iota(jnp.int32, (cl, LANES), 1)
    low = lane < SSM_HEAD_DIM
    for p in range(heads // 2):
        ms = []
        for hp in range(2):
            r = 2 * p + hp
            seg = acum[:, r:r + 1] - acum_t[r:r + 1, :]
            decay = jnp.exp(jnp.where(tril, seg, -jnp.inf))
            ms.append((cb * decay).astype(BF16))
        m_pair = jnp.concatenate(ms, axis=1)
        xp = xdt[:, p * LANES:(p + 1) * LANES]
        rhs = jnp.concatenate([jnp.where(low, xp, 0.0), jnp.where(low, 0.0, xp)], axis=0).astype(BF16)
        ydiag_ref[:, p * LANES:(p + 1) * LANES] = jnp.dot(m_pair, rhs, preferred_element_type=F32)

    state = state_ref[...]
    y_off = jnp.dot(cc.astype(BF16), state.astype(BF16), preferred_element_type=F32) * exp_acum_w
    y = ydiag_ref[...] + y_off + dskip_ref[...] * xs

    xw = (xdt * decay_end_w).astype(BF16)
    upd = jnp.dot(jnp.transpose(bc).astype(BF16), xw, preferred_element_type=F32)
    state_ref[...] = state * exp_acum_w[cl - 1:cl, :] + upd

    zb = z_ref[...]
    yz = y * (zb * _sigmoid(zb))
    ms2 = jnp.mean(yz * yz, axis=-1, keepdims=True)
    y_ref[...] = ((yz * lax.rsqrt(ms2 + SSM_NORM_EPS)) * gnorm_ref[...]).astype(y_ref.dtype)


def _ssd(zxbc, dt_raw, conv_w, conv_b, dt_bias, a_log, d_skip, g_norm, d_inner, n_heads):
    s = zxbc.shape[0]
    g = SSM_GROUPS
    heads = n_heads // g
    width = d_inner // g
    assert width == heads * SSM_HEAD_DIM and heads % 2 == 0 and heads <= LANES and width % LANES == 0
    assert D_STATE == LANES and s % SSM_CHUNK == 0
    hp = max(SUBLANES, heads)
    nc = s // SSM_CHUNK
    cl = SSM_CHUNK

    dt_g = dt_raw.reshape(s, g, heads).transpose(1, 0, 2)
    dt_row = jnp.pad(dt_g, ((0, 0), (0, 0), (0, LANES - heads)))
    dt_col = jnp.pad(dt_g.transpose(0, 2, 1), ((0, 0), (0, hp - heads), (0, 0)))

    def rowvec(v):
        return jnp.pad(v.astype(F32).reshape(g, 1, heads), ((0, 0), (0, 0), (0, LANES - heads)))

    def colvec(v):
        return jnp.pad(v.astype(F32).reshape(g, heads, 1), ((0, 0), (0, hp - heads), (0, 0)))

    dskip_w = jnp.repeat(d_skip.astype(F32), SSM_HEAD_DIM).reshape(1, d_inner)
    cbias = conv_b.reshape(1, -1)
    wb = width // LANES
    xoff = d_inner // width
    boff = 2 * d_inner // LANES
    coff = boff + g * D_STATE // LANES
    cwb_off = d_inner // LANES
    cwc_off = cwb_off + g * D_STATE // LANES

    est = 2 * (2 * cl * width * 4 + 2 * cl * LANES * 4 + cl * LANES * 4 + hp * cl * 4 + cl * width * 2) \
        + D_STATE * width * 4 + 3 * (cl + SUBLANES) * width * 4 + cl * width * 4 + 16 * cl * width * 4
    return pl.pallas_call(
        functools.partial(_ssd_kernel, heads=heads),
        grid=(g, nc),
        in_specs=[
            pl.BlockSpec((cl, width), lambda gi, c: (c, gi)),
            pl.BlockSpec((cl, width), lambda gi, c: (c, xoff + gi)),
            pl.BlockSpec((cl, D_STATE), lambda gi, c: (c, boff + gi)),
            pl.BlockSpec((cl, D_STATE), lambda gi, c: (c, coff + gi)),
            pl.BlockSpec((None, cl, LANES), lambda gi, c: (gi, c, 0)),
            pl.BlockSpec((None, hp, cl), lambda gi, c: (gi, 0, c)),
            pl.BlockSpec((SSM_CONV, width), lambda gi, c: (0, gi)),
            pl.BlockSpec((SSM_CONV, D_STATE), lambda gi, c: (0, cwb_off + gi)),
            pl.BlockSpec((SSM_CONV, D_STATE), lambda gi, c: (0, cwc_off + gi)),
            pl.BlockSpec((1, width), lambda gi, c: (0, gi)),
            pl.BlockSpec((1, D_STATE), lambda gi, c: (0, cwb_off + gi)),
            pl.BlockSpec((1, D_STATE), lambda gi, c: (0, cwc_off + gi)),
            pl.BlockSpec((None, 1, LANES), lambda gi, c: (gi, 0, 0)),
            pl.BlockSpec((None, hp, 1), lambda gi, c: (gi, 0, 0)),
            pl.BlockSpec((None, 1, LANES), lambda gi, c: (gi, 0, 0)),
            pl.BlockSpec((None, hp, 1), lambda gi, c: (gi, 0, 0)),
            pl.BlockSpec((1, width), lambda gi, c: (0, gi)),
            pl.BlockSpec((1, width), lambda gi, c: (0, gi)),
        ],
        out_specs=pl.BlockSpec((cl, width), lambda gi, c: (c, gi)),
        out_shape=jax.ShapeDtypeStruct((s, d_inner), BF16),
        scratch_shapes=[
            pltpu.VMEM((D_STATE, width), F32),
            pltpu.VMEM((cl + SUBLANES, width), F32),
            pltpu.VMEM((cl + SUBLANES, D_STATE), F32),
            pltpu.VMEM((cl + SUBLANES, D_STATE), F32),
            pltpu.VMEM((cl, width), F32),
        ],
        compiler_params=_params(("parallel", "arbitrary"), est),
        name="ssd_scan",
    )(zxbc, zxbc, zxbc, zxbc, dt_row, dt_col, conv_w, conv_w, conv_w, cbias, cbias, cbias,
      rowvec(dt_bias), colvec(dt_bias), rowvec(a_log), colvec(a_log), dskip_w, g_norm.reshape(1, d_inner))


def _t5_bucket_table():
    q = np.arange(ATTN_BLOCK)[:, None]
    k = np.arange(2 * ATTN_BLOCK)[None, :]
    rel = np.maximum(q - k + ATTN_BLOCK, 0)
    max_exact = N_BUCKETS // 2
    relf = np.maximum(rel, 1).astype(np.float32)
    large = max_exact + (np.log(relf / np.float32(max_exact)) / np.float32(math.log(MAX_DISTANCE / max_exact))
                         * np.float32(N_BUCKETS - max_exact)).astype(np.int32)
    large = np.minimum(large, N_BUCKETS - 1)
    return np.where(rel < max_exact, rel, large).astype(np.int32)


def _attn_kernel(relb_ref, sink_ref, bucket_ref, q_ref, kp_ref, kc_ref, vp_ref, vc_ref, o_ref, bias_ref,
                 *, qpk, n_q_heads):
    gp = pl.program_id(0)
    n = pl.program_id(1)
    blk = ATTN_BLOCK
    heads_here = 2 * qpk
    head0 = gp * heads_here

    @pl.when(n == 0)
    def _():
        bucket = bucket_ref[...]

        def body(hh, carry):
            acc = jnp.zeros((blk, 2 * blk), F32)
            for b in range(N_BUCKETS):
                acc = jnp.where(bucket == b, relb_ref[b * n_q_heads + head0 + hh], acc)
            bias_ref[hh] = acc
            return carry

        lax.fori_loop(0, heads_here, body, 0)

    qi = lax.broadcasted_iota(jnp.int32, (blk, 2 * blk), 0)
    ci = lax.broadcasted_iota(jnp.int32, (blk, 2 * blk), 1)
    rel = qi - ci + blk
    valid = (rel >= 0) & (rel < WINDOW) & ((ci >= blk) | (n > 0))

    kk = jnp.concatenate([kp_ref[...], kc_ref[...]], axis=0)
    vv = jnp.concatenate([vp_ref[...], vc_ref[...]], axis=0)
    lane = lax.broadcasted_iota(jnp.int32, (2 * blk, LANES), 1)
    low = lane < ATTN_HEAD_DIM
    scale = ATTN_HEAD_DIM ** -0.5

    def pair_operand(t, kvh, mult):
        sel = jnp.where(low if kvh == 0 else jnp.logical_not(low), t, 0.0)
        dup = sel + pltpu.roll(sel, ATTN_HEAD_DIM, axis=1)
        if mult is not None:
            dup = dup * mult
        return jnp.concatenate([jnp.where(low, dup, 0.0), jnp.where(low, 0.0, dup)], axis=0).astype(BF16)

    for kvh in range(2):
        kbd = pair_operand(kk, kvh, scale)
        vbd = pair_operand(vv, kvh, None)
        for qp in range(qpk // 2):
            pidx = kvh * (qpk // 2) + qp
            qpair = q_ref[:, pidx * LANES:(pidx + 1) * LANES]
            s = lax.dot_general(qpair, kbd, (((1,), (1,)), ((), ())), preferred_element_type=F32)
            ps = []
            for hp in range(2):
                hh = pidx * 2 + hp
                sink = sink_ref[head0 + hh]
                sh = jnp.where(valid, s[:, hp * 2 * blk:(hp + 1) * 2 * blk] + bias_ref[hh], -jnp.inf)
                m = jnp.maximum(jnp.max(sh, axis=-1, keepdims=True), sink)
                e = jnp.exp(sh - m)
                denom = jnp.sum(e, axis=-1, keepdims=True) + jnp.exp(sink - m)
                ps.append((e / denom).astype(BF16))
            p = jnp.concatenate(ps, axis=1)
            o_ref[:, pidx * LANES:(pidx + 1) * LANES] = jnp.dot(
                p, vbd, preferred_element_type=F32).astype(o_ref.dtype)


def _attention(q, kv, sinks, rel_bias):
    s, qd = q.shape
    n_q_heads = qd // ATTN_HEAD_DIM
    qpk = n_q_heads // N_KV_HEADS
    assert qpk % 2 == 0 and N_KV_HEADS % 2 == 0 and 2 * ATTN_HEAD_DIM == LANES and s % ATTN_BLOCK == 0
    blk = ATTN_BLOCK
    nb = s // blk
    ngp = N_KV_HEADS // 2
    qw = 2 * qpk * ATTN_HEAD_DIM
    voff = N_KV_HEADS * ATTN_HEAD_DIM // LANES
    bucket = jnp.asarray(_t5_bucket_table())
    est = 2 * (2 * blk * qw * 2 + 4 * blk * LANES * 4 + blk * 2 * blk * 4) + 2 * qpk * blk * 2 * blk * 4 \
        + 24 * blk * 4 * blk * 4
    smem = pl.BlockSpec(memory_space=pltpu.SMEM)
    return pl.pallas_call(
        functools.partial(_attn_kernel, qpk=qpk, n_q_heads=n_q_heads),
        grid=(ngp, nb),
        in_specs=[
            smem, smem,
            pl.BlockSpec((blk, 2 * blk), lambda g, n: (0, 0)),
            pl.BlockSpec((blk, qw), lambda g, n: (n, g)),
            pl.BlockSpec((blk, LANES), lambda g, n: (jnp.maximum(n - 1, 0), g)),
            pl.BlockSpec((blk, LANES), lambda g, n: (n, g)),
            pl.BlockSpec((blk, LANES), lambda g, n: (jnp.maximum(n - 1, 0), voff + g)),
            pl.BlockSpec((blk, LANES), lambda g, n: (n, voff + g)),
        ],
        out_specs=pl.BlockSpec((blk, qw), lambda g, n: (n, g)),
        out_shape=jax.ShapeDtypeStruct((s, qd), BF16),
        scratch_shapes=[pltpu.VMEM((2 * qpk, blk, 2 * blk), F32)],
        compiler_params=_params(("parallel", "arbitrary"), est),
        name="swa_attention",
    )(rel_bias.astype(F32).reshape(-1), sinks.astype(F32).reshape(-1), bucket, q, kv, kv, kv, kv)


def kernel(x, norm_mix_pre, norm_mix_post, norm_ffn_pre, norm_ffn_post, ssm_w_in, ssm_conv_w, ssm_conv_b, ssm_dt_bias, ssm_a_log, ssm_d, ssm_norm, ssm_w_out, kv_norm, w_kv, b_kv, attn_w_q, attn_b_q, attn_sinks, attn_w_o, attn_b_o, rel_bias, ffn_w_up, ffn_conv_w, ffn_conv_b, ffn_w_down):
    bsz, s, d = x.shape
    assert bsz == 1 and norm_mix_pre.shape[0] == 2
    d_inner = ssm_norm.shape[-1]
    n_heads = ssm_dt_bias.shape[-1]
    zxbc_dim = ssm_w_in.shape[-1] - n_heads

    def ffn(h_in_u, layer):
        hff = _ffn_up(h_in_u, ffn_w_up, ffn_conv_w, ffn_conv_b, layer)
        return _matmul(hff, ffn_w_down, None, F32, 1024, 256, "ffn_down", a_buffers=1, layer=layer)

    h0 = x.reshape(s, d)

    u = _prenorm(h0, norm_mix_pre[0])
    zxbc = _matmul(u, ssm_w_in, None, F32, 1024, 512, "in_proj", n=zxbc_dim, layer=0)
    dt_raw = _matmul(u, ssm_w_in, None, F32, 1024, 128, "dt_proj", col0=zxbc_dim, n=n_heads, layer=0)
    y = _ssd(zxbc, dt_raw, ssm_conv_w[0], ssm_conv_b[0], ssm_dt_bias[0], ssm_a_log[0], ssm_d[0], ssm_norm[0],
             d_inner, n_heads)
    mix = _matmul(y, ssm_w_out, None, F32, 1024, 256, "out_proj", a_buffers=1, layer=0)
    h1, u = _resnorm(h0, mix, norm_mix_post[0], [norm_ffn_pre[0]])
    f = ffn(u, 0)

    h2, ukv, uq = _resnorm(h1, f, norm_ffn_post[0], [kv_norm, norm_mix_pre[1]])
    kv = _matmul(ukv, w_kv, b_kv, F32, 1024, 512, "kv_proj")
    q = _matmul(uq, attn_w_q, attn_b_q[0], BF16, 1024, 512, "q_proj", layer=0)
    o = _attention(q, kv, attn_sinks[0], rel_bias)
    mix = _matmul(o, attn_w_o, attn_b_o[0], F32, 1024, 512, "o_proj", layer=0)
    h3, u = _resnorm(h2, mix, norm_mix_post[1], [norm_ffn_pre[1]])
    f = ffn(u, 1)
    (h4,) = _resnorm(h3, f, norm_ffn_post[1], [])
    return h4.reshape(bsz, s, d)
```

```python
import functools
import math

import numpy as np
import jax
import jax.numpy as jnp
from jax import lax
from jax.experimental import pallas as pl
from jax.experimental.pallas import tpu as pltpu

EPS = 1e-6
SSM_NORM_EPS = 1e-5
SSM_HEAD_DIM = 64
SSM_GROUPS = 8
D_STATE = 128
SSM_CONV = 4
SSM_CHUNK = 128
ATTN_HEAD_DIM = 64
N_KV_HEADS = 8
WINDOW = 128
ATTN_BLOCK = 128
N_BUCKETS = 32
MAX_DISTANCE = 128
FFN_CONV = 3

LANES = 128
SUBLANES = 8
VMEM_CAP_BYTES = 60 * 1024 * 1024

F32 = jnp.float32
BF16 = jnp.bfloat16
HIGHEST = lax.Precision.HIGHEST


def _vmem_limit(est_bytes):
    return int(min(VMEM_CAP_BYTES, max(32 * 1024 * 1024, est_bytes * 5 // 4 + (4 << 20))))


def _params(semantics, est_bytes, flags=None):
    return pltpu.CompilerParams(dimension_semantics=semantics, vmem_limit_bytes=_vmem_limit(est_bytes), flags=flags)


def _tile(dim, pref, align):
    if dim <= pref:
        return dim
    t = (pref // align) * align
    while t >= align:
        if dim % t == 0:
            return t
        t -= align
    raise ValueError(f"no tile for {dim} (pref {pref}, align {align})")


def _sigmoid(x):
    return 0.5 * jnp.tanh(0.5 * x) + 0.5


def _softplus(x):
    return jnp.maximum(x, 0.0) + jnp.log1p(jnp.exp(-jnp.abs(x)))


def _split3(v):
    hi = v.astype(BF16)
    r1 = v - hi.astype(F32)
    mid = r1.astype(BF16)
    lo = (r1 - mid.astype(F32)).astype(BF16)
    return [hi, mid, lo]


def _rms(x, g, eps):
    ms = jnp.mean(x * x, axis=-1, keepdims=True)
    return (x * lax.rsqrt(ms + eps)) * g


def _prenorm_kernel(x_ref, g_ref, u_ref):
    u_ref[...] = _rms(x_ref[...], g_ref[...], EPS).astype(u_ref.dtype)


def _prenorm(x, g):
    s, d = x.shape
    tr = _tile(s, 256, SUBLANES)
    est = 2 * tr * d * (4 + 2)
    return pl.pallas_call(
        _prenorm_kernel,
        grid=(s // tr,),
        in_specs=[pl.BlockSpec((tr, d), lambda i: (i, 0)), pl.BlockSpec((1, d), lambda i: (0, 0))],
        out_specs=pl.BlockSpec((tr, d), lambda i: (i, 0)),
        out_shape=jax.ShapeDtypeStruct((s, d), BF16),
        compiler_params=_params(("parallel",), est),
        name="prenorm",
    )(x, g.reshape(1, d))


def _resnorm_kernel(r_ref, m_ref, gpost_ref, *rest, n_u):
    g_refs = rest[:n_u]
    h_ref = rest[n_u]
    u_refs = rest[n_u + 1:]
    h = r_ref[...] + _rms(m_ref[...], gpost_ref[...], EPS)
    h_ref[...] = h
    if n_u:
        ms = jnp.mean(h * h, axis=-1, keepdims=True)
        hn = h * lax.rsqrt(ms + EPS)
        for g_ref, u_ref in zip(g_refs, u_refs):
            u_ref[...] = (hn * g_ref[...]).astype(u_ref.dtype)


def _resnorm(r, m, gpost, gains):
    s, d = r.shape
    n_u = len(gains)
    tr = _tile(s, 128, SUBLANES)
    est = 2 * tr * d * (4 * 3 + 2 * n_u)
    row = pl.BlockSpec((tr, d), lambda i: (i, 0))
    vec = pl.BlockSpec((1, d), lambda i: (0, 0))
    outs = pl.pallas_call(
        functools.partial(_resnorm_kernel, n_u=n_u),
        grid=(s // tr,),
        in_specs=[row, row, vec] + [vec] * n_u,
        out_specs=[row] + [row] * n_u,
        out_shape=[jax.ShapeDtypeStruct((s, d), F32)] + [jax.ShapeDtypeStruct((s, d), BF16)] * n_u,
        compiler_params=_params(("parallel",), est),
        name="resnorm",
    )(r, m, gpost.reshape(1, d), *[g.reshape(1, d) for g in gains])
    return outs


def _matmul_kernel(a_ref, w_ref, *rest, has_bias):
    o_ref = rest[-1]
    acc = jnp.dot(a_ref[...], w_ref[...].astype(BF16), preferred_element_type=F32)
    if has_bias:
        acc = acc + rest[0][...]
    o_ref[...] = acc.astype(o_ref.dtype)


def _matmul(a, w, bias, out_dtype, tm_pref, tn_pref, name, a_buffers=2, col0=0, n=None, layer=None):
    m, k = a.shape
    assert (w.ndim == 3) == (layer is not None)
    n = w.shape[-1] if n is None else n
    tm = _tile(m, tm_pref, SUBLANES)
    tn = _tile(math.gcd(n, col0) if col0 else n, tn_pref, LANES)
    assert n % tn == 0 and col0 % tn == 0
    joff = col0 // tn
    osz = jnp.dtype(out_dtype).itemsize
    wsz = jnp.dtype(w.dtype).itemsize
    est = a_buffers * tm * k * 2 + 2 * (k * tn * wsz + tm * tn * osz) + tm * tn * 4 + (k * tn * 2 if wsz != 2 else 0)
    a_mode = {} if a_buffers == 2 else {"pipeline_mode": pl.Buffered(a_buffers)}
    if layer is None:
        w_spec = pl.BlockSpec((k, tn), lambda i, j: (0, j + joff))
    else:
        w_spec = pl.BlockSpec((None, k, tn), lambda i, j: (layer, 0, j + joff))
    in_specs = [pl.BlockSpec((tm, k), lambda i, j: (i, 0), **a_mode), w_spec]
    args = [a, w]
    if bias is not None:
        in_specs.append(pl.BlockSpec((1, tn), lambda i, j: (0, j)))
        args.append(bias.reshape(1, n).astype(F32))
    return pl.pallas_call(
        functools.partial(_matmul_kernel, has_bias=bias is not None),
        grid=(m // tm, n // tn),
        in_specs=in_specs,
        out_specs=pl.BlockSpec((tm, tn), lambda i, j: (i, j)),
        out_shape=jax.ShapeDtypeStruct((m, n), out_dtype),
        compiler_params=_params(("parallel", "parallel"), est),
        name=name,
    )(*args)


FFN_EPILOGUE_ROWS = 128


FFN_ROW_SPLIT = 1


def _ffn_up_kernel(u_ref, wg_ref, wv_ref, cwg_ref, cwv_ref, cbg_ref, cbv_ref, o_ref, halo, wcat):
    i = pl.program_id(0)
    j = pl.program_id(1)
    tm = u_ref.shape[0]
    tn = wg_ref.shape[1]
    part = tm // FFN_ROW_SPLIT
    rows = min(FFN_EPILOGUE_ROWS, part)

    @pl.when(i == 0)
    def _():
        halo[j] = jnp.zeros(halo.shape[1:], F32)

    wcat[:, 0:tn] = wg_ref[...].astype(BF16)
    wcat[:, tn:2 * tn] = wv_ref[...].astype(BF16)
    parts = [jnp.dot(u_ref[p * part:(p + 1) * part, :], wcat[...], preferred_element_type=F32)
             for p in range(FFN_ROW_SPLIT)]

    cw = jnp.concatenate([cwg_ref[...], cwv_ref[...]], axis=1)
    cb = jnp.concatenate([cbg_ref[...], cbv_ref[...]], axis=1)
    prev = halo[j]
    for p in range(FFN_ROW_SPLIT):
        for c in range(part // rows):
            cur = parts[p][c * rows:(c + 1) * rows, :]
            both = jnp.concatenate([prev, cur], axis=0)
            acc = cb + both[SUBLANES - 2:SUBLANES - 2 + rows, :] * cw[0:1, :]
            acc = acc + both[SUBLANES - 1:SUBLANES - 1 + rows, :] * cw[1:2, :]
            acc = acc + cur * cw[2:3, :]
            g = acc[:, 0:tn]
            v = acc[:, tn:2 * tn]
            r0 = p * part + c * rows
            o_ref[r0:r0 + rows, :] = ((g * _sigmoid(g)) * v).astype(o_ref.dtype)
            prev = cur[rows - SUBLANES:rows, :]
    halo[j] = prev


def _ffn_up(u, w_up, conv_w, conv_b, layer):
    s, d = u.shape
    f = w_up.shape[-1] // 2
    tm = _tile(s, 1024, FFN_EPILOGUE_ROWS)
    tn = _tile(f, 256, LANES)
    nj = f // tn
    wsz = jnp.dtype(w_up.dtype).itemsize
    est = 2 * (tm * d * 2 + 2 * d * tn * wsz + tm * tn * 2) + d * 2 * tn * 2 + nj * SUBLANES * 2 * tn * 4 \
        + 3 * tm * 2 * tn * 4
    cb = conv_b.reshape(conv_b.shape[0], 1, 2 * f)
    return pl.pallas_call(
        _ffn_up_kernel,
        grid=(s // tm, nj),
        in_specs=[
            pl.BlockSpec((tm, d), lambda i, j: (i, 0)),
            pl.BlockSpec((None, d, tn), lambda i, j: (layer, 0, j)),
            pl.BlockSpec((None, d, tn), lambda i, j: (layer, 0, j + nj)),
            pl.BlockSpec((None, FFN_CONV, tn), lambda i, j: (layer, 0, j)),
            pl.BlockSpec((None, FFN_CONV, tn), lambda i, j: (layer, 0, j + nj)),
            pl.BlockSpec((None, 1, tn), lambda i, j: (layer, 0, j)),
            pl.BlockSpec((None, 1, tn), lambda i, j: (layer, 0, j + nj)),
        ],
        out_specs=pl.BlockSpec((tm, tn), lambda i, j: (i, j)),
        out_shape=jax.ShapeDtypeStruct((s, f), BF16),
        scratch_shapes=[
            pltpu.VMEM((nj, SUBLANES, 2 * tn), F32),
            pltpu.VMEM((d, 2 * tn), BF16),
        ],
        compiler_params=_params(("arbitrary", "arbitrary"), est),
        name="ffn_up",
    )(u, w_up, w_up, conv_w, conv_w, cb, cb)


def _ssd_kernel(z_ref, x_ref, b_ref, c_ref, dt_ref, dtt_ref,
                cwx_ref, cwb_ref, cwc_ref, cbx_ref, cbb_ref, cbc_ref,
                dtb_row_ref, dtb_col_ref, alog_row_ref, alog_col_ref, dskip_ref, gnorm_ref, expand_ref,
                y_ref, state_ref, ext_x, ext_b, ext_c, ydiag_ref, *, heads):
    chunk = pl.program_id(1)
    cl = SSM_CHUNK
    width = heads * SSM_HEAD_DIM

    @pl.when(chunk == 0)
    def _():
        state_ref[...] = jnp.zeros(state_ref.shape, F32)
        ext_x[0:SUBLANES, :] = jnp.zeros((SUBLANES, ext_x.shape[1]), F32)
        ext_b[0:SUBLANES, :] = jnp.zeros((SUBLANES, ext_b.shape[1]), F32)
        ext_c[0:SUBLANES, :] = jnp.zeros((SUBLANES, ext_c.shape[1]), F32)

    def conv_silu(cur_ref, ext, cw_ref, cb_ref):
        cur = cur_ref[...]
        ext[SUBLANES:SUBLANES + cl, :] = cur
        acc = cb_ref[...] + ext[pl.ds(SUBLANES - 3, cl), :] * cw_ref[0:1, :]
        acc = acc + ext[pl.ds(SUBLANES - 2, cl), :] * cw_ref[1:2, :]
        acc = acc + ext[pl.ds(SUBLANES - 1, cl), :] * cw_ref[2:3, :]
        acc = acc + cur * cw_ref[3:4, :]
        ext[0:SUBLANES, :] = cur[cl - SUBLANES:cl, :]
        return acc * _sigmoid(acc)

    xs = conv_silu(x_ref, ext_x, cwx_ref, cbx_ref)
    bc = conv_silu(b_ref, ext_b, cwb_ref, cbb_ref)
    cc = conv_silu(c_ref, ext_c, cwc_ref, cbc_ref)

    li = lax.broadcasted_iota(jnp.int32, (cl, cl), 0)
    si = lax.broadcasted_iota(jnp.int32, (cl, cl), 1)
    tril = li >= si

    dtv = _softplus(dt_ref[...] + dtb_row_ref[...])
    a = dtv * (-jnp.exp(alog_row_ref[...]))
    ones_lo = tril.astype(F32).astype(BF16)
    ones_up = (li <= si).astype(F32).astype(BF16)
    acum = jnp.dot(jnp.concatenate([ones_lo] * 3, axis=1), jnp.concatenate(_split3(a), axis=0),
                   preferred_element_type=F32)
    dtv_t = _softplus(dtt_ref[...] + dtb_col_ref[...])
    a_t = dtv_t * (-jnp.exp(alog_col_ref[...]))
    acum_t = jnp.dot(jnp.concatenate(_split3(a_t), axis=1), jnp.concatenate([ones_up] * 3, axis=0),
                     preferred_element_type=F32)
    alast = acum[cl - 1:cl, :]
    exp_acum = jnp.exp(acum)
    decay_end = jnp.exp(alast - acum)

    def widen(v):
        return jnp.dot(jnp.concatenate(_split3(v), axis=1), expand_ref[...], preferred_element_type=F32)

    dt_w = widen(dtv)
    exp_acum_w = widen(exp_acum)
    decay_end_w = widen(decay_end)

    xdt = xs * dt_w
    cb = lax.dot_general(cc.astype(BF16), bc.astype(BF16), (((1,), (1,)), ((), ())),
                         preferred_element_type=F32)
    cb = jnp.where(tril, cb, 0.0)

    lane = lax.broadcasted_iota(jnp.int32, (cl, LANES), 1)
    low = lane < SSM_HEAD_DIM
    for p in range(heads // 2):
        ms = []
        for hp in range(2):
            r = 2 * p + hp
            seg = acum[:, r:r + 1] - acum_t[r:r + 1, :]
            decay = jnp.exp(jnp.minimum(seg, 0.0))
            ms.append((cb * decay).astype(BF16))
        m_pair = jnp.concatenate(ms, axis=1)
        xp = xdt[:, p * LANES:(p + 1) * LANES]
        rhs = jnp.concatenate([jnp.where(low, xp, 0.0), jnp.where(low, 0.0, xp)], axis=0).astype(BF16)
        ydiag_ref[:, p * LANES:(p + 1) * LANES] = jnp.dot(m_pair, rhs, preferred_element_type=F32)

    state = state_ref[...]
    y_off = jnp.dot(cc.astype(BF16), state.astype(BF16), preferred_element_type=F32) * exp_acum_w
    y = ydiag_ref[...] + y_off + dskip_ref[...] * xs

    xw = (xdt * decay_end_w).astype(BF16)
    upd = jnp.dot(jnp.transpose(bc).astype(BF16), xw, preferred_element_type=F32)
    state_ref[...] = state * exp_acum_w[cl - 1:cl, :] + upd

    zb = z_ref[...]
    yz = y * (zb * _sigmoid(zb))
    ms2 = jnp.mean(yz * yz, axis=-1, keepdims=True)
    y_ref[...] = ((yz * lax.rsqrt(ms2 + SSM_NORM_EPS)) * gnorm_ref[...]).astype(y_ref.dtype)


def _ssd(zxbc, dt_raw, conv_w, conv_b, dt_bias, a_log, d_skip, g_norm, d_inner, n_heads):
    s = zxbc.shape[0]
    g = SSM_GROUPS
    heads = n_heads // g
    width = d_inner // g
    assert width == heads * SSM_HEAD_DIM and heads % 2 == 0 and heads <= LANES and width % LANES == 0
    assert D_STATE == LANES and s % SSM_CHUNK == 0
    hp = max(SUBLANES, heads)
    nc = s // SSM_CHUNK
    cl = SSM_CHUNK

    dt_g = dt_raw.reshape(s, g, heads).transpose(1, 0, 2)
    dt_row = jnp.pad(dt_g, ((0, 0), (0, 0), (0, LANES - heads)))
    dt_col = jnp.pad(dt_g.transpose(0, 2, 1), ((0, 0), (0, hp - heads), (0, 0)))

    def rowvec(v):
        return jnp.pad(v.astype(F32).reshape(g, 1, heads), ((0, 0), (0, 0), (0, LANES - heads)))

    def colvec(v):
        return jnp.pad(v.astype(F32).reshape(g, heads, 1), ((0, 0), (0, hp - heads), (0, 0)))

    dskip_w = jnp.repeat(d_skip.astype(F32), SSM_HEAD_DIM).reshape(1, d_inner)
    expand3 = np.tile(np.arange(LANES)[:, None] == (np.arange(width)[None, :] // SSM_HEAD_DIM), (3, 1))
    cbias = conv_b.reshape(1, -1)
    wb = width // LANES
    xoff = d_inner // width
    boff = 2 * d_inner // LANES
    coff = boff + g * D_STATE // LANES
    cwb_off = d_inner // LANES
    cwc_off = cwb_off + g * D_STATE // LANES

    est = 2 * (2 * cl * width * 4 + 2 * cl * LANES * 4 + cl * LANES * 4 + hp * cl * 4 + cl * width * 2) \
        + D_STATE * width * 4 + 3 * (cl + SUBLANES) * width * 4 + cl * width * 4 + 16 * cl * width * 4
    return pl.pallas_call(
        functools.partial(_ssd_kernel, heads=heads),
        grid=(g, nc),
        in_specs=[
            pl.BlockSpec((cl, width), lambda gi, c: (c, gi)),
            pl.BlockSpec((cl, width), lambda gi, c: (c, xoff + gi)),
            pl.BlockSpec((cl, D_STATE), lambda gi, c: (c, boff + gi)),
            pl.BlockSpec((cl, D_STATE), lambda gi, c: (c, coff + gi)),
            pl.BlockSpec((None, cl, LANES), lambda gi, c: (gi, c, 0)),
            pl.BlockSpec((None, hp, cl), lambda gi, c: (gi, 0, c)),
            pl.BlockSpec((SSM_CONV, width), lambda gi, c: (0, gi)),
            pl.BlockSpec((SSM_CONV, D_STATE), lambda gi, c: (0, cwb_off + gi)),
            pl.BlockSpec((SSM_CONV, D_STATE), lambda gi, c: (0, cwc_off + gi)),
            pl.BlockSpec((1, width), lambda gi, c: (0, gi)),
            pl.BlockSpec((1, D_STATE), lambda gi, c: (0, cwb_off + gi)),
            pl.BlockSpec((1, D_STATE), lambda gi, c: (0, cwc_off + gi)),
            pl.BlockSpec((None, 1, LANES), lambda gi, c: (gi, 0, 0)),
            pl.BlockSpec((None, hp, 1), lambda gi, c: (gi, 0, 0)),
            pl.BlockSpec((None, 1, LANES), lambda gi, c: (gi, 0, 0)),
            pl.BlockSpec((None, hp, 1), lambda gi, c: (gi, 0, 0)),
            pl.BlockSpec((1, width), lambda gi, c: (0, gi)),
            pl.BlockSpec((1, width), lambda gi, c: (0, gi)),
            pl.BlockSpec((3 * LANES, width), lambda gi, c: (0, 0)),
        ],
        out_specs=pl.BlockSpec((cl, width), lambda gi, c: (c, gi)),
        out_shape=jax.ShapeDtypeStruct((s, d_inner), BF16),
        scratch_shapes=[
            pltpu.VMEM((D_STATE, width), F32),
            pltpu.VMEM((cl + SUBLANES, width), F32),
            pltpu.VMEM((cl + SUBLANES, D_STATE), F32),
            pltpu.VMEM((cl + SUBLANES, D_STATE), F32),
            pltpu.VMEM((cl, width), F32),
        ],
        compiler_params=_params(("parallel", "arbitrary"), est),
        name="ssd_scan",
    )(zxbc, zxbc, zxbc, zxbc, dt_row, dt_col, conv_w, conv_w, conv_w, cbias, cbias, cbias,
      rowvec(dt_bias), colvec(dt_bias), rowvec(a_log), colvec(a_log), dskip_w, g_norm.reshape(1, d_inner),
      jnp.asarray(expand3, BF16))


def _t5_bucket_table():
    q = np.arange(ATTN_BLOCK)[:, None]
    k = np.arange(2 * ATTN_BLOCK)[None, :]
    rel = np.maximum(q - k + ATTN_BLOCK, 0)
    max_exact = N_BUCKETS // 2
    relf = np.maximum(rel, 1).astype(np.float32)
    large = max_exact + (np.log(relf / np.float32(max_exact)) / np.float32(math.log(MAX_DISTANCE / max_exact))
                         * np.float32(N_BUCKETS - max_exact)).astype(np.int32)
    large = np.minimum(large, N_BUCKETS - 1)
    return np.where(rel < max_exact, rel, large).astype(np.int32)


def _attn_kernel(relb_ref, sink_ref, bucket_ref, q_ref, kp_ref, kc_ref, vp_ref, vc_ref, o_ref, bias_ref,
                 *, qpk, n_q_heads):
    gp = pl.program_id(0)
    n = pl.program_id(1)
    blk = ATTN_BLOCK
    heads_here = 2 * qpk
    head0 = gp * heads_here

    @pl.when(n == 0)
    def _():
        bucket = bucket_ref[...]

        def body(hh, carry):
            acc = jnp.zeros((blk, 2 * blk), F32)
            for b in range(N_BUCKETS):
                acc = jnp.where(bucket == b, relb_ref[b * n_q_heads + head0 + hh], acc)
            bias_ref[hh] = acc
            return carry

        lax.fori_loop(0, heads_here, body, 0)

    qi = lax.broadcasted_iota(jnp.int32, (blk, 2 * blk), 0)
    ci = lax.broadcasted_iota(jnp.int32, (blk, 2 * blk), 1)
    rel = qi - ci + blk
    valid = (rel >= 0) & (rel < WINDOW) & ((ci >= blk) | (n > 0))

    kk = jnp.concatenate([kp_ref[...], kc_ref[...]], axis=0)
    vv = jnp.concatenate([vp_ref[...], vc_ref[...]], axis=0)
    lane = lax.broadcasted_iota(jnp.int32, (2 * blk, LANES), 1)
    low = lane < ATTN_HEAD_DIM
    scale = ATTN_HEAD_DIM ** -0.5

    def pair_operand(t, kvh, mult):
        sel = jnp.where(low if kvh == 0 else jnp.logical_not(low), t, 0.0)
        dup = sel + pltpu.roll(sel, ATTN_HEAD_DIM, axis=1)
        if mult is not None:
            dup = dup * mult
        return jnp.concatenate([jnp.where(low, dup, 0.0), jnp.where(low, 0.0, dup)], axis=0).astype(BF16)

    for kvh in range(2):
        kbd = pair_operand(kk, kvh, scale)
        vbd = pair_operand(vv, kvh, None)
        for qp in range(qpk // 2):
            pidx = kvh * (qpk // 2) + qp
            qpair = q_ref[:, pidx * LANES:(pidx + 1) * LANES]
            s = lax.dot_general(qpair, kbd, (((1,), (1,)), ((), ())), preferred_element_type=F32)
            ps = []
            for hp in range(2):
                hh = pidx * 2 + hp
                sink = sink_ref[head0 + hh]
                sh = jnp.where(valid, s[:, hp * 2 * blk:(hp + 1) * 2 * blk] + bias_ref[hh], -jnp.inf)
                m = jnp.maximum(jnp.max(sh, axis=-1, keepdims=True), sink)
                e = jnp.exp(sh - m)
                denom = jnp.sum(e, axis=-1, keepdims=True) + jnp.exp(sink - m)
                ps.append((e / denom).astype(BF16))
            p = jnp.concatenate(ps, axis=1)
            o_ref[:, pidx * LANES:(pidx + 1) * LANES] = jnp.dot(
                p, vbd, preferred_element_type=F32).astype(o_ref.dtype)


def _attention(q, kv, sinks, rel_bias):
    s, qd = q.shape
    n_q_heads = qd // ATTN_HEAD_DIM
    qpk = n_q_heads // N_KV_HEADS
    assert qpk % 2 == 0 and N_KV_HEADS % 2 == 0 and 2 * ATTN_HEAD_DIM == LANES and s % ATTN_BLOCK == 0
    blk = ATTN_BLOCK
    nb = s // blk
    ngp = N_KV_HEADS // 2
    qw = 2 * qpk * ATTN_HEAD_DIM
    voff = N_KV_HEADS * ATTN_HEAD_DIM // LANES
    bucket = jnp.asarray(_t5_bucket_table())
    est = 2 * (2 * blk * qw * 2 + 4 * blk * LANES * 4 + blk * 2 * blk * 4) + 2 * qpk * blk * 2 * blk * 4 \
        + 24 * blk * 4 * blk * 4
    smem = pl.BlockSpec(memory_space=pltpu.SMEM)
    return pl.pallas_call(
        functools.partial(_attn_kernel, qpk=qpk, n_q_heads=n_q_heads),
        grid=(ngp, nb),
        in_specs=[
            smem, smem,
            pl.BlockSpec((blk, 2 * blk), lambda g, n: (0, 0)),
            pl.BlockSpec((blk, qw), lambda g, n: (n, g)),
            pl.BlockSpec((blk, LANES), lambda g, n: (jnp.maximum(n - 1, 0), g)),
            pl.BlockSpec((blk, LANES), lambda g, n: (n, g)),
            pl.BlockSpec((blk, LANES), lambda g, n: (jnp.maximum(n - 1, 0), voff + g)),
            pl.BlockSpec((blk, LANES), lambda g, n: (n, voff + g)),
        ],
        out_specs=pl.BlockSpec((blk, qw), lambda g, n: (n, g)),
        out_shape=jax.ShapeDtypeStruct((s, qd), BF16),
        scratch_shapes=[pltpu.VMEM((2 * qpk, blk, 2 * blk), F32)],
        compiler_params=_params(("parallel", "arbitrary"), est),
        name="swa_attention",
    )(rel_bias.astype(F32).reshape(-1), sinks.astype(F32).reshape(-1), bucket, q, kv, kv, kv, kv)


def kernel(x, norm_mix_pre, norm_mix_post, norm_ffn_pre, norm_ffn_post, ssm_w_in, ssm_conv_w, ssm_conv_b, ssm_dt_bias, ssm_a_log, ssm_d, ssm_norm, ssm_w_out, kv_norm, w_kv, b_kv, attn_w_q, attn_b_q, attn_sinks, attn_w_o, attn_b_o, rel_bias, ffn_w_up, ffn_conv_w, ffn_conv_b, ffn_w_down):
    bsz, s, d = x.shape
    assert bsz == 1 and norm_mix_pre.shape[0] == 2
    d_inner = ssm_norm.shape[-1]
    n_heads = ssm_dt_bias.shape[-1]
    zxbc_dim = ssm_w_in.shape[-1] - n_heads

    def ffn(h_in_u, layer):
        hff = _ffn_up(h_in_u, ffn_w_up, ffn_conv_w, ffn_conv_b, layer)
        return _matmul(hff, ffn_w_down, None, F32, 1024, 256, "ffn_down", a_buffers=1, layer=layer)

    h0 = x.reshape(s, d)

    u = _prenorm(h0, norm_mix_pre[0])
    zxbc = _matmul(u, ssm_w_in, None, F32, 1024, 512, "in_proj", n=zxbc_dim, layer=0)
    dt_raw = _matmul(u, ssm_w_in, None, F32, 1024, 128, "dt_proj", col0=zxbc_dim, n=n_heads, layer=0)
    y = _ssd(zxbc, dt_raw, ssm_conv_w[0], ssm_conv_b[0], ssm_dt_bias[0], ssm_a_log[0], ssm_d[0], ssm_norm[0],
             d_inner, n_heads)
    mix = _matmul(y, ssm_w_out, None, F32, 1024, 256, "out_proj", a_buffers=1, layer=0)
    h1, u = _resnorm(h0, mix, norm_mix_post[0], [norm_ffn_pre[0]])
    f = ffn(u, 0)

    h2, ukv, uq = _resnorm(h1, f, norm_ffn_post[0], [kv_norm, norm_mix_pre[1]])
    kv = _matmul(ukv, w_kv, b_kv, F32, 1024, 512, "kv_proj")
    q = _matmul(uq, attn_w_q, attn_b_q[0], BF16, 1024, 512, "q_proj", layer=0)
    o = _attention(q, kv, attn_sinks[0], rel_bias)
    mix = _matmul(o, attn_w_o, attn_b_o[0], F32, 1024, 512, "o_proj", layer=0)
    h3, u = _resnorm(h2, mix, norm_mix_post[1], [norm_ffn_pre[1]])
    f = ffn(u, 1)
    (h4,) = _resnorm(h3, f, norm_ffn_post[1], [])
    return h4.reshape(bsz, s, d)
```

```python
import functools
import math

import numpy as np
import jax
import jax.numpy as jnp
from jax import lax
from jax.experimental import pallas as pl
from jax.experimental.pallas import tpu as pltpu

EPS = 1e-6
SSM_NORM_EPS = 1e-5
SSM_HEAD_DIM = 64
SSM_GROUPS = 8
D_STATE = 128
SSM_CONV = 4
SSM_CHUNK = 128
ATTN_HEAD_DIM = 64
N_KV_HEADS = 8
WINDOW = 128
ATTN_BLOCK = 128
N_BUCKETS = 32
MAX_DISTANCE = 128
FFN_CONV = 3

LANES = 128
SUBLANES = 8
VMEM_CAP_BYTES = 60 * 1024 * 1024

F32 = jnp.float32
BF16 = jnp.bfloat16
HIGHEST = lax.Precision.HIGHEST


def _vmem_limit(est_bytes):
    return int(min(VMEM_CAP_BYTES, max(32 * 1024 * 1024, est_bytes * 5 // 4 + (4 << 20))))


def _params(semantics, est_bytes, flags=None):
    return pltpu.CompilerParams(dimension_semantics=semantics, vmem_limit_bytes=_vmem_limit(est_bytes), flags=flags)


def _tile(dim, pref, align):
    if dim <= pref:
        return dim
    t = (pref // align) * align
    while t >= align:
        if dim % t == 0:
            return t
        t -= align
    raise ValueError(f"no tile for {dim} (pref {pref}, align {align})")


def _sigmoid(x):
    return 0.5 * jnp.tanh(0.5 * x) + 0.5


def _softplus(x):
    return jnp.maximum(x, 0.0) + jnp.log1p(jnp.exp(-jnp.abs(x)))


def _split3(v):
    hi = v.astype(BF16)
    r1 = v - hi.astype(F32)
    mid = r1.astype(BF16)
    lo = (r1 - mid.astype(F32)).astype(BF16)
    return [hi, mid, lo]


def _rms(x, g, eps):
    ms = jnp.mean(x * x, axis=-1, keepdims=True)
    return (x * lax.rsqrt(ms + eps)) * g


def _prenorm_kernel(x_ref, g_ref, u_ref):
    u_ref[...] = _rms(x_ref[...], g_ref[...], EPS).astype(u_ref.dtype)


def _prenorm(x, g):
    s, d = x.shape
    tr = _tile(s, 256, SUBLANES)
    est = 2 * tr * d * (4 + 2)
    return pl.pallas_call(
        _prenorm_kernel,
        grid=(s // tr,),
        in_specs=[pl.BlockSpec((tr, d), lambda i: (i, 0)), pl.BlockSpec((1, d), lambda i: (0, 0))],
        out_specs=pl.BlockSpec((tr, d), lambda i: (i, 0)),
        out_shape=jax.ShapeDtypeStruct((s, d), BF16),
        compiler_params=_params(("parallel",), est),
        name="prenorm",
    )(x, g.reshape(1, d))


def _resnorm_kernel(r_ref, m_ref, gpost_ref, *rest, n_u):
    g_refs = rest[:n_u]
    h_ref = rest[n_u]
    u_refs = rest[n_u + 1:]
    h = r_ref[...] + _rms(m_ref[...], gpost_ref[...], EPS)
    h_ref[...] = h
    if n_u:
        ms = jnp.mean(h * h, axis=-1, keepdims=True)
        hn = h * lax.rsqrt(ms + EPS)
        for g_ref, u_ref in zip(g_refs, u_refs):
            u_ref[...] = (hn * g_ref[...]).astype(u_ref.dtype)


def _resnorm(r, m, gpost, gains):
    s, d = r.shape
    n_u = len(gains)
    tr = _tile(s, 128, SUBLANES)
    est = 2 * tr * d * (4 * 3 + 2 * n_u)
    row = pl.BlockSpec((tr, d), lambda i: (i, 0))
    vec = pl.BlockSpec((1, d), lambda i: (0, 0))
    outs = pl.pallas_call(
        functools.partial(_resnorm_kernel, n_u=n_u),
        grid=(s // tr,),
        in_specs=[row, row, vec] + [vec] * n_u,
        out_specs=[row] + [row] * n_u,
        out_shape=[jax.ShapeDtypeStruct((s, d), F32)] + [jax.ShapeDtypeStruct((s, d), BF16)] * n_u,
        compiler_params=_params(("parallel",), est),
        name="resnorm",
    )(r, m, gpost.reshape(1, d), *[g.reshape(1, d) for g in gains])
    return outs


def _matmul_kernel(a_ref, w_ref, *rest, has_bias):
    o_ref = rest[-1]
    acc = jnp.dot(a_ref[...], w_ref[...].astype(BF16), preferred_element_type=F32)
    if has_bias:
        acc = acc + rest[0][...]
    o_ref[...] = acc.astype(o_ref.dtype)


def _matmul(a, w, bias, out_dtype, tm_pref, tn_pref, name, a_buffers=2, col0=0, n=None, layer=None):
    m, k = a.shape
    assert (w.ndim == 3) == (layer is not None)
    n = w.shape[-1] if n is None else n
    tm = _tile(m, tm_pref, SUBLANES)
    tn = _tile(math.gcd(n, col0) if col0 else n, tn_pref, LANES)
    assert n % tn == 0 and col0 % tn == 0
    joff = col0 // tn
    osz = jnp.dtype(out_dtype).itemsize
    wsz = jnp.dtype(w.dtype).itemsize
    est = a_buffers * tm * k * 2 + 2 * (k * tn * wsz + tm * tn * osz) + tm * tn * 4 + (k * tn * 2 if wsz != 2 else 0)
    a_mode = {} if a_buffers == 2 else {"pipeline_mode": pl.Buffered(a_buffers)}
    if layer is None:
        w_spec = pl.BlockSpec((k, tn), lambda i, j: (0, j + joff))
    else:
        w_spec = pl.BlockSpec((None, k, tn), lambda i, j: (layer, 0, j + joff))
    in_specs = [pl.BlockSpec((tm, k), lambda i, j: (i, 0), **a_mode), w_spec]
    args = [a, w]
    if bias is not None:
        in_specs.append(pl.BlockSpec((1, tn), lambda i, j: (0, j)))
        args.append(bias.reshape(1, n).astype(F32))
    return pl.pallas_call(
        functools.partial(_matmul_kernel, has_bias=bias is not None),
        grid=(m // tm, n // tn),
        in_specs=in_specs,
        out_specs=pl.BlockSpec((tm, tn), lambda i, j: (i, j)),
        out_shape=jax.ShapeDtypeStruct((m, n), out_dtype),
        compiler_params=_params(("parallel", "parallel"), est),
        name=name,
    )(*args)


FFN_EPILOGUE_ROWS = 128


FFN_ROW_SPLIT = 1


def _ffn_up_kernel(u_ref, wg_ref, wv_ref, cwg_ref, cwv_ref, cbg_ref, cbv_ref, o_ref, halo, wcat):
    i = pl.program_id(0)
    j = pl.program_id(1)
    tm = u_ref.shape[0]
    tn = wg_ref.shape[1]
    part = tm // FFN_ROW_SPLIT
    rows = min(FFN_EPILOGUE_ROWS, part)

    @pl.when(i == 0)
    def _():
        halo[j] = jnp.zeros(halo.shape[1:], F32)

    wcat[:, 0:tn] = wg_ref[...].astype(BF16)
    wcat[:, tn:2 * tn] = wv_ref[...].astype(BF16)
    parts = [jnp.dot(u_ref[p * part:(p + 1) * part, :], wcat[...], preferred_element_type=F32)
             for p in range(FFN_ROW_SPLIT)]

    cw = jnp.concatenate([cwg_ref[...], cwv_ref[...]], axis=1)
    cb = jnp.concatenate([cbg_ref[...], cbv_ref[...]], axis=1)
    prev = halo[j]
    for p in range(FFN_ROW_SPLIT):
        for c in range(part // rows):
            cur = parts[p][c * rows:(c + 1) * rows, :]
            both = jnp.concatenate([prev, cur], axis=0)
            acc = cb + both[SUBLANES - 2:SUBLANES - 2 + rows, :] * cw[0:1, :]
            acc = acc + both[SUBLANES - 1:SUBLANES - 1 + rows, :] * cw[1:2, :]
            acc = acc + cur * cw[2:3, :]
            g = acc[:, 0:tn]
            v = acc[:, tn:2 * tn]
            r0 = p * part + c * rows
            o_ref[r0:r0 + rows, :] = ((g * _sigmoid(g)) * v).astype(o_ref.dtype)
            prev = cur[rows - SUBLANES:rows, :]
    halo[j] = prev


def _ffn_up(u, w_up, conv_w, conv_b, layer):
    s, d = u.shape
    f = w_up.shape[-1] // 2
    tm = _tile(s, 1024, FFN_EPILOGUE_ROWS)
    tn = _tile(f, 256, LANES)
    nj = f // tn
    wsz = jnp.dtype(w_up.dtype).itemsize
    est = 2 * (tm * d * 2 + 2 * d * tn * wsz + tm * tn * 2) + d * 2 * tn * 2 + nj * SUBLANES * 2 * tn * 4 \
        + 3 * tm * 2 * tn * 4
    cb = conv_b.reshape(conv_b.shape[0], 1, 2 * f)
    return pl.pallas_call(
        _ffn_up_kernel,
        grid=(s // tm, nj),
        in_specs=[
            pl.BlockSpec((tm, d), lambda i, j: (i, 0)),
            pl.BlockSpec((None, d, tn), lambda i, j: (layer, 0, j)),
            pl.BlockSpec((None, d, tn), lambda i, j: (layer, 0, j + nj)),
            pl.BlockSpec((None, FFN_CONV, tn), lambda i, j: (layer, 0, j)),
            pl.BlockSpec((None, FFN_CONV, tn), lambda i, j: (layer, 0, j + nj)),
            pl.BlockSpec((None, 1, tn), lambda i, j: (layer, 0, j)),
            pl.BlockSpec((None, 1, tn), lambda i, j: (layer, 0, j + nj)),
        ],
        out_specs=pl.BlockSpec((tm, tn), lambda i, j: (i, j)),
        out_shape=jax.ShapeDtypeStruct((s, f), BF16),
        scratch_shapes=[
            pltpu.VMEM((nj, SUBLANES, 2 * tn), F32),
            pltpu.VMEM((d, 2 * tn), BF16),
        ],
        compiler_params=_params(("arbitrary", "arbitrary"), est),
        name="ffn_up",
    )(u, w_up, w_up, conv_w, conv_w, cb, cb)


def _ssd_kernel(z_ref, x_ref, b_ref, c_ref, dt_ref, dtt_ref,
                cwx_ref, cwb_ref, cwc_ref, cbx_ref, cbb_ref, cbc_ref,
                dtb_row_ref, dtb_col_ref, alog_row_ref, alog_col_ref, dskip_ref, gnorm_ref, expand_ref,
                y_ref, state_ref, ext_x, ext_b, ext_c, ydiag_ref, *, heads):
    chunk = pl.program_id(1)
    cl = SSM_CHUNK
    width = heads * SSM_HEAD_DIM

    @pl.when(chunk == 0)
    def _():
        state_ref[...] = jnp.zeros(state_ref.shape, F32)
        ext_x[0:SUBLANES, :] = jnp.zeros((SUBLANES, ext_x.shape[1]), F32)
        ext_b[0:SUBLANES, :] = jnp.zeros((SUBLANES, ext_b.shape[1]), F32)
        ext_c[0:SUBLANES, :] = jnp.zeros((SUBLANES, ext_c.shape[1]), F32)

    def conv_silu(cur_ref, ext, cw_ref, cb_ref):
        cur = cur_ref[...]
        ext[SUBLANES:SUBLANES + cl, :] = cur
        acc = cb_ref[...] + ext[pl.ds(SUBLANES - 3, cl), :] * cw_ref[0:1, :]
        acc = acc + ext[pl.ds(SUBLANES - 2, cl), :] * cw_ref[1:2, :]
        acc = acc + ext[pl.ds(SUBLANES - 1, cl), :] * cw_ref[2:3, :]
        acc = acc + cur * cw_ref[3:4, :]
        ext[0:SUBLANES, :] = cur[cl - SUBLANES:cl, :]
        return acc * _sigmoid(acc)

    xs = conv_silu(x_ref, ext_x, cwx_ref, cbx_ref)
    bc = conv_silu(b_ref, ext_b, cwb_ref, cbb_ref)
    cc = conv_silu(c_ref, ext_c, cwc_ref, cbc_ref)

    li = lax.broadcasted_iota(jnp.int32, (cl, cl), 0)
    si = lax.broadcasted_iota(jnp.int32, (cl, cl), 1)
    tril = li >= si

    dtv = _softplus(dt_ref[...] + dtb_row_ref[...])
    a = dtv * (-jnp.exp(alog_row_ref[...]))
    ones_lo = tril.astype(F32).astype(BF16)
    ones_up = (li <= si).astype(F32).astype(BF16)
    acum = jnp.dot(jnp.concatenate([ones_lo] * 3, axis=1), jnp.concatenate(_split3(a), axis=0),
                   preferred_element_type=F32)
    dtv_t = _softplus(dtt_ref[...] + dtb_col_ref[...])
    a_t = dtv_t * (-jnp.exp(alog_col_ref[...]))
    acum_t = jnp.dot(jnp.concatenate(_split3(a_t), axis=1), jnp.concatenate([ones_up] * 3, axis=0),
                     preferred_element_type=F32)
    alast = acum[cl - 1:cl, :]
    exp_acum = jnp.exp(acum)
    decay_end = jnp.exp(alast - acum)

    def widen(v):
        return jnp.dot(jnp.concatenate(_split3(v), axis=1), expand_ref[...], preferred_element_type=F32)

    dt_w = widen(dtv)
    exp_acum_w = widen(exp_acum)
    decay_end_w = widen(decay_end)

    xdt = xs * dt_w
    cb = lax.dot_general(cc.astype(BF16), bc.astype(BF16), (((1,), (1,)), ((), ())),
                         preferred_element_type=F32)
    cb = jnp.where(tril, cb, 0.0)

    lane = lax.broadcasted_iota(jnp.int32, (cl, LANES), 1)
    low = lane < SSM_HEAD_DIM
    for p in range(heads // 2):
        ms = []
        for hp in range(2):
            r = 2 * p + hp
            seg = acum[:, r:r + 1] - acum_t[r:r + 1, :]
            decay = jnp.exp(jnp.minimum(seg, 0.0))
            ms.append((cb * decay).astype(BF16))
        m_pair = jnp.concatenate(ms, axis=1)
        xp = xdt[:, p * LANES:(p + 1) * LANES]
        rhs = jnp.concatenate([jnp.where(low, xp, 0.0), jnp.where(low, 0.0, xp)], axis=0).astype(BF16)
        ydiag_ref[:, p * LANES:(p + 1) * LANES] = jnp.dot(m_pair, rhs, preferred_element_type=F32)

    state = state_ref[...]
    y_off = jnp.dot(cc.astype(BF16), state.astype(BF16), preferred_element_type=F32) * exp_acum_w
    y = ydiag_ref[...] + y_off + dskip_ref[...] * xs

    xw = (xdt * decay_end_w).astype(BF16)
    upd = jnp.dot(jnp.transpose(bc).astype(BF16), xw, preferred_element_type=F32)
    state_ref[...] = state * exp_acum_w[cl - 1:cl, :] + upd

    zb = z_ref[...]
    yz = y * (zb * _sigmoid(zb))
    ms2 = jnp.mean(yz * yz, axis=-1, keepdims=True)
    y_ref[...] = ((yz * lax.rsqrt(ms2 + SSM_NORM_EPS)) * gnorm_ref[...]).astype(y_ref.dtype)


def _ssd(zxbc, dt_raw, conv_w, conv_b, dt_bias, a_log, d_skip, g_norm, d_inner, n_heads):
    s = zxbc.shape[0]
    g = SSM_GROUPS
    heads = n_heads // g
    width = d_inner // g
    assert width == heads * SSM_HEAD_DIM and heads % 2 == 0 and heads <= LANES and width % LANES == 0
    assert D_STATE == LANES and s % SSM_CHUNK == 0
    hp = max(SUBLANES, heads)
    nc = s // SSM_CHUNK
    cl = SSM_CHUNK

    dt_g = dt_raw.reshape(s, g, heads).transpose(1, 0, 2)
    dt_row = jnp.pad(dt_g, ((0, 0), (0, 0), (0, LANES - heads)))
    dt_col = jnp.pad(dt_g.transpose(0, 2, 1), ((0, 0), (0, hp - heads), (0, 0)))

    def rowvec(v):
        return jnp.pad(v.astype(F32).reshape(g, 1, heads), ((0, 0), (0, 0), (0, LANES - heads)))

    def colvec(v):
        return jnp.pad(v.astype(F32).reshape(g, heads, 1), ((0, 0), (0, hp - heads), (0, 0)))

    dskip_w = jnp.repeat(d_skip.astype(F32), SSM_HEAD_DIM).reshape(1, d_inner)
    expand3 = np.tile(np.arange(LANES)[:, None] == (np.arange(width)[None, :] // SSM_HEAD_DIM), (3, 1))
    cbias = conv_b.reshape(1, -1)
    wb = width // LANES
    xoff = d_inner // width
    boff = 2 * d_inner // LANES
    coff = boff + g * D_STATE // LANES
    cwb_off = d_inner // LANES
    cwc_off = cwb_off + g * D_STATE // LANES

    est = 2 * (2 * cl * width * 4 + 2 * cl * LANES * 4 + cl * LANES * 4 + hp * cl * 4 + cl * width * 2) \
        + D_STATE * width * 4 + 3 * (cl + SUBLANES) * width * 4 + cl * width * 4 + 16 * cl * width * 4
    return pl.pallas_call(
        functools.partial(_ssd_kernel, heads=heads),
        grid=(g, nc),
        in_specs=[
            pl.BlockSpec((cl, width), lambda gi, c: (c, gi)),
            pl.BlockSpec((cl, width), lambda gi, c: (c, xoff + gi)),
            pl.BlockSpec((cl, D_STATE), lambda gi, c: (c, boff + gi)),
            pl.BlockSpec((cl, D_STATE), lambda gi, c: (c, coff + gi)),
            pl.BlockSpec((None, cl, LANES), lambda gi, c: (gi, c, 0)),
            pl.BlockSpec((None, hp, cl), lambda gi, c: (gi, 0, c)),
            pl.BlockSpec((SSM_CONV, width), lambda gi, c: (0, gi)),
            pl.BlockSpec((SSM_CONV, D_STATE), lambda gi, c: (0, cwb_off + gi)),
            pl.BlockSpec((SSM_CONV, D_STATE), lambda gi, c: (0, cwc_off + gi)),
            pl.BlockSpec((1, width), lambda gi, c: (0, gi)),
            pl.BlockSpec((1, D_STATE), lambda gi, c: (0, cwb_off + gi)),
            pl.BlockSpec((1, D_STATE), lambda gi, c: (0, cwc_off + gi)),
            pl.BlockSpec((None, 1, LANES), lambda gi, c: (gi, 0, 0)),
            pl.BlockSpec((None, hp, 1), lambda gi, c: (gi, 0, 0)),
            pl.BlockSpec((None, 1, LANES), lambda gi, c: (gi, 0, 0)),
            pl.BlockSpec((None, hp, 1), lambda gi, c: (gi, 0, 0)),
            pl.BlockSpec((1, width), lambda gi, c: (0, gi)),
            pl.BlockSpec((1, width), lambda gi, c: (0, gi)),
            pl.BlockSpec((3 * LANES, width), lambda gi, c: (0, 0)),
        ],
        out_specs=pl.BlockSpec((cl, width), lambda gi, c: (c, gi)),
        out_shape=jax.ShapeDtypeStruct((s, d_inner), BF16),
        scratch_shapes=[
            pltpu.VMEM((D_STATE, width), F32),
            pltpu.VMEM((cl + SUBLANES, width), F32),
            pltpu.VMEM((cl + SUBLANES, D_STATE), F32),
            pltpu.VMEM((cl + SUBLANES, D_STATE), F32),
            pltpu.VMEM((cl, width), F32),
        ],
        compiler_params=_params(("parallel", "arbitrary"), est),
        name="ssd_scan",
    )(zxbc, zxbc, zxbc, zxbc, dt_row, dt_col, conv_w, conv_w, conv_w, cbias, cbias, cbias,
      rowvec(dt_bias), colvec(dt_bias), rowvec(a_log), colvec(a_log), dskip_w, g_norm.reshape(1, d_inner),
      jnp.asarray(expand3, BF16))


def _t5_bucket_table():
    q = np.arange(ATTN_BLOCK)[:, None]
    k = np.arange(2 * ATTN_BLOCK)[None, :]
    rel = np.maximum(q - k + ATTN_BLOCK, 0)
    max_exact = N_BUCKETS // 2
    relf = np.maximum(rel, 1).astype(np.float32)
    large = max_exact + (np.log(relf / np.float32(max_exact)) / np.float32(math.log(MAX_DISTANCE / max_exact))
                         * np.float32(N_BUCKETS - max_exact)).astype(np.int32)
    large = np.minimum(large, N_BUCKETS - 1)
    return np.where(rel < max_exact, rel, large).astype(np.int32)


def _attn_kernel(relb_ref, sink_ref, bucket_ref, q_ref, kp_ref, kc_ref, vp_ref, vc_ref, o_ref, bias_ref,
                 *, qpk, n_q_heads):
    gp = pl.program_id(0)
    n = pl.program_id(1)
    blk = ATTN_BLOCK
    npair = qpk // 2
    head0 = gp * 2 * qpk

    @pl.when(n == 0)
    def _():
        bucket = bucket_ref[...]
        qi = lax.broadcasted_iota(jnp.int32, (blk, 2 * blk), 0)
        ci = lax.broadcasted_iota(jnp.int32, (blk, 2 * blk), 1)
        rel = qi - ci + blk
        in_window = (rel >= 0) & (rel < WINDOW)

        def body(it, carry):
            kvh = it // npair
            pair = it - kvh * npair
            row0 = pl.multiple_of(pair * blk, blk)
            for hp in range(2):
                head = head0 + kvh * qpk + pair * 2 + hp
                acc = jnp.zeros((blk, 2 * blk), F32)
                for b in range(N_BUCKETS):
                    acc = jnp.where(bucket == b, relb_ref[b * n_q_heads + head], acc)
                bias_ref[kvh, pl.ds(row0, blk), hp * 2 * blk:(hp + 1) * 2 * blk] = jnp.where(in_window, acc, -jnp.inf)
            return carry

        lax.fori_loop(0, 2 * npair, body, 0)

    kk = jnp.concatenate([kp_ref[...], kc_ref[...]], axis=0)
    vv = jnp.concatenate([vp_ref[...], vc_ref[...]], axis=0)
    lane = lax.broadcasted_iota(jnp.int32, (2 * blk, LANES), 1)
    low = lane < ATTN_HEAD_DIM
    scale = ATTN_HEAD_DIM ** -0.5

    def pair_operand(t, kvh, mult):
        sel = jnp.where(low if kvh == 0 else jnp.logical_not(low), t, 0.0)
        dup = sel + pltpu.roll(sel, ATTN_HEAD_DIM, axis=1)
        if mult is not None:
            dup = dup * mult
        return jnp.concatenate([jnp.where(low, dup, 0.0), jnp.where(low, 0.0, dup)], axis=0).astype(BF16)

    row_bd = lax.broadcasted_iota(jnp.int32, (4 * blk, LANES), 0)
    lane_bd = lax.broadcasted_iota(jnp.int32, (4 * blk, LANES), 1)
    ones_bd = ((row_bd < 2 * blk) == (lane_bd < ATTN_HEAD_DIM)).astype(F32).astype(BF16)
    low_out = lax.broadcasted_iota(jnp.int32, (npair * blk, LANES), 1) < ATTN_HEAD_DIM

    def run(first_block):
        if first_block:
            ci = lax.broadcasted_iota(jnp.int32, (npair * blk, 2 * blk), 1)
            before_start = ci < blk
        for kvh in range(2):
            kbd = pair_operand(kk, kvh, scale)
            vbd = pair_operand(vv, kvh, None)
            base = kvh * npair
            qs = jnp.concatenate([q_ref[:, (base + i) * LANES:(base + i + 1) * LANES] for i in range(npair)], axis=0)
            s = lax.dot_general(qs, kbd, (((1,), (1,)), ((), ())), preferred_element_type=F32)
            es = []
            sink_terms = []
            for hp in range(2):
                sinks = [sink_ref[head0 + kvh * qpk + i * 2 + hp] for i in range(npair)]
                sink_col = jnp.concatenate([jnp.full((blk, 1), v, F32) for v in sinks], axis=0)
                sink_all = jnp.concatenate([jnp.full((blk, LANES), v, F32) for v in sinks], axis=0)
                sh = s[:, hp * 2 * blk:(hp + 1) * 2 * blk] + bias_ref[kvh, :, hp * 2 * blk:(hp + 1) * 2 * blk]
                if first_block:
                    sh = jnp.where(before_start, -jnp.inf, sh)
                m = jnp.maximum(jnp.max(sh, axis=-1, keepdims=True), sink_col)
                es.append(jnp.exp(sh - m).astype(BF16))
                sink_terms.append(jnp.exp(sink_all - jnp.broadcast_to(m, sink_all.shape)))
            e = jnp.concatenate(es, axis=1)
            ov = jnp.dot(e, jnp.concatenate([vbd, ones_bd], axis=1), preferred_element_type=F32)
            denom = ov[:, LANES:2 * LANES] + jnp.where(low_out, sink_terms[0], sink_terms[1])
            o = (ov[:, 0:LANES] * (1.0 / denom)).astype(o_ref.dtype)
            for i in range(npair):
                o_ref[:, (base + i) * LANES:(base + i + 1) * LANES] = o[i * blk:(i + 1) * blk, :]

    @pl.when(n == 0)
    def _():
        run(True)

    @pl.when(n != 0)
    def _():
        run(False)


def _attention(q, kv, sinks, rel_bias):
    s, qd = q.shape
    n_q_heads = qd // ATTN_HEAD_DIM
    qpk = n_q_heads // N_KV_HEADS
    assert qpk % 2 == 0 and N_KV_HEADS % 2 == 0 and 2 * ATTN_HEAD_DIM == LANES and s % ATTN_BLOCK == 0
    blk = ATTN_BLOCK
    nb = s // blk
    ngp = N_KV_HEADS // 2
    qw = 2 * qpk * ATTN_HEAD_DIM
    voff = N_KV_HEADS * ATTN_HEAD_DIM // LANES
    bucket = jnp.asarray(_t5_bucket_table())
    est = 2 * (2 * blk * qw * 2 + 4 * blk * LANES * 4 + blk * 2 * blk * 4) + 2 * qpk * blk * 2 * blk * 4 \
        + 24 * blk * 4 * blk * 4
    smem = pl.BlockSpec(memory_space=pltpu.SMEM)
    return pl.pallas_call(
        functools.partial(_attn_kernel, qpk=qpk, n_q_heads=n_q_heads),
        grid=(ngp, nb),
        in_specs=[
            smem, smem,
            pl.BlockSpec((blk, 2 * blk), lambda g, n: (0, 0)),
            pl.BlockSpec((blk, qw), lambda g, n: (n, g)),
            pl.BlockSpec((blk, LANES), lambda g, n: (jnp.maximum(n - 1, 0), g)),
            pl.BlockSpec((blk, LANES), lambda g, n: (n, g)),
            pl.BlockSpec((blk, LANES), lambda g, n: (jnp.maximum(n - 1, 0), voff + g)),
            pl.BlockSpec((blk, LANES), lambda g, n: (n, voff + g)),
        ],
        out_specs=pl.BlockSpec((blk, qw), lambda g, n: (n, g)),
        out_shape=jax.ShapeDtypeStruct((s, qd), BF16),
        scratch_shapes=[pltpu.VMEM((2, (qpk // 2) * blk, 4 * blk), F32)],
        compiler_params=_params(("parallel", "arbitrary"), est),
        name="swa_attention",
    )(rel_bias.astype(F32).reshape(-1), sinks.astype(F32).reshape(-1), bucket, q, kv, kv, kv, kv)


def kernel(x, norm_mix_pre, norm_mix_post, norm_ffn_pre, norm_ffn_post, ssm_w_in, ssm_conv_w, ssm_conv_b, ssm_dt_bias, ssm_a_log, ssm_d, ssm_norm, ssm_w_out, kv_norm, w_kv, b_kv, attn_w_q, attn_b_q, attn_sinks, attn_w_o, attn_b_o, rel_bias, ffn_w_up, ffn_conv_w, ffn_conv_b, ffn_w_down):
    bsz, s, d = x.shape
    assert bsz == 1 and norm_mix_pre.shape[0] == 2
    d_inner = ssm_norm.shape[-1]
    n_heads = ssm_dt_bias.shape[-1]
    zxbc_dim = ssm_w_in.shape[-1] - n_heads

    def ffn(h_in_u, layer):
        hff = _ffn_up(h_in_u, ffn_w_up, ffn_conv_w, ffn_conv_b, layer)
        return _matmul(hff, ffn_w_down, None, F32, 1024, 256, "ffn_down", a_buffers=1, layer=layer)

    h0 = x.reshape(s, d)

    u = _prenorm(h0, norm_mix_pre[0])
    zxbc = _matmul(u, ssm_w_in, None, F32, 1024, 512, "in_proj", n=zxbc_dim, layer=0)
    dt_raw = _matmul(u, ssm_w_in, None, F32, 1024, 128, "dt_proj", col0=zxbc_dim, n=n_heads, layer=0)
    y = _ssd(zxbc, dt_raw, ssm_conv_w[0], ssm_conv_b[0], ssm_dt_bias[0], ssm_a_log[0], ssm_d[0], ssm_norm[0],
             d_inner, n_heads)
    mix = _matmul(y, ssm_w_out, None, F32, 1024, 256, "out_proj", a_buffers=1, layer=0)
    h1, u = _resnorm(h0, mix, norm_mix_post[0], [norm_ffn_pre[0]])
    f = ffn(u, 0)

    h2, ukv, uq = _resnorm(h1, f, norm_ffn_post[0], [kv_norm, norm_mix_pre[1]])
    kv = _matmul(ukv, w_kv, b_kv, F32, 1024, 512, "kv_proj")
    q = _matmul(uq, attn_w_q, attn_b_q[0], BF16, 1024, 512, "q_proj", layer=0)
    o = _attention(q, kv, attn_sinks[0], rel_bias)
    mix = _matmul(o, attn_w_o, attn_b_o[0], F32, 1024, 512, "o_proj", layer=0)
    h3, u = _resnorm(h2, mix, norm_mix_post[1], [norm_ffn_pre[1]])
    f = ffn(u, 1)
    (h4,) = _resnorm(h3, f, norm_ffn_post[1], [])
    return h4.reshape(bsz, s, d)
```

```python
import functools
import math

import numpy as np
import jax
import jax.numpy as jnp
from jax import lax
from jax.experimental import pallas as pl
from jax.experimental.pallas import tpu as pltpu

EPS = 1e-6
SSM_NORM_EPS = 1e-5
SSM_HEAD_DIM = 64
SSM_GROUPS = 8
D_STATE = 128
SSM_CONV = 4
SSM_CHUNK = 128
ATTN_HEAD_DIM = 64
N_KV_HEADS = 8
WINDOW = 128
ATTN_BLOCK = 128
N_BUCKETS = 32
MAX_DISTANCE = 128
FFN_CONV = 3

LANES = 128
SUBLANES = 8
VMEM_CAP_BYTES = 60 * 1024 * 1024

F32 = jnp.float32
BF16 = jnp.bfloat16
HIGHEST = lax.Precision.HIGHEST


def _vmem_limit(est_bytes):
    return int(min(VMEM_CAP_BYTES, max(32 * 1024 * 1024, est_bytes * 5 // 4 + (4 << 20))))


def _params(semantics, est_bytes, flags=None):
    return pltpu.CompilerParams(dimension_semantics=semantics, vmem_limit_bytes=_vmem_limit(est_bytes), flags=flags)


def _tile(dim, pref, align):
    if dim <= pref:
        return dim
    t = (pref // align) * align
    while t >= align:
        if dim % t == 0:
            return t
        t -= align
    raise ValueError(f"no tile for {dim} (pref {pref}, align {align})")


def _sigmoid(x):
    return 0.5 * jnp.tanh(0.5 * x) + 0.5


def _softplus(x):
    return jnp.maximum(x, 0.0) + jnp.log1p(jnp.exp(-jnp.abs(x)))


def _split3(v):
    hi = v.astype(BF16)
    r1 = v - hi.astype(F32)
    mid = r1.astype(BF16)
    lo = (r1 - mid.astype(F32)).astype(BF16)
    return [hi, mid, lo]


def _rms(x, g, eps):
    ms = jnp.mean(x * x, axis=-1, keepdims=True)
    return (x * lax.rsqrt(ms + eps)) * g


def _prenorm_kernel(x_ref, g_ref, u_ref):
    u_ref[...] = _rms(x_ref[...], g_ref[...], EPS).astype(u_ref.dtype)


def _prenorm(x, g):
    s, d = x.shape
    tr = _tile(s, 256, SUBLANES)
    est = 2 * tr * d * (4 + 2)
    return pl.pallas_call(
        _prenorm_kernel,
        grid=(s // tr,),
        in_specs=[pl.BlockSpec((tr, d), lambda i: (i, 0)), pl.BlockSpec((1, d), lambda i: (0, 0))],
        out_specs=pl.BlockSpec((tr, d), lambda i: (i, 0)),
        out_shape=jax.ShapeDtypeStruct((s, d), BF16),
        compiler_params=_params(("parallel",), est),
        name="prenorm",
    )(x, g.reshape(1, d))


def _resnorm_kernel(r_ref, m_ref, gpost_ref, *rest, n_u):
    g_refs = rest[:n_u]
    h_ref = rest[n_u]
    u_refs = rest[n_u + 1:]
    h = r_ref[...] + _rms(m_ref[...], gpost_ref[...], EPS)
    h_ref[...] = h
    if n_u:
        ms = jnp.mean(h * h, axis=-1, keepdims=True)
        hn = h * lax.rsqrt(ms + EPS)
        for g_ref, u_ref in zip(g_refs, u_refs):
            u_ref[...] = (hn * g_ref[...]).astype(u_ref.dtype)


def _resnorm(r, m, gpost, gains):
    s, d = r.shape
    n_u = len(gains)
    tr = _tile(s, 128, SUBLANES)
    est = 2 * tr * d * (4 * 3 + 2 * n_u)
    row = pl.BlockSpec((tr, d), lambda i: (i, 0))
    vec = pl.BlockSpec((1, d), lambda i: (0, 0))
    outs = pl.pallas_call(
        functools.partial(_resnorm_kernel, n_u=n_u),
        grid=(s // tr,),
        in_specs=[row, row, vec] + [vec] * n_u,
        out_specs=[row] + [row] * n_u,
        out_shape=[jax.ShapeDtypeStruct((s, d), F32)] + [jax.ShapeDtypeStruct((s, d), BF16)] * n_u,
        compiler_params=_params(("parallel",), est),
        name="resnorm",
    )(r, m, gpost.reshape(1, d), *[g.reshape(1, d) for g in gains])
    return outs


def _matmul_kernel(a_ref, w_ref, *rest, has_bias):
    o_ref = rest[-1]
    acc = jnp.dot(a_ref[...], w_ref[...].astype(BF16), preferred_element_type=F32)
    if has_bias:
        acc = acc + rest[0][...]
    o_ref[...] = acc.astype(o_ref.dtype)


def _matmul(a, w, bias, out_dtype, tm_pref, tn_pref, name, a_buffers=2, col0=0, n=None, layer=None):
    m, k = a.shape
    assert (w.ndim == 3) == (layer is not None)
    n = w.shape[-1] if n is None else n
    tm = _tile(m, tm_pref, SUBLANES)
    tn = _tile(math.gcd(n, col0) if col0 else n, tn_pref, LANES)
    assert n % tn == 0 and col0 % tn == 0
    joff = col0 // tn
    osz = jnp.dtype(out_dtype).itemsize
    wsz = jnp.dtype(w.dtype).itemsize
    est = a_buffers * tm * k * 2 + 2 * (k * tn * wsz + tm * tn * osz) + tm * tn * 4 + (k * tn * 2 if wsz != 2 else 0)
    a_mode = {} if a_buffers == 2 else {"pipeline_mode": pl.Buffered(a_buffers)}
    if layer is None:
        w_spec = pl.BlockSpec((k, tn), lambda i, j: (0, j + joff))
    else:
        w_spec = pl.BlockSpec((None, k, tn), lambda i, j: (layer, 0, j + joff))
    in_specs = [pl.BlockSpec((tm, k), lambda i, j: (i, 0), **a_mode), w_spec]
    args = [a, w]
    if bias is not None:
        in_specs.append(pl.BlockSpec((1, tn), lambda i, j: (0, j)))
        args.append(bias.reshape(1, n).astype(F32))
    return pl.pallas_call(
        functools.partial(_matmul_kernel, has_bias=bias is not None),
        grid=(m // tm, n // tn),
        in_specs=in_specs,
        out_specs=pl.BlockSpec((tm, tn), lambda i, j: (i, j)),
        out_shape=jax.ShapeDtypeStruct((m, n), out_dtype),
        compiler_params=_params(("parallel", "parallel"), est),
        name=name,
    )(*args)


FFN_K_CHUNK = 256


BF16_TILE_ROWS = 16


def _zero_after(x):
    w = pltpu.bitcast(x, jnp.uint32)
    return (w >> 16) >> 16


def _ffn_up_kernel(u_ref, wg_ref, wv_ref, cwg_ref, cwv_ref, cbg_ref, cbv_ref, o_ref, halo, ext_a, ext_b, wcat,
                   *, nj):
    t = pl.program_id(0)
    tm = u_ref.shape[0]
    d = u_ref.shape[1]
    tn = wg_ref.shape[1]
    nch = d // FFN_K_CHUNK
    rows = tm // nch
    jp = lax.rem(jnp.maximum(t - 1, 0), nj)

    @pl.when(t == 0)
    def _():
        halo[...] = jnp.zeros(halo.shape, F32)
        ext_b[...] = jnp.zeros(ext_b.shape, F32)

    def step(ext_mm, ext_ep):
        cw = jnp.concatenate([cwg_ref[...], cwv_ref[...]], axis=1)
        cb = jnp.concatenate([cbg_ref[...], cbv_ref[...]], axis=1)
        prev = halo[jp]
        for c in range(nch):
            cur = ext_ep[c * rows:(c + 1) * rows, :]
            both = jnp.concatenate([prev, cur], axis=0)
            acc = cb + both[SUBLANES - 2:SUBLANES - 2 + rows, :] * cw[0:1, :]
            acc = acc + both[SUBLANES - 1:SUBLANES - 1 + rows, :] * cw[1:2, :]
            acc = acc + cur * cw[2:3, :]
            out = ((acc[:, 0:tn] * _sigmoid(acc[:, 0:tn])) * acc[:, tn:2 * tn]).astype(o_ref.dtype)
            o_ref[c * rows:(c + 1) * rows, :] = out
            prev = cur[rows - SUBLANES:rows, :]

            zero = _zero_after(out[0:BF16_TILE_ROWS, :])
            zero = jnp.concatenate([zero, zero], axis=0)
            k0 = c * FFN_K_CHUNK
            for w_ref, lane0 in ((wg_ref, 0), (wv_ref, tn)):
                top = pltpu.bitcast(pltpu.bitcast(w_ref[k0:k0 + BF16_TILE_ROWS, :], jnp.uint32) | zero, F32)
                wcat[k0:k0 + BF16_TILE_ROWS, lane0:lane0 + tn] = top.astype(BF16)
                wcat[k0 + BF16_TILE_ROWS:k0 + FFN_K_CHUNK, lane0:lane0 + tn] = (
                    w_ref[k0 + BF16_TILE_ROWS:k0 + FFN_K_CHUNK, :].astype(BF16))
        halo[jp] = prev
        ext_mm[...] = jnp.dot(u_ref[...], wcat[...], preferred_element_type=F32)

    @pl.when(lax.rem(t, 2) == 0)
    def _():
        step(ext_a, ext_b)

    @pl.when(lax.rem(t, 2) == 1)
    def _():
        step(ext_b, ext_a)


def _ffn_up(u, w_up, conv_w, conv_b, layer):
    s, d = u.shape
    f = w_up.shape[-1] // 2
    assert d % FFN_K_CHUNK == 0
    nch = d // FFN_K_CHUNK
    tm = _tile(s, 1024, nch * BF16_TILE_ROWS)
    tn = _tile(f, 256, LANES)
    nj = f // tn
    wsz = jnp.dtype(w_up.dtype).itemsize
    est = 2 * (tm * d * 2 + 2 * d * tn * wsz + tm * tn * 2) + d * 2 * tn * 2 + nj * SUBLANES * 2 * tn * 4 \
        + 4 * tm * 2 * tn * 4
    cb = conv_b.reshape(conv_b.shape[0], 1, 2 * f)
    n_tiles = (s // tm) * nj

    def mm_tile(t):
        return jnp.minimum(t, n_tiles - 1)

    def ep_tile(t):
        return jnp.maximum(t - 1, 0)

    return pl.pallas_call(
        functools.partial(_ffn_up_kernel, nj=nj),
        grid=(n_tiles + 1,),
        in_specs=[
            pl.BlockSpec((tm, d), lambda t: (mm_tile(t) // nj, 0)),
            pl.BlockSpec((None, d, tn), lambda t: (layer, 0, mm_tile(t) % nj)),
            pl.BlockSpec((None, d, tn), lambda t: (layer, 0, mm_tile(t) % nj + nj)),
            pl.BlockSpec((None, FFN_CONV, tn), lambda t: (layer, 0, ep_tile(t) % nj)),
            pl.BlockSpec((None, FFN_CONV, tn), lambda t: (layer, 0, ep_tile(t) % nj + nj)),
            pl.BlockSpec((None, 1, tn), lambda t: (layer, 0, ep_tile(t) % nj)),
            pl.BlockSpec((None, 1, tn), lambda t: (layer, 0, ep_tile(t) % nj + nj)),
        ],
        out_specs=pl.BlockSpec((tm, tn), lambda t: (ep_tile(t) // nj, ep_tile(t) % nj)),
        out_shape=jax.ShapeDtypeStruct((s, f), BF16),
        scratch_shapes=[
            pltpu.VMEM((nj, SUBLANES, 2 * tn), F32),
            pltpu.VMEM((tm, 2 * tn), F32),
            pltpu.VMEM((tm, 2 * tn), F32),
            pltpu.VMEM((d, 2 * tn), BF16),
        ],
        compiler_params=_params(("arbitrary",), est),
        name="ffn_up",
    )(u, w_up, w_up, conv_w, conv_w, cb, cb)


def _ssd_kernel(z_ref, x_ref, b_ref, c_ref, dt_ref, dtt_ref,
                cwx_ref, cwb_ref, cwc_ref, cbx_ref, cbb_ref, cbc_ref,
                dtb_row_ref, dtb_col_ref, alog_row_ref, alog_col_ref, dskip_ref, gnorm_ref, expand_ref,
                y_ref, state_ref, ext_x, ext_b, ext_c, ydiag_ref, *, heads):
    chunk = pl.program_id(1)
    cl = SSM_CHUNK
    width = heads * SSM_HEAD_DIM

    @pl.when(chunk == 0)
    def _():
        state_ref[...] = jnp.zeros(state_ref.shape, F32)
        ext_x[0:SUBLANES, :] = jnp.zeros((SUBLANES, ext_x.shape[1]), F32)
        ext_b[0:SUBLANES, :] = jnp.zeros((SUBLANES, ext_b.shape[1]), F32)
        ext_c[0:SUBLANES, :] = jnp.zeros((SUBLANES, ext_c.shape[1]), F32)

    def conv_silu(cur_ref, ext, cw_ref, cb_ref):
        cur = cur_ref[...]
        ext[SUBLANES:SUBLANES + cl, :] = cur
        acc = cb_ref[...] + ext[pl.ds(SUBLANES - 3, cl), :] * cw_ref[0:1, :]
        acc = acc + ext[pl.ds(SUBLANES - 2, cl), :] * cw_ref[1:2, :]
        acc = acc + ext[pl.ds(SUBLANES - 1, cl), :] * cw_ref[2:3, :]
        acc = acc + cur * cw_ref[3:4, :]
        ext[0:SUBLANES, :] = cur[cl - SUBLANES:cl, :]
        return acc * _sigmoid(acc)

    xs = conv_silu(x_ref, ext_x, cwx_ref, cbx_ref)
    bc = conv_silu(b_ref, ext_b, cwb_ref, cbb_ref)
    cc = conv_silu(c_ref, ext_c, cwc_ref, cbc_ref)

    li = lax.broadcasted_iota(jnp.int32, (cl, cl), 0)
    si = lax.broadcasted_iota(jnp.int32, (cl, cl), 1)
    tril = li >= si

    dtv = _softplus(dt_ref[...] + dtb_row_ref[...])
    a = dtv * (-jnp.exp(alog_row_ref[...]))
    ones_lo = tril.astype(F32).astype(BF16)
    ones_up = (li <= si).astype(F32).astype(BF16)
    acum = jnp.dot(jnp.concatenate([ones_lo] * 3, axis=1), jnp.concatenate(_split3(a), axis=0),
                   preferred_element_type=F32)
    dtv_t = _softplus(dtt_ref[...] + dtb_col_ref[...])
    a_t = dtv_t * (-jnp.exp(alog_col_ref[...]))
    acum_t = jnp.dot(jnp.concatenate(_split3(a_t), axis=1), jnp.concatenate([ones_up] * 3, axis=0),
                     preferred_element_type=F32)
    alast = acum[cl - 1:cl, :]
    exp_acum = jnp.exp(acum)
    decay_end = jnp.exp(alast - acum)

    def widen(v):
        return jnp.dot(jnp.concatenate(_split3(v), axis=1), expand_ref[...], preferred_element_type=F32)

    dt_w = widen(dtv)
    exp_acum_w = widen(exp_acum)
    decay_end_w = widen(decay_end)

    xdt = xs * dt_w
    cb = lax.dot_general(cc.astype(BF16), bc.astype(BF16), (((1,), (1,)), ((), ())),
                         preferred_element_type=F32)
    cb = jnp.where(tril, cb, 0.0)

    lane = lax.broadcasted_iota(jnp.int32, (cl, LANES), 1)
    low = lane < SSM_HEAD_DIM
    for p in range(heads // 2):
        ms = []
        for hp in range(2):
            r = 2 * p + hp
            seg = acum[:, r:r + 1] - acum_t[r:r + 1, :]
            decay = jnp.exp(jnp.minimum(seg, 0.0))
            ms.append((cb * decay).astype(BF16))
        m_pair = jnp.concatenate(ms, axis=1)
        xp = xdt[:, p * LANES:(p + 1) * LANES]
        rhs = jnp.concatenate([jnp.where(low, xp, 0.0), jnp.where(low, 0.0, xp)], axis=0).astype(BF16)
        ydiag_ref[:, p * LANES:(p + 1) * LANES] = jnp.dot(m_pair, rhs, preferred_element_type=F32)

    state = state_ref[...]
    y_off = jnp.dot(cc.astype(BF16), state.astype(BF16), preferred_element_type=F32) * exp_acum_w
    y = ydiag_ref[...] + y_off + dskip_ref[...] * xs

    xw = (xdt * decay_end_w).astype(BF16)
    upd = jnp.dot(jnp.transpose(bc).astype(BF16), xw, preferred_element_type=F32)
    state_ref[...] = state * exp_acum_w[cl - 1:cl, :] + upd

    zb = z_ref[...]
    yz = y * (zb * _sigmoid(zb))
    ms2 = jnp.mean(yz * yz, axis=-1, keepdims=True)
    y_ref[...] = ((yz * lax.rsqrt(ms2 + SSM_NORM_EPS)) * gnorm_ref[...]).astype(y_ref.dtype)


def _ssd(zxbc, dt_raw, conv_w, conv_b, dt_bias, a_log, d_skip, g_norm, d_inner, n_heads):
    s = zxbc.shape[0]
    g = SSM_GROUPS
    heads = n_heads // g
    width = d_inner // g
    assert width == heads * SSM_HEAD_DIM and heads % 2 == 0 and heads <= LANES and width % LANES == 0
    assert D_STATE == LANES and s % SSM_CHUNK == 0
    hp = max(SUBLANES, heads)
    nc = s // SSM_CHUNK
    cl = SSM_CHUNK

    dt_g = dt_raw.reshape(s, g, heads).transpose(1, 0, 2)
    dt_row = jnp.pad(dt_g, ((0, 0), (0, 0), (0, LANES - heads)))
    dt_col = jnp.pad(dt_g.transpose(0, 2, 1), ((0, 0), (0, hp - heads), (0, 0)))

    def rowvec(v):
        return jnp.pad(v.astype(F32).reshape(g, 1, heads), ((0, 0), (0, 0), (0, LANES - heads)))

    def colvec(v):
        return jnp.pad(v.astype(F32).reshape(g, heads, 1), ((0, 0), (0, hp - heads), (0, 0)))

    dskip_w = jnp.repeat(d_skip.astype(F32), SSM_HEAD_DIM).reshape(1, d_inner)
    expand3 = np.tile(np.arange(LANES)[:, None] == (np.arange(width)[None, :] // SSM_HEAD_DIM), (3, 1))
    cbias = conv_b.reshape(1, -1)
    wb = width // LANES
    xoff = d_inner // width
    boff = 2 * d_inner // LANES
    coff = boff + g * D_STATE // LANES
    cwb_off = d_inner // LANES
    cwc_off = cwb_off + g * D_STATE // LANES

    est = 2 * (2 * cl * width * 4 + 2 * cl * LANES * 4 + cl * LANES * 4 + hp * cl * 4 + cl * width * 2) \
        + D_STATE * width * 4 + 3 * (cl + SUBLANES) * width * 4 + cl * width * 4 + 16 * cl * width * 4
    return pl.pallas_call(
        functools.partial(_ssd_kernel, heads=heads),
        grid=(g, nc),
        in_specs=[
            pl.BlockSpec((cl, width), lambda gi, c: (c, gi)),
            pl.BlockSpec((cl, width), lambda gi, c: (c, xoff + gi)),
            pl.BlockSpec((cl, D_STATE), lambda gi, c: (c, boff + gi)),
            pl.BlockSpec((cl, D_STATE), lambda gi, c: (c, coff + gi)),
            pl.BlockSpec((None, cl, LANES), lambda gi, c: (gi, c, 0)),
            pl.BlockSpec((None, hp, cl), lambda gi, c: (gi, 0, c)),
            pl.BlockSpec((SSM_CONV, width), lambda gi, c: (0, gi)),
            pl.BlockSpec((SSM_CONV, D_STATE), lambda gi, c: (0, cwb_off + gi)),
            pl.BlockSpec((SSM_CONV, D_STATE), lambda gi, c: (0, cwc_off + gi)),
            pl.BlockSpec((1, width), lambda gi, c: (0, gi)),
            pl.BlockSpec((1, D_STATE), lambda gi, c: (0, cwb_off + gi)),
            pl.BlockSpec((1, D_STATE), lambda gi, c: (0, cwc_off + gi)),
            pl.BlockSpec((None, 1, LANES), lambda gi, c: (gi, 0, 0)),
            pl.BlockSpec((None, hp, 1), lambda gi, c: (gi, 0, 0)),
            pl.BlockSpec((None, 1, LANES), lambda gi, c: (gi, 0, 0)),
            pl.BlockSpec((None, hp, 1), lambda gi, c: (gi, 0, 0)),
            pl.BlockSpec((1, width), lambda gi, c: (0, gi)),
            pl.BlockSpec((1, width), lambda gi, c: (0, gi)),
            pl.BlockSpec((3 * LANES, width), lambda gi, c: (0, 0)),
        ],
        out_specs=pl.BlockSpec((cl, width), lambda gi, c: (c, gi)),
        out_shape=jax.ShapeDtypeStruct((s, d_inner), BF16),
        scratch_shapes=[
            pltpu.VMEM((D_STATE, width), F32),
            pltpu.VMEM((cl + SUBLANES, width), F32),
            pltpu.VMEM((cl + SUBLANES, D_STATE), F32),
            pltpu.VMEM((cl + SUBLANES, D_STATE), F32),
            pltpu.VMEM((cl, width), F32),
        ],
        compiler_params=_params(("parallel", "arbitrary"), est),
        name="ssd_scan",
    )(zxbc, zxbc, zxbc, zxbc, dt_row, dt_col, conv_w, conv_w, conv_w, cbias, cbias, cbias,
      rowvec(dt_bias), colvec(dt_bias), rowvec(a_log), colvec(a_log), dskip_w, g_norm.reshape(1, d_inner),
      jnp.asarray(expand3, BF16))


def _t5_bucket_table():
    q = np.arange(ATTN_BLOCK)[:, None]
    k = np.arange(2 * ATTN_BLOCK)[None, :]
    rel = np.maximum(q - k + ATTN_BLOCK, 0)
    max_exact = N_BUCKETS // 2
    relf = np.maximum(rel, 1).astype(np.float32)
    large = max_exact + (np.log(relf / np.float32(max_exact)) / np.float32(math.log(MAX_DISTANCE / max_exact))
                         * np.float32(N_BUCKETS - max_exact)).astype(np.int32)
    large = np.minimum(large, N_BUCKETS - 1)
    return np.where(rel < max_exact, rel, large).astype(np.int32)


def _attn_kernel(relb_ref, sink_ref, bucket_ref, q_ref, kp_ref, kc_ref, vp_ref, vc_ref, o_ref, bias_ref,
                 *, qpk, n_q_heads):
    gp = pl.program_id(0)
    n = pl.program_id(1)
    blk = ATTN_BLOCK
    npair = qpk // 2
    head0 = gp * 2 * qpk

    @pl.when(n == 0)
    def _():
        bucket = bucket_ref[...]
        qi = lax.broadcasted_iota(jnp.int32, (blk, 2 * blk), 0)
        ci = lax.broadcasted_iota(jnp.int32, (blk, 2 * blk), 1)
        rel = qi - ci + blk
        in_window = (rel >= 0) & (rel < WINDOW)

        def body(it, carry):
            kvh = it // npair
            pair = it - kvh * npair
            row0 = pl.multiple_of(pair * blk, blk)
            for hp in range(2):
                head = head0 + kvh * qpk + pair * 2 + hp
                acc = jnp.zeros((blk, 2 * blk), F32)
                for b in range(N_BUCKETS):
                    acc = jnp.where(bucket == b, relb_ref[b * n_q_heads + head], acc)
                bias_ref[kvh, pl.ds(row0, blk), hp * 2 * blk:(hp + 1) * 2 * blk] = jnp.where(in_window, acc, -jnp.inf)
            return carry

        lax.fori_loop(0, 2 * npair, body, 0)

    kk = jnp.concatenate([kp_ref[...], kc_ref[...]], axis=0)
    vv = jnp.concatenate([vp_ref[...], vc_ref[...]], axis=0)
    lane = lax.broadcasted_iota(jnp.int32, (2 * blk, LANES), 1)
    low = lane < ATTN_HEAD_DIM
    scale = ATTN_HEAD_DIM ** -0.5

    def pair_operand(t, kvh, mult):
        sel = jnp.where(low if kvh == 0 else jnp.logical_not(low), t, 0.0)
        dup = sel + pltpu.roll(sel, ATTN_HEAD_DIM, axis=1)
        if mult is not None:
            dup = dup * mult
        return jnp.concatenate([jnp.where(low, dup, 0.0), jnp.where(low, 0.0, dup)], axis=0).astype(BF16)

    row_bd = lax.broadcasted_iota(jnp.int32, (4 * blk, LANES), 0)
    lane_bd = lax.broadcasted_iota(jnp.int32, (4 * blk, LANES), 1)
    ones_bd = ((row_bd < 2 * blk) == (lane_bd < ATTN_HEAD_DIM)).astype(F32).astype(BF16)
    low_out = lax.broadcasted_iota(jnp.int32, (npair * blk, LANES), 1) < ATTN_HEAD_DIM

    def run(first_block):
        if first_block:
            ci = lax.broadcasted_iota(jnp.int32, (npair * blk, 2 * blk), 1)
            before_start = ci < blk
        for kvh in range(2):
            kbd = pair_operand(kk, kvh, scale)
            vbd = pair_operand(vv, kvh, None)
            base = kvh * npair
            qs = jnp.concatenate([q_ref[:, (base + i) * LANES:(base + i + 1) * LANES] for i in range(npair)], axis=0)
            s = lax.dot_general(qs, kbd, (((1,), (1,)), ((), ())), preferred_element_type=F32)
            es = []
            sink_terms = []
            for hp in range(2):
                sinks = [sink_ref[head0 + kvh * qpk + i * 2 + hp] for i in range(npair)]
                sink_col = jnp.concatenate([jnp.full((blk, 1), v, F32) for v in sinks], axis=0)
                sink_all = jnp.concatenate([jnp.full((blk, LANES), v, F32) for v in sinks], axis=0)
                sh = s[:, hp * 2 * blk:(hp + 1) * 2 * blk] + bias_ref[kvh, :, hp * 2 * blk:(hp + 1) * 2 * blk]
                if first_block:
                    sh = jnp.where(before_start, -jnp.inf, sh)
                m = jnp.maximum(jnp.max(sh, axis=-1, keepdims=True), sink_col)
                es.append(jnp.exp(sh - m).astype(BF16))
                sink_terms.append(jnp.exp(sink_all - jnp.broadcast_to(m, sink_all.shape)))
            e = jnp.concatenate(es, axis=1)
            ov = jnp.dot(e, jnp.concatenate([vbd, ones_bd], axis=1), preferred_element_type=F32)
            denom = ov[:, LANES:2 * LANES] + jnp.where(low_out, sink_terms[0], sink_terms[1])
            o = (ov[:, 0:LANES] * (1.0 / denom)).astype(o_ref.dtype)
            for i in range(npair):
                o_ref[:, (base + i) * LANES:(base + i + 1) * LANES] = o[i * blk:(i + 1) * blk, :]

    @pl.when(n == 0)
    def _():
        run(True)

    @pl.when(n != 0)
    def _():
        run(False)


def _attention(q, kv, sinks, rel_bias):
    s, qd = q.shape
    n_q_heads = qd // ATTN_HEAD_DIM
    qpk = n_q_heads // N_KV_HEADS
    assert qpk % 2 == 0 and N_KV_HEADS % 2 == 0 and 2 * ATTN_HEAD_DIM == LANES and s % ATTN_BLOCK == 0
    blk = ATTN_BLOCK
    nb = s // blk
    ngp = N_KV_HEADS // 2
    qw = 2 * qpk * ATTN_HEAD_DIM
    voff = N_KV_HEADS * ATTN_HEAD_DIM // LANES
    bucket = jnp.asarray(_t5_bucket_table())
    est = 2 * (2 * blk * qw * 2 + 4 * blk * LANES * 4 + blk * 2 * blk * 4) + 2 * qpk * blk * 2 * blk * 4 \
        + 24 * blk * 4 * blk * 4
    smem = pl.BlockSpec(memory_space=pltpu.SMEM)
    return pl.pallas_call(
        functools.partial(_attn_kernel, qpk=qpk, n_q_heads=n_q_heads),
        grid=(ngp, nb),
        in_specs=[
            smem, smem,
            pl.BlockSpec((blk, 2 * blk), lambda g, n: (0, 0)),
            pl.BlockSpec((blk, qw), lambda g, n: (n, g)),
            pl.BlockSpec((blk, LANES), lambda g, n: (jnp.maximum(n - 1, 0), g)),
            pl.BlockSpec((blk, LANES), lambda g, n: (n, g)),
            pl.BlockSpec((blk, LANES), lambda g, n: (jnp.maximum(n - 1, 0), voff + g)),
            pl.BlockSpec((blk, LANES), lambda g, n: (n, voff + g)),
        ],
        out_specs=pl.BlockSpec((blk, qw), lambda g, n: (n, g)),
        out_shape=jax.ShapeDtypeStruct((s, qd), BF16),
        scratch_shapes=[pltpu.VMEM((2, (qpk // 2) * blk, 4 * blk), F32)],
        compiler_params=_params(("parallel", "arbitrary"), est),
        name="swa_attention",
    )(rel_bias.astype(F32).reshape(-1), sinks.astype(F32).reshape(-1), bucket, q, kv, kv, kv, kv)


def kernel(x, norm_mix_pre, norm_mix_post, norm_ffn_pre, norm_ffn_post, ssm_w_in, ssm_conv_w, ssm_conv_b, ssm_dt_bias, ssm_a_log, ssm_d, ssm_norm, ssm_w_out, kv_norm, w_kv, b_kv, attn_w_q, attn_b_q, attn_sinks, attn_w_o, attn_b_o, rel_bias, ffn_w_up, ffn_conv_w, ffn_conv_b, ffn_w_down):
    bsz, s, d = x.shape
    assert bsz == 1 and norm_mix_pre.shape[0] == 2
    d_inner = ssm_norm.shape[-1]
    n_heads = ssm_dt_bias.shape[-1]
    zxbc_dim = ssm_w_in.shape[-1] - n_heads

    def ffn(h_in_u, layer):
        hff = _ffn_up(h_in_u, ffn_w_up, ffn_conv_w, ffn_conv_b, layer)
        return _matmul(hff, ffn_w_down, None, F32, 1024, 256, "ffn_down", a_buffers=1, layer=layer)

    h0 = x.reshape(s, d)

    u = _prenorm(h0, norm_mix_pre[0])
    zxbc = _matmul(u, ssm_w_in, None, F32, 1024, 512, "in_proj", n=zxbc_dim, layer=0)
    dt_raw = _matmul(u, ssm_w_in, None, F32, 1024, 128, "dt_proj", col0=zxbc_dim, n=n_heads, layer=0)
    y = _ssd(zxbc, dt_raw, ssm_conv_w[0], ssm_conv_b[0], ssm_dt_bias[0], ssm_a_log[0], ssm_d[0], ssm_norm[0],
             d_inner, n_heads)
    mix = _matmul(y, ssm_w_out, None, F32, 1024, 256, "out_proj", a_buffers=1, layer=0)
    h1, u = _resnorm(h0, mix, norm_mix_post[0], [norm_ffn_pre[0]])
    f = ffn(u, 0)

    h2, ukv, uq = _resnorm(h1, f, norm_ffn_post[0], [kv_norm, norm_mix_pre[1]])
    kv = _matmul(ukv, w_kv, b_kv, F32, 1024, 512, "kv_proj")
    q = _matmul(uq, attn_w_q, attn_b_q[0], BF16, 1024, 512, "q_proj", layer=0)
    o = _attention(q, kv, attn_sinks[0], rel_bias)
    mix = _matmul(o, attn_w_o, attn_b_o[0], F32, 1024, 512, "o_proj", layer=0)
    h3, u = _resnorm(h2, mix, norm_mix_post[1], [norm_ffn_pre[1]])
    f = ffn(u, 1)
    (h4,) = _resnorm(h3, f, norm_ffn_post[1], [])
    return h4.reshape(bsz, s, d)
```

```python
import functools
import math

import numpy as np
import jax
import jax.numpy as jnp
from jax import lax
from jax.experimental import pallas as pl
from jax.experimental.pallas import tpu as pltpu

EPS = 1e-6
SSM_NORM_EPS = 1e-5
SSM_HEAD_DIM = 64
SSM_GROUPS = 8
D_STATE = 128
SSM_CONV = 4
SSM_CHUNK = 128
ATTN_HEAD_DIM = 64
N_KV_HEADS = 8
WINDOW = 128
ATTN_BLOCK = 128
N_BUCKETS = 32
MAX_DISTANCE = 128
FFN_CONV = 3

LANES = 128
SUBLANES = 8
BF16_TILE_ROWS = 16
MXU_K_CHUNK = 256
VMEM_CAP_BYTES = 60 * 1024 * 1024

F32 = jnp.float32
BF16 = jnp.bfloat16


def _vmem_limit(est_bytes):
    return int(min(VMEM_CAP_BYTES, max(32 * 1024 * 1024, est_bytes * 5 // 4 + (4 << 20))))


def _params(semantics, est_bytes, flags=None):
    return pltpu.CompilerParams(dimension_semantics=semantics, vmem_limit_bytes=_vmem_limit(est_bytes), flags=flags)


def _tile(dim, pref, align):
    if dim <= pref:
        return dim
    t = (pref // align) * align
    while t >= align:
        if dim % t == 0:
            return t
        t -= align
    raise ValueError(f"no tile for {dim} (pref {pref}, align {align})")


def _sigmoid(x):
    return 0.5 * jnp.tanh(0.5 * x) + 0.5


def _softplus(x):
    return jnp.maximum(x, 0.0) + jnp.log1p(jnp.exp(-jnp.abs(x)))


def _split3(v):
    hi = v.astype(BF16)
    r1 = v - hi.astype(F32)
    mid = r1.astype(BF16)
    lo = (r1 - mid.astype(F32)).astype(BF16)
    return [hi, mid, lo]


def _rms(x, g, eps):
    ms = jnp.mean(x * x, axis=-1, keepdims=True)
    return (x * lax.rsqrt(ms + eps)) * g


def _prenorm_kernel(x_ref, g_ref, u_ref):
    u_ref[...] = _rms(x_ref[...], g_ref[...], EPS).astype(u_ref.dtype)


def _prenorm(x, g):
    s, d = x.shape
    tr = _tile(s, 256, SUBLANES)
    est = 2 * tr * d * (4 + 2)
    return pl.pallas_call(
        _prenorm_kernel,
        grid=(s // tr,),
        in_specs=[pl.BlockSpec((tr, d), lambda i: (i, 0)), pl.BlockSpec((1, d), lambda i: (0, 0))],
        out_specs=pl.BlockSpec((tr, d), lambda i: (i, 0)),
        out_shape=jax.ShapeDtypeStruct((s, d), BF16),
        compiler_params=_params(("parallel",), est),
        name="prenorm",
    )(x, g.reshape(1, d))


def _resnorm_kernel(r_ref, m_ref, gpost_ref, *rest, n_u):
    g_refs = rest[:n_u]
    h_ref = rest[n_u]
    u_refs = rest[n_u + 1:]
    h = r_ref[...] + _rms(m_ref[...], gpost_ref[...], EPS)
    h_ref[...] = h
    if n_u:
        ms = jnp.mean(h * h, axis=-1, keepdims=True)
        hn = h * lax.rsqrt(ms + EPS)
        for g_ref, u_ref in zip(g_refs, u_refs):
            u_ref[...] = (hn * g_ref[...]).astype(u_ref.dtype)


def _resnorm(r, m, gpost, gains):
    s, d = r.shape
    n_u = len(gains)
    tr = _tile(s, 128, SUBLANES)
    est = 2 * tr * d * (4 * 3 + 2 * n_u)
    row = pl.BlockSpec((tr, d), lambda i: (i, 0))
    vec = pl.BlockSpec((1, d), lambda i: (0, 0))
    outs = pl.pallas_call(
        functools.partial(_resnorm_kernel, n_u=n_u),
        grid=(s // tr,),
        in_specs=[row, row, vec] + [vec] * n_u,
        out_specs=[row] + [row] * n_u,
        out_shape=[jax.ShapeDtypeStruct((s, d), F32)] + [jax.ShapeDtypeStruct((s, d), BF16)] * n_u,
        compiler_params=_params(("parallel",), est),
        name="resnorm",
    )(r, m, gpost.reshape(1, d), *[g.reshape(1, d) for g in gains])
    return outs


def _matmul_kernel(a_ref, w_ref, *rest, has_bias):
    o_ref = rest[-1]
    acc = jnp.dot(a_ref[...], w_ref[...].astype(BF16), preferred_element_type=F32)
    if has_bias:
        acc = acc + rest[0][...]
    o_ref[...] = acc.astype(o_ref.dtype)


def _matmul(a, w, bias, out_dtype, tm_pref, tn_pref, name, a_buffers=2, col0=0, n=None, layer=None):
    m, k = a.shape
    assert (w.ndim == 3) == (layer is not None)
    n = w.shape[-1] if n is None else n
    tm = _tile(m, tm_pref, SUBLANES)
    tn = _tile(math.gcd(n, col0) if col0 else n, tn_pref, LANES)
    assert n % tn == 0 and col0 % tn == 0
    joff = col0 // tn
    osz = jnp.dtype(out_dtype).itemsize
    wsz = jnp.dtype(w.dtype).itemsize
    est = a_buffers * tm * k * 2 + 2 * (k * tn * wsz + tm * tn * osz) + tm * tn * 4 + (k * tn * 2 if wsz != 2 else 0)
    a_mode = {} if a_buffers == 2 else {"pipeline_mode": pl.Buffered(a_buffers)}
    if layer is None:
        w_spec = pl.BlockSpec((k, tn), lambda i, j: (0, j + joff))
    else:
        w_spec = pl.BlockSpec((None, k, tn), lambda i, j: (layer, 0, j + joff))
    in_specs = [pl.BlockSpec((tm, k), lambda i, j: (i, 0), **a_mode), w_spec]
    args = [a, w]
    if bias is not None:
        in_specs.append(pl.BlockSpec((1, tn), lambda i, j: (0, j)))
        args.append(bias.reshape(1, n).astype(F32))
    return pl.pallas_call(
        functools.partial(_matmul_kernel, has_bias=bias is not None),
        grid=(m // tm, n // tn),
        in_specs=in_specs,
        out_specs=pl.BlockSpec((tm, tn), lambda i, j: (i, j)),
        out_shape=jax.ShapeDtypeStruct((m, n), out_dtype),
        compiler_params=_params(("parallel", "parallel"), est),
        name=name,
    )(*args)


def _zero_after(x):
    w = pltpu.bitcast(x, jnp.uint32)
    zero = (w >> 16) >> 16
    return zero if zero.shape[0] == BF16_TILE_ROWS else jnp.concatenate([zero, zero], axis=0)


def _conv_mm_kernel(u_ref, *refs, nj, n_w, taps, gate):
    w_refs = refs[0:n_w]
    cw_refs = refs[n_w:2 * n_w]
    cb_refs = refs[2 * n_w:3 * n_w]
    o_ref, halo, ext_a, ext_b, wcat = refs[3 * n_w:]
    t = pl.program_id(0)
    tm = u_ref.shape[0]
    d = u_ref.shape[1]
    tn = w_refs[0].shape[1]
    nch = d // MXU_K_CHUNK
    rows = tm // nch
    jp = lax.rem(jnp.maximum(t - 1, 0), nj)

    @pl.when(t == 0)
    def _():
        halo[...] = jnp.zeros(halo.shape, F32)
        ext_b[...] = jnp.zeros(ext_b.shape, F32)

    def step(ext_mm, ext_ep):
        cw = jnp.concatenate([r[...] for r in cw_refs], axis=1)
        cb = jnp.concatenate([r[...] for r in cb_refs], axis=1)
        prev = halo[jp]
        for c in range(nch):
            cur = ext_ep[c * rows:(c + 1) * rows, :]
            both = jnp.concatenate([prev, cur], axis=0)
            acc = cb
            for k in range(taps - 1):
                r0 = SUBLANES - (taps - 1) + k
                acc = acc + both[r0:r0 + rows, :] * cw[k:k + 1, :]
            acc = acc + cur * cw[taps - 1:taps, :]
            if gate:
                out = (acc[:, 0:tn] * _sigmoid(acc[:, 0:tn])) * acc[:, tn:2 * tn]
            else:
                out = acc * _sigmoid(acc)
            out = out.astype(o_ref.dtype)
            o_ref[c * rows:(c + 1) * rows, :] = out
            prev = cur[rows - SUBLANES:rows, :]

            zero = _zero_after(out[0:BF16_TILE_ROWS, 0:tn])
            k0 = c * MXU_K_CHUNK
            for idx, w_ref in enumerate(w_refs):
                top = pltpu.bitcast(pltpu.bitcast(w_ref[k0:k0 + BF16_TILE_ROWS, :], jnp.uint32) | zero, F32)
                wcat[k0:k0 + BF16_TILE_ROWS, idx * tn:(idx + 1) * tn] = top.astype(BF16)
                wcat[k0 + BF16_TILE_ROWS:k0 + MXU_K_CHUNK, idx * tn:(idx + 1) * tn] = (
                    w_ref[k0 + BF16_TILE_ROWS:k0 + MXU_K_CHUNK, :].astype(BF16))
        halo[jp] = prev
        ext_mm[...] = jnp.dot(u_ref[...], wcat[...], preferred_element_type=F32)

    @pl.when(lax.rem(t, 2) == 0)
    def _():
        step(ext_a, ext_b)

    @pl.when(lax.rem(t, 2) == 1)
    def _():
        step(ext_b, ext_a)


def _conv_matmul(u, w, layer, n_cols, col_blocks, conv_w, conv_b, tn, gate, out_dtype, name):
    s, d = u.shape
    n_w = len(col_blocks)
    assert n_w == (2 if gate else 1) and d % MXU_K_CHUNK == 0
    taps = conv_w.shape[0]
    assert n_cols % tn == 0
    nj = n_cols // tn
    nch = d // MXU_K_CHUNK
    tm = _tile(s, 1024, nch * BF16_TILE_ROWS)
    wsz = jnp.dtype(w.dtype).itemsize
    osz = jnp.dtype(out_dtype).itemsize
    wt = n_w * tn
    wo = tn if gate else wt
    est = 2 * (tm * d * 2 + d * wt * wsz + tm * wo * osz) + d * wt * 2 + nj * SUBLANES * wt * 4 + 4 * tm * wt * 4
    n_tiles = (s // tm) * nj

    def mm_tile(t):
        return jnp.minimum(t, n_tiles - 1)

    def ep_tile(t):
        return jnp.maximum(t - 1, 0)

    w_specs = [pl.BlockSpec((None, d, tn), lambda t, o=o: (layer, 0, mm_tile(t) % nj + o)) for o in col_blocks]
    cw_specs = [pl.BlockSpec((taps, tn), lambda t, o=o: (0, ep_tile(t) % nj + o)) for o in col_blocks]
    cb_specs = [pl.BlockSpec((1, tn), lambda t, o=o: (0, ep_tile(t) % nj + o)) for o in col_blocks]
    return pl.pallas_call(
        functools.partial(_conv_mm_kernel, nj=nj, n_w=n_w, taps=taps, gate=gate),
        grid=(n_tiles + 1,),
        in_specs=[pl.BlockSpec((tm, d), lambda t: (mm_tile(t) // nj, 0))] + w_specs + cw_specs + cb_specs,
        out_specs=pl.BlockSpec((tm, wo), lambda t: (ep_tile(t) // nj, ep_tile(t) % nj)),
        out_shape=jax.ShapeDtypeStruct((s, nj * wo), out_dtype),
        scratch_shapes=[
            pltpu.VMEM((nj, SUBLANES, wt), F32),
            pltpu.VMEM((tm, wt), F32),
            pltpu.VMEM((tm, wt), F32),
            pltpu.VMEM((d, wt), BF16),
        ],
        compiler_params=_params(("arbitrary",), est),
        name=name,
    )(u, *([w] * n_w), *([conv_w] * n_w), *([conv_b] * n_w))


def _ssd_kernel(z_ref, x_ref, b_ref, c_ref, dt_ref, dtt_ref,
                dtb_row_ref, dtb_col_ref, alog_row_ref, alog_col_ref, dskip_ref, gnorm_ref, expand_ref,
                y_ref, state_ref, ydiag_ref, *, heads):
    chunk = pl.program_id(1)
    cl = SSM_CHUNK

    @pl.when(chunk == 0)
    def _():
        state_ref[...] = jnp.zeros(state_ref.shape, F32)

    xs = x_ref[...]
    bc = b_ref[...]
    cc = c_ref[...]

    li = lax.broadcasted_iota(jnp.int32, (cl, cl), 0)
    si = lax.broadcasted_iota(jnp.int32, (cl, cl), 1)
    tril = li >= si

    dtv = _softplus(dt_ref[...] + dtb_row_ref[...])
    a = dtv * (-jnp.exp(alog_row_ref[...]))
    ones_lo = tril.astype(F32).astype(BF16)
    ones_up = (li <= si).astype(F32).astype(BF16)
    acum = jnp.dot(jnp.concatenate([ones_lo] * 3, axis=1), jnp.concatenate(_split3(a), axis=0),
                   preferred_element_type=F32)
    dtv_t = _softplus(dtt_ref[...] + dtb_col_ref[...])
    a_t = dtv_t * (-jnp.exp(alog_col_ref[...]))
    acum_t = jnp.dot(jnp.concatenate(_split3(a_t), axis=1), jnp.concatenate([ones_up] * 3, axis=0),
                     preferred_element_type=F32)
    alast = acum[cl - 1:cl, :]
    exp_acum = jnp.exp(acum)
    decay_end = jnp.exp(alast - acum)

    def widen(v):
        return jnp.dot(jnp.concatenate(_split3(v), axis=1), expand_ref[...], preferred_element_type=F32)

    dt_w = widen(dtv)
    exp_acum_w = widen(exp_acum)
    decay_end_w = widen(decay_end)

    xdt = xs * dt_w
    cb = lax.dot_general(cc.astype(BF16), bc.astype(BF16), (((1,), (1,)), ((), ())),
                         preferred_element_type=F32)
    cb = jnp.where(tril, cb, 0.0)

    lane = lax.broadcasted_iota(jnp.int32, (cl, LANES), 1)
    low = lane < SSM_HEAD_DIM
    for p in range(heads // 2):
        ms = []
        for hp in range(2):
            r = 2 * p + hp
            seg = acum[:, r:r + 1] - acum_t[r:r + 1, :]
            decay = jnp.exp(jnp.minimum(seg, 0.0))
            ms.append((cb * decay).astype(BF16))
        m_pair = jnp.concatenate(ms, axis=1)
        xp = xdt[:, p * LANES:(p + 1) * LANES]
        rhs = jnp.concatenate([jnp.where(low, xp, 0.0), jnp.where(low, 0.0, xp)], axis=0).astype(BF16)
        ydiag_ref[:, p * LANES:(p + 1) * LANES] = jnp.dot(m_pair, rhs, preferred_element_type=F32)

    state = state_ref[...]
    y_off = jnp.dot(cc.astype(BF16), state.astype(BF16), preferred_element_type=F32) * exp_acum_w
    y = ydiag_ref[...] + y_off + dskip_ref[...] * xs

    xw = (xdt * decay_end_w).astype(BF16)
    upd = jnp.dot(jnp.transpose(bc).astype(BF16), xw, preferred_element_type=F32)
    state_ref[...] = state * exp_acum_w[cl - 1:cl, :] + upd

    yz = y * z_ref[...]
    ms2 = jnp.mean(yz * yz, axis=-1, keepdims=True)
    y_ref[...] = ((yz * lax.rsqrt(ms2 + SSM_NORM_EPS)) * gnorm_ref[...]).astype(y_ref.dtype)


def _ssd(zxbc, dt_raw, dt_bias, a_log, d_skip, g_norm, d_inner, n_heads):
    s = zxbc.shape[0]
    g = SSM_GROUPS
    heads = n_heads // g
    width = d_inner // g
    assert width == heads * SSM_HEAD_DIM and heads % 2 == 0 and heads <= LANES and width % LANES == 0
    assert D_STATE == LANES and s % SSM_CHUNK == 0
    hp = max(SUBLANES, heads)
    nc = s // SSM_CHUNK
    cl = SSM_CHUNK

    dt_g = dt_raw.reshape(s, g, heads).transpose(1, 0, 2)
    dt_row = jnp.pad(dt_g, ((0, 0), (0, 0), (0, LANES - heads)))
    dt_col = jnp.pad(dt_g.transpose(0, 2, 1), ((0, 0), (0, hp - heads), (0, 0)))

    def rowvec(v):
        return jnp.pad(v.astype(F32).reshape(g, 1, heads), ((0, 0), (0, 0), (0, LANES - heads)))

    def colvec(v):
        return jnp.pad(v.astype(F32).reshape(g, heads, 1), ((0, 0), (0, hp - heads), (0, 0)))

    dskip_w = jnp.repeat(d_skip.astype(F32), SSM_HEAD_DIM).reshape(1, d_inner)
    expand3 = np.tile(np.arange(LANES)[:, None] == (np.arange(width)[None, :] // SSM_HEAD_DIM), (3, 1))
    xoff = d_inner // width
    boff = 2 * d_inner // LANES
    coff = boff + g * D_STATE // LANES

    est = 2 * (2 * cl * width * 4 + 2 * cl * LANES * 4 + cl * LANES * 4 + hp * cl * 4 + cl * width * 2) \
        + D_STATE * width * 4 + cl * width * 4 + 16 * cl * width * 4
    return pl.pallas_call(
        functools.partial(_ssd_kernel, heads=heads),
        grid=(g, nc),
        in_specs=[
            pl.BlockSpec((cl, width), lambda gi, c: (c, gi)),
            pl.BlockSpec((cl, width), lambda gi, c: (c, xoff + gi)),
            pl.BlockSpec((cl, D_STATE), lambda gi, c: (c, boff + gi)),
            pl.BlockSpec((cl, D_STATE), lambda gi, c: (c, coff + gi)),
            pl.BlockSpec((None, cl, LANES), lambda gi, c: (gi, c, 0)),
            pl.BlockSpec((None, hp, cl), lambda gi, c: (gi, 0, c)),
            pl.BlockSpec((None, 1, LANES), lambda gi, c: (gi, 0, 0)),
            pl.BlockSpec((None, hp, 1), lambda gi, c: (gi, 0, 0)),
            pl.BlockSpec((None, 1, LANES), lambda gi, c: (gi, 0, 0)),
            pl.BlockSpec((None, hp, 1), lambda gi, c: (gi, 0, 0)),
            pl.BlockSpec((1, width), lambda gi, c: (0, gi)),
            pl.BlockSpec((1, width), lambda gi, c: (0, gi)),
            pl.BlockSpec((3 * LANES, width), lambda gi, c: (0, 0)),
        ],
        out_specs=pl.BlockSpec((cl, width), lambda gi, c: (c, gi)),
        out_shape=jax.ShapeDtypeStruct((s, d_inner), BF16),
        scratch_shapes=[
            pltpu.VMEM((D_STATE, width), F32),
            pltpu.VMEM((cl, width), F32),
        ],
        compiler_params=_params(("parallel", "arbitrary"), est),
        name="ssd_scan",
    )(zxbc, zxbc, zxbc, zxbc, dt_row, dt_col,
      rowvec(dt_bias), colvec(dt_bias), rowvec(a_log), colvec(a_log), dskip_w, g_norm.reshape(1, d_inner),
      jnp.asarray(expand3, BF16))


def _t5_bucket_table():
    q = np.arange(ATTN_BLOCK)[:, None]
    k = np.arange(2 * ATTN_BLOCK)[None, :]
    rel = np.maximum(q - k + ATTN_BLOCK, 0)
    max_exact = N_BUCKETS // 2
    relf = np.maximum(rel, 1).astype(np.float32)
    large = max_exact + (np.log(relf / np.float32(max_exact)) / np.float32(math.log(MAX_DISTANCE / max_exact))
                         * np.float32(N_BUCKETS - max_exact)).astype(np.int32)
    large = np.minimum(large, N_BUCKETS - 1)
    return np.where(rel < max_exact, rel, large).astype(np.int32)


def _attn_kernel(relb_ref, sink_ref, bucket_ref, q_ref, kp_ref, kc_ref, vp_ref, vc_ref, o_ref, bias_ref,
                 *, qpk, n_q_heads):
    gp = pl.program_id(0)
    n = pl.program_id(1)
    blk = ATTN_BLOCK
    npair = qpk // 2
    head0 = gp * 2 * qpk

    @pl.when(n == 0)
    def _():
        bucket = bucket_ref[...]
        qi = lax.broadcasted_iota(jnp.int32, (blk, 2 * blk), 0)
        ci = lax.broadcasted_iota(jnp.int32, (blk, 2 * blk), 1)
        rel = qi - ci + blk
        in_window = (rel >= 0) & (rel < WINDOW)

        def body(it, carry):
            kvh = it // npair
            pair = it - kvh * npair
            row0 = pl.multiple_of(pair * blk, blk)
            for hp in range(2):
                head = head0 + kvh * qpk + pair * 2 + hp
                acc = jnp.zeros((blk, 2 * blk), F32)
                for b in range(N_BUCKETS):
                    acc = jnp.where(bucket == b, relb_ref[b * n_q_heads + head], acc)
                bias_ref[kvh, pl.ds(row0, blk), hp * 2 * blk:(hp + 1) * 2 * blk] = jnp.where(in_window, acc, -jnp.inf)
            return carry

        lax.fori_loop(0, 2 * npair, body, 0)

    kk = jnp.concatenate([kp_ref[...], kc_ref[...]], axis=0)
    vv = jnp.concatenate([vp_ref[...], vc_ref[...]], axis=0)
    lane = lax.broadcasted_iota(jnp.int32, (2 * blk, LANES), 1)
    low = lane < ATTN_HEAD_DIM
    scale = ATTN_HEAD_DIM ** -0.5

    def pair_operand(t, kvh, mult):
        sel = jnp.where(low if kvh == 0 else jnp.logical_not(low), t, 0.0)
        dup = sel + pltpu.roll(sel, ATTN_HEAD_DIM, axis=1)
        if mult is not None:
            dup = dup * mult
        return jnp.concatenate([jnp.where(low, dup, 0.0), jnp.where(low, 0.0, dup)], axis=0).astype(BF16)

    row_bd = lax.broadcasted_iota(jnp.int32, (4 * blk, LANES), 0)
    lane_bd = lax.broadcasted_iota(jnp.int32, (4 * blk, LANES), 1)
    ones_bd = ((row_bd < 2 * blk) == (lane_bd < ATTN_HEAD_DIM)).astype(F32).astype(BF16)
    low_out = lax.broadcasted_iota(jnp.int32, (npair * blk, LANES), 1) < ATTN_HEAD_DIM

    def run(first_block):
        if first_block:
            ci = lax.broadcasted_iota(jnp.int32, (npair * blk, 2 * blk), 1)
            before_start = ci < blk
        for kvh in range(2):
            kbd = pair_operand(kk, kvh, scale)
            vbd = pair_operand(vv, kvh, None)
            base = kvh * npair
            qs = jnp.concatenate([q_ref[:, (base + i) * LANES:(base + i + 1) * LANES] for i in range(npair)], axis=0)
            s = lax.dot_general(qs, kbd, (((1,), (1,)), ((), ())), preferred_element_type=F32)
            es = []
            sink_terms = []
            for hp in range(2):
                sinks = [sink_ref[head0 + kvh * qpk + i * 2 + hp] for i in range(npair)]
                sink_col = jnp.concatenate([jnp.full((blk, 1), v, F32) for v in sinks], axis=0)
                sink_all = jnp.concatenate([jnp.full((blk, LANES), v, F32) for v in sinks], axis=0)
                sh = s[:, hp * 2 * blk:(hp + 1) * 2 * blk] + bias_ref[kvh, :, hp * 2 * blk:(hp + 1) * 2 * blk]
                if first_block:
                    sh = jnp.where(before_start, -jnp.inf, sh)
                m = jnp.maximum(jnp.max(sh, axis=-1, keepdims=True), sink_col)
                es.append(jnp.exp(sh - m).astype(BF16))
                sink_terms.append(jnp.exp(sink_all - jnp.broadcast_to(m, sink_all.shape)))
            e = jnp.concatenate(es, axis=1)
            ov = jnp.dot(e, jnp.concatenate([vbd, ones_bd], axis=1), preferred_element_type=F32)
            denom = ov[:, LANES:2 * LANES] + jnp.where(low_out, sink_terms[0], sink_terms[1])
            o = (ov[:, 0:LANES] * (1.0 / denom)).astype(o_ref.dtype)
            for i in range(npair):
                o_ref[:, (base + i) * LANES:(base + i + 1) * LANES] = o[i * blk:(i + 1) * blk, :]

    @pl.when(n == 0)
    def _():
        run(True)

    @pl.when(n != 0)
    def _():
        run(False)


def _attention(q, kv, sinks, rel_bias):
    s, qd = q.shape
    n_q_heads = qd // ATTN_HEAD_DIM
    qpk = n_q_heads // N_KV_HEADS
    assert qpk % 2 == 0 and N_KV_HEADS % 2 == 0 and 2 * ATTN_HEAD_DIM == LANES and s % ATTN_BLOCK == 0
    blk = ATTN_BLOCK
    nb = s // blk
    ngp = N_KV_HEADS // 2
    qw = 2 * qpk * ATTN_HEAD_DIM
    voff = N_KV_HEADS * ATTN_HEAD_DIM // LANES
    bucket = jnp.asarray(_t5_bucket_table())
    est = 2 * (2 * blk * qw * 2 + 4 * blk * LANES * 4 + blk * 2 * blk * 4) + 2 * qpk * blk * 2 * blk * 4 \
        + 24 * blk * 4 * blk * 4
    smem = pl.BlockSpec(memory_space=pltpu.SMEM)
    return pl.pallas_call(
        functools.partial(_attn_kernel, qpk=qpk, n_q_heads=n_q_heads),
        grid=(ngp, nb),
        in_specs=[
            smem, smem,
            pl.BlockSpec((blk, 2 * blk), lambda g, n: (0, 0)),
            pl.BlockSpec((blk, qw), lambda g, n: (n, g)),
            pl.BlockSpec((blk, LANES), lambda g, n: (jnp.maximum(n - 1, 0), g)),
            pl.BlockSpec((blk, LANES), lambda g, n: (n, g)),
            pl.BlockSpec((blk, LANES), lambda g, n: (jnp.maximum(n - 1, 0), voff + g)),
            pl.BlockSpec((blk, LANES), lambda g, n: (n, voff + g)),
        ],
        out_specs=pl.BlockSpec((blk, qw), lambda g, n: (n, g)),
        out_shape=jax.ShapeDtypeStruct((s, qd), BF16),
        scratch_shapes=[pltpu.VMEM((2, (qpk // 2) * blk, 4 * blk), F32)],
        compiler_params=_params(("parallel", "arbitrary"), est),
        name="swa_attention",
    )(rel_bias.astype(F32).reshape(-1), sinks.astype(F32).reshape(-1), bucket, q, kv, kv, kv, kv)


def kernel(x, norm_mix_pre, norm_mix_post, norm_ffn_pre, norm_ffn_post, ssm_w_in, ssm_conv_w, ssm_conv_b, ssm_dt_bias, ssm_a_log, ssm_d, ssm_norm, ssm_w_out, kv_norm, w_kv, b_kv, attn_w_q, attn_b_q, attn_sinks, attn_w_o, attn_b_o, rel_bias, ffn_w_up, ffn_conv_w, ffn_conv_b, ffn_w_down):
    bsz, s, d = x.shape
    assert bsz == 1 and norm_mix_pre.shape[0] == 2
    d_inner = ssm_norm.shape[-1]
    n_heads = ssm_dt_bias.shape[-1]
    zxbc_dim = ssm_w_in.shape[-1] - n_heads

    def ffn(h_in_u, layer):
        d_ff = ffn_w_up.shape[-1] // 2
        tn = _tile(d_ff, 256, LANES)
        hff = _conv_matmul(h_in_u, ffn_w_up, layer, d_ff, [0, d_ff // tn], ffn_conv_w[layer],
                           ffn_conv_b[layer].reshape(1, -1), tn, True, BF16, "ffn_up")
        return _matmul(hff, ffn_w_down, None, F32, 1024, 256, "ffn_down", a_buffers=1, layer=layer)

    h0 = x.reshape(s, d)

    u = _prenorm(h0, norm_mix_pre[0])
    ident = jnp.zeros((SSM_CONV, d_inner), F32).at[SSM_CONV - 1].set(1.0)
    conv_w_ext = jnp.concatenate([ident, ssm_conv_w[0].astype(F32)], axis=1)
    conv_b_ext = jnp.concatenate([jnp.zeros((d_inner,), F32), ssm_conv_b[0].astype(F32)]).reshape(1, -1)
    zxbc = _conv_matmul(u, ssm_w_in, 0, zxbc_dim, [0], conv_w_ext, conv_b_ext, _tile(zxbc_dim, 512, LANES), False,
                        F32, "in_proj")
    dt_raw = _matmul(u, ssm_w_in, None, F32, 1024, 128, "dt_proj", col0=zxbc_dim, n=n_heads, layer=0)
    y = _ssd(zxbc, dt_raw, ssm_dt_bias[0], ssm_a_log[0], ssm_d[0], ssm_norm[0], d_inner, n_heads)
    mix = _matmul(y, ssm_w_out, None, F32, 1024, 256, "out_proj", a_buffers=1, layer=0)
    h1, u = _resnorm(h0, mix, norm_mix_post[0], [norm_ffn_pre[0]])
    f = ffn(u, 0)

    h2, ukv, uq = _resnorm(h1, f, norm_ffn_post[0], [kv_norm, norm_mix_pre[1]])
    kv = _matmul(ukv, w_kv, b_kv, F32, 1024, 512, "kv_proj")
    q = _matmul(uq, attn_w_q, attn_b_q[0], BF16, 1024, 512, "q_proj", layer=0)
    o = _attention(q, kv, attn_sinks[0], rel_bias)
    mix = _matmul(o, attn_w_o, attn_b_o[0], F32, 1024, 512, "o_proj", layer=0)
    h3, u = _resnorm(h2, mix, norm_mix_post[1], [norm_ffn_pre[1]])
    f = ffn(u, 1)
    (h4,) = _resnorm(h3, f, norm_ffn_post[1], [])
    return h4.reshape(bsz, s, d)
```

```python
import functools
import math

import numpy as np
import jax
import jax.numpy as jnp
from jax import lax
from jax.experimental import pallas as pl
from jax.experimental.pallas import tpu as pltpu

EPS = 1e-6
SSM_NORM_EPS = 1e-5
SSM_HEAD_DIM = 64
SSM_GROUPS = 8
D_STATE = 128
SSM_CONV = 4
SSM_CHUNK = 128
SSD_PAIRS_PER_DOT = 2
ATTN_HEAD_DIM = 64
N_KV_HEADS = 8
WINDOW = 128
ATTN_BLOCK = 128
N_BUCKETS = 32
MAX_DISTANCE = 128
FFN_CONV = 3

LANES = 128
SUBLANES = 8
BF16_TILE_ROWS = 16
MXU_K_CHUNK = 256
VMEM_CAP_BYTES = 60 * 1024 * 1024

F32 = jnp.float32
BF16 = jnp.bfloat16


def _vmem_limit(est_bytes):
    return int(min(VMEM_CAP_BYTES, max(32 * 1024 * 1024, est_bytes * 5 // 4 + (4 << 20))))


def _params(semantics, est_bytes, flags=None):
    return pltpu.CompilerParams(dimension_semantics=semantics, vmem_limit_bytes=_vmem_limit(est_bytes), flags=flags)


def _tile(dim, pref, align):
    if dim <= pref:
        return dim
    t = (pref // align) * align
    while t >= align:
        if dim % t == 0:
            return t
        t -= align
    raise ValueError(f"no tile for {dim} (pref {pref}, align {align})")


def _sigmoid(x):
    return 0.5 * jnp.tanh(0.5 * x) + 0.5


def _softplus(x):
    return jnp.maximum(x, 0.0) + jnp.log1p(jnp.exp(-jnp.abs(x)))


def _split3(v):
    hi = v.astype(BF16)
    r1 = v - hi.astype(F32)
    mid = r1.astype(BF16)
    lo = (r1 - mid.astype(F32)).astype(BF16)
    return [hi, mid, lo]


def _rms(x, g, eps):
    ms = jnp.mean(x * x, axis=-1, keepdims=True)
    return (x * lax.rsqrt(ms + eps)) * g


def _prenorm_kernel(x_ref, g_ref, u_ref):
    u_ref[...] = _rms(x_ref[...], g_ref[...], EPS).astype(u_ref.dtype)


def _prenorm(x, g):
    s, d = x.shape
    tr = _tile(s, 256, SUBLANES)
    est = 2 * tr * d * (4 + 2)
    return pl.pallas_call(
        _prenorm_kernel,
        grid=(s // tr,),
        in_specs=[pl.BlockSpec((tr, d), lambda i: (i, 0)), pl.BlockSpec((1, d), lambda i: (0, 0))],
        out_specs=pl.BlockSpec((tr, d), lambda i: (i, 0)),
        out_shape=jax.ShapeDtypeStruct((s, d), BF16),
        compiler_params=_params(("parallel",), est),
        name="prenorm",
    )(x, g.reshape(1, d))


def _resnorm_kernel(r_ref, m_ref, gpost_ref, *rest, n_u):
    g_refs = rest[:n_u]
    h_ref = rest[n_u]
    u_refs = rest[n_u + 1:]
    h = r_ref[...] + _rms(m_ref[...], gpost_ref[...], EPS)
    h_ref[...] = h
    if n_u:
        ms = jnp.mean(h * h, axis=-1, keepdims=True)
        hn = h * lax.rsqrt(ms + EPS)
        for g_ref, u_ref in zip(g_refs, u_refs):
            u_ref[...] = (hn * g_ref[...]).astype(u_ref.dtype)


def _resnorm(r, m, gpost, gains):
    s, d = r.shape
    n_u = len(gains)
    tr = _tile(s, 128, SUBLANES)
    est = 2 * tr * d * (4 * 3 + 2 * n_u)
    row = pl.BlockSpec((tr, d), lambda i: (i, 0))
    vec = pl.BlockSpec((1, d), lambda i: (0, 0))
    outs = pl.pallas_call(
        functools.partial(_resnorm_kernel, n_u=n_u),
        grid=(s // tr,),
        in_specs=[row, row, vec] + [vec] * n_u,
        out_specs=[row] + [row] * n_u,
        out_shape=[jax.ShapeDtypeStruct((s, d), F32)] + [jax.ShapeDtypeStruct((s, d), BF16)] * n_u,
        compiler_params=_params(("parallel",), est),
        name="resnorm",
    )(r, m, gpost.reshape(1, d), *[g.reshape(1, d) for g in gains])
    return outs


def _matmul_kernel(a_ref, w_ref, *rest, has_bias):
    o_ref = rest[-1]
    acc = jnp.dot(a_ref[...], w_ref[...].astype(BF16), preferred_element_type=F32)
    if has_bias:
        acc = acc + rest[0][...]
    o_ref[...] = acc.astype(o_ref.dtype)


def _matmul(a, w, bias, out_dtype, tm_pref, tn_pref, name, a_buffers=2, col0=0, n=None, layer=None):
    m, k = a.shape
    assert (w.ndim == 3) == (layer is not None)
    n = w.shape[-1] if n is None else n
    tm = _tile(m, tm_pref, SUBLANES)
    tn = _tile(math.gcd(n, col0) if col0 else n, tn_pref, LANES)
    assert n % tn == 0 and col0 % tn == 0
    joff = col0 // tn
    osz = jnp.dtype(out_dtype).itemsize
    wsz = jnp.dtype(w.dtype).itemsize
    est = a_buffers * tm * k * 2 + 2 * (k * tn * wsz + tm * tn * osz) + tm * tn * 4 + (k * tn * 2 if wsz != 2 else 0)
    a_mode = {} if a_buffers == 2 else {"pipeline_mode": pl.Buffered(a_buffers)}
    if layer is None:
        w_spec = pl.BlockSpec((k, tn), lambda i, j: (0, j + joff))
    else:
        w_spec = pl.BlockSpec((None, k, tn), lambda i, j: (layer, 0, j + joff))
    in_specs = [pl.BlockSpec((tm, k), lambda i, j: (i, 0), **a_mode), w_spec]
    args = [a, w]
    if bias is not None:
        in_specs.append(pl.BlockSpec((1, tn), lambda i, j: (0, j)))
        args.append(bias.reshape(1, n).astype(F32))
    return pl.pallas_call(
        functools.partial(_matmul_kernel, has_bias=bias is not None),
        grid=(m // tm, n // tn),
        in_specs=in_specs,
        out_specs=pl.BlockSpec((tm, tn), lambda i, j: (i, j)),
        out_shape=jax.ShapeDtypeStruct((m, n), out_dtype),
        compiler_params=_params(("parallel", "parallel"), est),
        name=name,
    )(*args)


def _zero_after(x):
    w = pltpu.bitcast(x, jnp.uint32)
    zero = (w >> 16) >> 16
    return zero if zero.shape[0] == BF16_TILE_ROWS else jnp.concatenate([zero, zero], axis=0)


def _conv_mm_kernel(u_ref, *refs, nj, n_w, taps, gate):
    w_refs = refs[0:n_w]
    cw_refs = refs[n_w:2 * n_w]
    cb_refs = refs[2 * n_w:3 * n_w]
    o_ref, halo, ext_a, ext_b, wcat = refs[3 * n_w:]
    t = pl.program_id(0)
    tm = u_ref.shape[0]
    d = u_ref.shape[1]
    tn = w_refs[0].shape[1]
    nch = d // MXU_K_CHUNK
    rows = tm // nch
    jp = lax.rem(jnp.maximum(t - 1, 0), nj)

    @pl.when(t == 0)
    def _():
        halo[...] = jnp.zeros(halo.shape, F32)
        ext_b[...] = jnp.zeros(ext_b.shape, F32)

    def step(ext_mm, ext_ep):
        cw = jnp.concatenate([r[...] for r in cw_refs], axis=1)
        cb = jnp.concatenate([r[...] for r in cb_refs], axis=1)
        prev = halo[jp]
        for c in range(nch):
            cur = ext_ep[c * rows:(c + 1) * rows, :]
            both = jnp.concatenate([prev, cur], axis=0)
            acc = cb
            for k in range(taps - 1):
                r0 = SUBLANES - (taps - 1) + k
                acc = acc + both[r0:r0 + rows, :] * cw[k:k + 1, :]
            acc = acc + cur * cw[taps - 1:taps, :]
            if gate:
                out = (acc[:, 0:tn] * _sigmoid(acc[:, 0:tn])) * acc[:, tn:2 * tn]
            else:
                out = acc * _sigmoid(acc)
            out = out.astype(o_ref.dtype)
            o_ref[c * rows:(c + 1) * rows, :] = out
            prev = cur[rows - SUBLANES:rows, :]

            zero = _zero_after(out[0:BF16_TILE_ROWS, 0:tn])
            k0 = c * MXU_K_CHUNK
            for idx, w_ref in enumerate(w_refs):
                top = pltpu.bitcast(pltpu.bitcast(w_ref[k0:k0 + BF16_TILE_ROWS, :], jnp.uint32) | zero, F32)
                wcat[k0:k0 + BF16_TILE_ROWS, idx * tn:(idx + 1) * tn] = top.astype(BF16)
                wcat[k0 + BF16_TILE_ROWS:k0 + MXU_K_CHUNK, idx * tn:(idx + 1) * tn] = (
                    w_ref[k0 + BF16_TILE_ROWS:k0 + MXU_K_CHUNK, :].astype(BF16))
        halo[jp] = prev
        ext_mm[...] = jnp.dot(u_ref[...], wcat[...], preferred_element_type=F32)

    @pl.when(lax.rem(t, 2) == 0)
    def _():
        step(ext_a, ext_b)

    @pl.when(lax.rem(t, 2) == 1)
    def _():
        step(ext_b, ext_a)


def _conv_matmul(u, w, layer, n_cols, col_blocks, conv_w, conv_b, tn, gate, out_dtype, name, conv_col_blocks=None):
    s, d = u.shape
    n_w = len(col_blocks)
    assert n_w == (2 if gate else 1) and d % MXU_K_CHUNK == 0
    taps = conv_w.shape[0]
    assert n_cols % tn == 0
    nj = n_cols // tn
    nch = d // MXU_K_CHUNK
    tm = _tile(s, 1024, nch * BF16_TILE_ROWS)
    wsz = jnp.dtype(w.dtype).itemsize
    osz = jnp.dtype(out_dtype).itemsize
    wt = n_w * tn
    wo = tn if gate else wt
    est = 2 * (tm * d * 2 + d * wt * wsz + tm * wo * osz) + d * wt * 2 + nj * SUBLANES * wt * 4 + 4 * tm * wt * 4
    n_tiles = (s // tm) * nj

    def mm_tile(t):
        return jnp.minimum(t, n_tiles - 1)

    def ep_tile(t):
        return jnp.maximum(t - 1, 0)

    w_specs = [pl.BlockSpec((None, d, tn), lambda t, o=o: (layer, 0, mm_tile(t) % nj + o)) for o in col_blocks]
    conv_col_blocks = col_blocks if conv_col_blocks is None else conv_col_blocks
    cw_specs = [pl.BlockSpec((taps, tn), lambda t, o=o: (0, ep_tile(t) % nj + o)) for o in conv_col_blocks]
    cb_specs = [pl.BlockSpec((1, tn), lambda t, o=o: (0, ep_tile(t) % nj + o)) for o in conv_col_blocks]
    return pl.pallas_call(
        functools.partial(_conv_mm_kernel, nj=nj, n_w=n_w, taps=taps, gate=gate),
        grid=(n_tiles + 1,),
        in_specs=[pl.BlockSpec((tm, d), lambda t: (mm_tile(t) // nj, 0))] + w_specs + cw_specs + cb_specs,
        out_specs=pl.BlockSpec((tm, wo), lambda t: (ep_tile(t) // nj, ep_tile(t) % nj)),
        out_shape=jax.ShapeDtypeStruct((s, nj * wo), out_dtype),
        scratch_shapes=[
            pltpu.VMEM((nj, SUBLANES, wt), F32),
            pltpu.VMEM((tm, wt), F32),
            pltpu.VMEM((tm, wt), F32),
            pltpu.VMEM((d, wt), BF16),
        ],
        compiler_params=_params(("arbitrary",), est),
        name=name,
    )(u, *([w] * n_w), *([conv_w] * n_w), *([conv_b] * n_w))


def _ssd_kernel(z_ref, x_ref, b_ref, c_ref, dt_ref, dtt_ref,
                dtb_row_ref, dtb_col_ref, alog_row_ref, alog_col_ref, dskip_ref, gnorm_ref, expand_ref,
                y_ref, state_ref, *, heads):
    chunk = pl.program_id(1)
    cl = SSM_CHUNK
    width = heads * SSM_HEAD_DIM

    @pl.when(chunk == 0)
    def _():
        state_ref[...] = jnp.zeros(state_ref.shape, F32)

    xs = x_ref[...]
    bc = b_ref[...]
    cc = c_ref[...]

    li = lax.broadcasted_iota(jnp.int32, (cl, cl), 0)
    si = lax.broadcasted_iota(jnp.int32, (cl, cl), 1)
    tril = li >= si

    dtv = _softplus(dt_ref[...] + dtb_row_ref[...])
    a = dtv * (-jnp.exp(alog_row_ref[...]))
    ones_lo = tril.astype(F32).astype(BF16)
    ones_up = (li <= si).astype(F32).astype(BF16)
    acum = jnp.dot(jnp.concatenate([ones_lo] * 3, axis=1), jnp.concatenate(_split3(a), axis=0),
                   preferred_element_type=F32)
    dtv_t = _softplus(dtt_ref[...] + dtb_col_ref[...])
    a_t = dtv_t * (-jnp.exp(alog_col_ref[...]))
    acum_t = jnp.dot(jnp.concatenate(_split3(a_t), axis=1), jnp.concatenate([ones_up] * 3, axis=0),
                     preferred_element_type=F32)
    alast = acum[cl - 1:cl, :]
    exp_acum = jnp.exp(acum)
    decay_end = jnp.exp(alast - acum)

    stacked = jnp.concatenate([dtv, exp_acum, decay_end], axis=0)
    wide = jnp.dot(jnp.concatenate(_split3(stacked)[0:2], axis=1), expand_ref[...], preferred_element_type=F32)
    dt_w = wide[0:cl, :]
    exp_acum_w = wide[cl:2 * cl, :]
    decay_end_w = wide[2 * cl:3 * cl, :]

    xdt = xs * dt_w
    state = state_ref[...]
    bc_t = jnp.transpose(bc).astype(BF16)
    c_prod = jnp.dot(cc.astype(BF16), jnp.concatenate([state.astype(BF16), bc_t], axis=1),
                     preferred_element_type=F32)
    y_off = c_prod[:, 0:width] * exp_acum_w
    cb = jnp.where(tril, c_prod[:, width:width + cl], 0.0)

    lane = lax.broadcasted_iota(jnp.int32, (cl, LANES), 1)
    low = lane < SSM_HEAD_DIM
    npairs = heads // 2
    ppd = min(SSD_PAIRS_PER_DOT, npairs)
    zero_blk = jnp.zeros((cl, LANES), BF16)
    y_diag = []
    for q in range(npairs // ppd):
        ms = []
        rhs_rows = []
        for pi in range(ppd):
            p = q * ppd + pi
            for hp in range(2):
                r = 2 * p + hp
                seg = acum[:, r:r + 1] - acum_t[r:r + 1, :]
                decay = jnp.exp(jnp.minimum(seg, 0.0))
                ms.append((cb * decay).astype(BF16))
            xp = xdt[:, p * LANES:(p + 1) * LANES]
            for blk in (jnp.where(low, xp, 0.0).astype(BF16), jnp.where(low, 0.0, xp).astype(BF16)):
                rhs_rows.append(jnp.concatenate([zero_blk] * pi + [blk] + [zero_blk] * (ppd - 1 - pi), axis=1))
        y_diag.append(jnp.dot(jnp.concatenate(ms, axis=1), jnp.concatenate(rhs_rows, axis=0),
                              preferred_element_type=F32))
    y = jnp.concatenate(y_diag, axis=1) + y_off + dskip_ref[...] * xs

    xw = (xdt * decay_end_w).astype(BF16)
    upd = jnp.dot(bc_t, xw, preferred_element_type=F32)
    state_ref[...] = state * exp_acum_w[cl - 1:cl, :] + upd

    yz = y * z_ref[...]
    ms2 = jnp.mean(yz * yz, axis=-1, keepdims=True)
    y_ref[...] = ((yz * lax.rsqrt(ms2 + SSM_NORM_EPS)) * gnorm_ref[...]).astype(y_ref.dtype)


def _ssd(zs, xbc, dt_raw, dt_bias, a_log, d_skip, g_norm, d_inner, n_heads):
    s = zs.shape[0]
    g = SSM_GROUPS
    heads = n_heads // g
    width = d_inner // g
    assert width == heads * SSM_HEAD_DIM and heads % 2 == 0 and heads <= LANES and width % LANES == 0
    assert (heads // 2) % min(SSD_PAIRS_PER_DOT, heads // 2) == 0
    assert D_STATE == LANES and s % SSM_CHUNK == 0
    hp = max(SUBLANES, heads)
    nc = s // SSM_CHUNK
    cl = SSM_CHUNK

    dt_g = dt_raw.reshape(s, g, heads).transpose(1, 0, 2)
    dt_row = jnp.pad(dt_g, ((0, 0), (0, 0), (0, LANES - heads)))
    dt_col = jnp.pad(dt_g.transpose(0, 2, 1), ((0, 0), (0, hp - heads), (0, 0)))

    def rowvec(v):
        return jnp.pad(v.astype(F32).reshape(g, 1, heads), ((0, 0), (0, 0), (0, LANES - heads)))

    def colvec(v):
        return jnp.pad(v.astype(F32).reshape(g, heads, 1), ((0, 0), (0, hp - heads), (0, 0)))

    dskip_w = jnp.repeat(d_skip.astype(F32), SSM_HEAD_DIM).reshape(1, d_inner)
    expand2 = np.tile(np.arange(LANES)[:, None] == (np.arange(width)[None, :] // SSM_HEAD_DIM), (2, 1))
    boff = d_inner // LANES
    coff = boff + g * D_STATE // LANES

    est = 2 * (2 * cl * width * 4 + 2 * cl * LANES * 4 + cl * LANES * 4 + hp * cl * 4 + cl * width * 2) \
        + D_STATE * width * 4 + cl * width * 4 + 16 * cl * width * 4
    return pl.pallas_call(
        functools.partial(_ssd_kernel, heads=heads),
        grid=(g, nc),
        in_specs=[
            pl.BlockSpec((cl, width), lambda gi, c: (c, gi)),
            pl.BlockSpec((cl, width), lambda gi, c: (c, gi)),
            pl.BlockSpec((cl, D_STATE), lambda gi, c: (c, boff + gi)),
            pl.BlockSpec((cl, D_STATE), lambda gi, c: (c, coff + gi)),
            pl.BlockSpec((None, cl, LANES), lambda gi, c: (gi, c, 0)),
            pl.BlockSpec((None, hp, cl), lambda gi, c: (gi, 0, c)),
            pl.BlockSpec((None, 1, LANES), lambda gi, c: (gi, 0, 0)),
            pl.BlockSpec((None, hp, 1), lambda gi, c: (gi, 0, 0)),
            pl.BlockSpec((None, 1, LANES), lambda gi, c: (gi, 0, 0)),
            pl.BlockSpec((None, hp, 1), lambda gi, c: (gi, 0, 0)),
            pl.BlockSpec((1, width), lambda gi, c: (0, gi)),
            pl.BlockSpec((1, width), lambda gi, c: (0, gi)),
            pl.BlockSpec((2 * LANES, width), lambda gi, c: (0, 0)),
        ],
        out_specs=pl.BlockSpec((cl, width), lambda gi, c: (c, gi)),
        out_shape=jax.ShapeDtypeStruct((s, d_inner), BF16),
        scratch_shapes=[
            pltpu.VMEM((D_STATE, width), F32),
        ],
        compiler_params=_params(("parallel", "arbitrary"), est),
        name="ssd_scan",
    )(zs, xbc, xbc, xbc, dt_row, dt_col,
      rowvec(dt_bias), colvec(dt_bias), rowvec(a_log), colvec(a_log), dskip_w, g_norm.reshape(1, d_inner),
      jnp.asarray(expand2, BF16))


def _t5_bucket_table():
    q = np.arange(ATTN_BLOCK)[:, None]
    k = np.arange(2 * ATTN_BLOCK)[None, :]
    rel = np.maximum(q - k + ATTN_BLOCK, 0)
    max_exact = N_BUCKETS // 2
    relf = np.maximum(rel, 1).astype(np.float32)
    large = max_exact + (np.log(relf / np.float32(max_exact)) / np.float32(math.log(MAX_DISTANCE / max_exact))
                         * np.float32(N_BUCKETS - max_exact)).astype(np.int32)
    large = np.minimum(large, N_BUCKETS - 1)
    return np.where(rel < max_exact, rel, large).astype(np.int32)


def _attn_kernel(relb_ref, sink_ref, bucket_ref, q_ref, kp_ref, kc_ref, vp_ref, vc_ref, o_ref, bias_ref,
                 *, qpk, n_q_heads):
    gp = pl.program_id(0)
    n = pl.program_id(1)
    blk = ATTN_BLOCK
    npair = qpk // 2
    head0 = gp * 2 * qpk

    @pl.when(n == 0)
    def _():
        bucket = bucket_ref[...]
        qi = lax.broadcasted_iota(jnp.int32, (blk, 2 * blk), 0)
        ci = lax.broadcasted_iota(jnp.int32, (blk, 2 * blk), 1)
        rel = qi - ci + blk
        in_window = (rel >= 0) & (rel < WINDOW)

        def body(it, carry):
            kvh = it // npair
            pair = it - kvh * npair
            row0 = pl.multiple_of(pair * blk, blk)
            for hp in range(2):
                head = head0 + kvh * qpk + pair * 2 + hp
                acc = jnp.zeros((blk, 2 * blk), F32)
                for b in range(N_BUCKETS):
                    acc = jnp.where(bucket == b, relb_ref[b * n_q_heads + head], acc)
                bias_ref[kvh, pl.ds(row0, blk), hp * 2 * blk:(hp + 1) * 2 * blk] = jnp.where(in_window, acc, -jnp.inf)
            return carry

        lax.fori_loop(0, 2 * npair, body, 0)

    kk = jnp.concatenate([kp_ref[...], kc_ref[...]], axis=0)
    vv = jnp.concatenate([vp_ref[...], vc_ref[...]], axis=0)
    lane = lax.broadcasted_iota(jnp.int32, (2 * blk, LANES), 1)
    low = lane < ATTN_HEAD_DIM
    scale = ATTN_HEAD_DIM ** -0.5

    def pair_operand(t, kvh, mult):
        sel = jnp.where(low if kvh == 0 else jnp.logical_not(low), t, 0.0)
        dup = sel + pltpu.roll(sel, ATTN_HEAD_DIM, axis=1)
        if mult is not None:
            dup = dup * mult
        return jnp.concatenate([jnp.where(low, dup, 0.0), jnp.where(low, 0.0, dup)], axis=0).astype(BF16)

    row_bd = lax.broadcasted_iota(jnp.int32, (4 * blk, LANES), 0)
    lane_bd = lax.broadcasted_iota(jnp.int32, (4 * blk, LANES), 1)
    ones_bd = ((row_bd < 2 * blk) == (lane_bd < ATTN_HEAD_DIM)).astype(F32).astype(BF16)
    low_out = lax.broadcasted_iota(jnp.int32, (npair * blk, LANES), 1) < ATTN_HEAD_DIM

    def run(first_block):
        if first_block:
            ci = lax.broadcasted_iota(jnp.int32, (npair * blk, 2 * blk), 1)
            before_start = ci < blk
        for kvh in range(2):
            kbd = pair_operand(kk, kvh, scale)
            vbd = pair_operand(vv, kvh, None)
            base = kvh * npair
            qs = jnp.concatenate([q_ref[:, (base + i) * LANES:(base + i + 1) * LANES] for i in range(npair)], axis=0)
            s = lax.dot_general(qs, kbd, (((1,), (1,)), ((), ())), preferred_element_type=F32)
            es = []
            sink_terms = []
            for hp in range(2):
                sinks = [sink_ref[head0 + kvh * qpk + i * 2 + hp] for i in range(npair)]
                sink_col = jnp.concatenate([jnp.full((blk, 1), v, F32) for v in sinks], axis=0)
                sink_all = jnp.concatenate([jnp.full((blk, LANES), v, F32) for v in sinks], axis=0)
                sh = s[:, hp * 2 * blk:(hp + 1) * 2 * blk] + bias_ref[kvh, :, hp * 2 * blk:(hp + 1) * 2 * blk]
                if first_block:
                    sh = jnp.where(before_start, -jnp.inf, sh)
                m = jnp.maximum(jnp.max(sh, axis=-1, keepdims=True), sink_col)
                es.append(jnp.exp(sh - m).astype(BF16))
                sink_terms.append(jnp.exp(sink_all - jnp.broadcast_to(m, sink_all.shape)))
            e = jnp.concatenate(es, axis=1)
            ov = jnp.dot(e, jnp.concatenate([vbd, ones_bd], axis=1), preferred_element_type=F32)
            denom = ov[:, LANES:2 * LANES] + jnp.where(low_out, sink_terms[0], sink_terms[1])
            o = (ov[:, 0:LANES] * (1.0 / denom)).astype(o_ref.dtype)
            for i in range(npair):
                o_ref[:, (base + i) * LANES:(base + i + 1) * LANES] = o[i * blk:(i + 1) * blk, :]

    @pl.when(n == 0)
    def _():
        run(True)

    @pl.when(n != 0)
    def _():
        run(False)


def _attention(q, kv, sinks, rel_bias):
    s, qd = q.shape
    n_q_heads = qd // ATTN_HEAD_DIM
    qpk = n_q_heads // N_KV_HEADS
    assert qpk % 2 == 0 and N_KV_HEADS % 2 == 0 and 2 * ATTN_HEAD_DIM == LANES and s % ATTN_BLOCK == 0
    blk = ATTN_BLOCK
    nb = s // blk
    ngp = N_KV_HEADS // 2
    qw = 2 * qpk * ATTN_HEAD_DIM
    voff = N_KV_HEADS * ATTN_HEAD_DIM // LANES
    bucket = jnp.asarray(_t5_bucket_table())
    est = 2 * (2 * blk * qw * 2 + 4 * blk * LANES * 4 + blk * 2 * blk * 4) + 2 * qpk * blk * 2 * blk * 4 \
        + 24 * blk * 4 * blk * 4
    smem = pl.BlockSpec(memory_space=pltpu.SMEM)
    return pl.pallas_call(
        functools.partial(_attn_kernel, qpk=qpk, n_q_heads=n_q_heads),
        grid=(ngp, nb),
        in_specs=[
            smem, smem,
            pl.BlockSpec((blk, 2 * blk), lambda g, n: (0, 0)),
            pl.BlockSpec((blk, qw), lambda g, n: (n, g)),
            pl.BlockSpec((blk, LANES), lambda g, n: (jnp.maximum(n - 1, 0), g)),
            pl.BlockSpec((blk, LANES), lambda g, n: (n, g)),
            pl.BlockSpec((blk, LANES), lambda g, n: (jnp.maximum(n - 1, 0), voff + g)),
            pl.BlockSpec((blk, LANES), lambda g, n: (n, voff + g)),
        ],
        out_specs=pl.BlockSpec((blk, qw), lambda g, n: (n, g)),
        out_shape=jax.ShapeDtypeStruct((s, qd), BF16),
        scratch_shapes=[pltpu.VMEM((2, (qpk // 2) * blk, 4 * blk), F32)],
        compiler_params=_params(("parallel", "arbitrary"), est),
        name="swa_attention",
    )(rel_bias.astype(F32).reshape(-1), sinks.astype(F32).reshape(-1), bucket, q, kv, kv, kv, kv)


def kernel(x, norm_mix_pre, norm_mix_post, norm_ffn_pre, norm_ffn_post, ssm_w_in, ssm_conv_w, ssm_conv_b, ssm_dt_bias, ssm_a_log, ssm_d, ssm_norm, ssm_w_out, kv_norm, w_kv, b_kv, attn_w_q, attn_b_q, attn_sinks, attn_w_o, attn_b_o, rel_bias, ffn_w_up, ffn_conv_w, ffn_conv_b, ffn_w_down):
    bsz, s, d = x.shape
    assert bsz == 1 and norm_mix_pre.shape[0] == 2
    d_inner = ssm_norm.shape[-1]
    n_heads = ssm_dt_bias.shape[-1]
    zxbc_dim = ssm_w_in.shape[-1] - n_heads

    def ffn(h_in_u, layer):
        d_ff = ffn_w_up.shape[-1] // 2
        tn = _tile(d_ff, 256, LANES)
        hff = _conv_matmul(h_in_u, ffn_w_up, layer, d_ff, [0, d_ff // tn], ffn_conv_w[layer],
                           ffn_conv_b[layer].reshape(1, -1), tn, True, BF16, "ffn_up")
        return _matmul(hff, ffn_w_down, None, F32, 1024, 256, "ffn_down", a_buffers=1, layer=layer)

    h0 = x.reshape(s, d)

    u = _prenorm(h0, norm_mix_pre[0])
    conv_dim = zxbc_dim - d_inner
    tn_in = _tile(math.gcd(d_inner, conv_dim), 512, LANES)
    zs = _conv_matmul(u, ssm_w_in, 0, d_inner, [0], jnp.ones((1, d_inner), F32), jnp.zeros((1, d_inner), F32),
                      tn_in, False, F32, "in_proj_z")
    xbc = _conv_matmul(u, ssm_w_in, 0, conv_dim, [d_inner // tn_in], ssm_conv_w[0].astype(F32),
                       ssm_conv_b[0].astype(F32).reshape(1, -1), tn_in, False, F32, "in_proj_xbc",
                       conv_col_blocks=[0])
    dt_raw = _matmul(u, ssm_w_in, None, F32, 1024, 128, "dt_proj", col0=zxbc_dim, n=n_heads, layer=0)
    y = _ssd(zs, xbc, dt_raw, ssm_dt_bias[0], ssm_a_log[0], ssm_d[0], ssm_norm[0], d_inner, n_heads)
    mix = _matmul(y, ssm_w_out, None, F32, 1024, 256, "out_proj", a_buffers=1, layer=0)
    h1, u = _resnorm(h0, mix, norm_mix_post[0], [norm_ffn_pre[0]])
    f = ffn(u, 0)

    h2, ukv, uq = _resnorm(h1, f, norm_ffn_post[0], [kv_norm, norm_mix_pre[1]])
    kv = _matmul(ukv, w_kv, b_kv, F32, 1024, 512, "kv_proj")
    q = _matmul(uq, attn_w_q, attn_b_q[0], BF16, 1024, 512, "q_proj", layer=0)
    o = _attention(q, kv, attn_sinks[0], rel_bias)
    mix = _matmul(o, attn_w_o, attn_b_o[0], F32, 1024, 512, "o_proj", layer=0)
    h3, u = _resnorm(h2, mix, norm_mix_post[1], [norm_ffn_pre[1]])
    f = ffn(u, 1)
    (h4,) = _resnorm(h3, f, norm_ffn_post[1], [])
    return h4.reshape(bsz, s, d)
```

```python
import functools
import math

import numpy as np
import jax
import jax.numpy as jnp
from jax import lax
from jax.experimental import pallas as pl
from jax.experimental.pallas import tpu as pltpu

EPS = 1e-6
SSM_NORM_EPS = 1e-5
SSM_HEAD_DIM = 64
SSM_GROUPS = 8
D_STATE = 128
SSM_CONV = 4
SSM_CHUNK = 128
SSD_PAIRS_PER_DOT = 2
ATTN_HEAD_DIM = 64
N_KV_HEADS = 8
WINDOW = 128
ATTN_BLOCK = 128
N_BUCKETS = 32
MAX_DISTANCE = 128
FFN_CONV = 3

LANES = 128
SUBLANES = 8
BF16_TILE_ROWS = 16
MXU_K_CHUNK = 256
VMEM_CAP_BYTES = 60 * 1024 * 1024

F32 = jnp.float32
BF16 = jnp.bfloat16


def _vmem_limit(est_bytes):
    return int(min(VMEM_CAP_BYTES, max(32 * 1024 * 1024, est_bytes * 5 // 4 + (4 << 20))))


def _params(semantics, est_bytes, flags=None):
    return pltpu.CompilerParams(dimension_semantics=semantics, vmem_limit_bytes=_vmem_limit(est_bytes), flags=flags)


def _tile(dim, pref, align):
    if dim <= pref:
        return dim
    t = (pref // align) * align
    while t >= align:
        if dim % t == 0:
            return t
        t -= align
    raise ValueError(f"no tile for {dim} (pref {pref}, align {align})")


def _sigmoid(x):
    return 0.5 * jnp.tanh(0.5 * x) + 0.5


def _softplus(x):
    return jnp.maximum(x, 0.0) + jnp.log1p(jnp.exp(-jnp.abs(x)))


def _split3(v):
    hi = v.astype(BF16)
    r1 = v - hi.astype(F32)
    mid = r1.astype(BF16)
    lo = (r1 - mid.astype(F32)).astype(BF16)
    return [hi, mid, lo]


def _rms(x, g, eps):
    ms = jnp.mean(x * x, axis=-1, keepdims=True)
    return (x * lax.rsqrt(ms + eps)) * g


def _prenorm_kernel(x_ref, g_ref, u_ref):
    u_ref[...] = _rms(x_ref[...], g_ref[...], EPS).astype(u_ref.dtype)


def _prenorm(x, g):
    s, d = x.shape
    tr = _tile(s, 256, SUBLANES)
    est = 2 * tr * d * (4 + 2)
    return pl.pallas_call(
        _prenorm_kernel,
        grid=(s // tr,),
        in_specs=[pl.BlockSpec((tr, d), lambda i: (i, 0)), pl.BlockSpec((1, d), lambda i: (0, 0))],
        out_specs=pl.BlockSpec((tr, d), lambda i: (i, 0)),
        out_shape=jax.ShapeDtypeStruct((s, d), BF16),
        compiler_params=_params(("parallel",), est),
        name="prenorm",
    )(x, g.reshape(1, d))


def _resnorm_kernel(r_ref, m_ref, gpost_ref, *rest, n_u):
    g_refs = rest[:n_u]
    h_ref = rest[n_u]
    u_refs = rest[n_u + 1:]
    h = r_ref[...] + _rms(m_ref[...], gpost_ref[...], EPS)
    h_ref[...] = h
    if n_u:
        ms = jnp.mean(h * h, axis=-1, keepdims=True)
        hn = h * lax.rsqrt(ms + EPS)
        for g_ref, u_ref in zip(g_refs, u_refs):
            u_ref[...] = (hn * g_ref[...]).astype(u_ref.dtype)


def _resnorm(r, m, gpost, gains):
    s, d = r.shape
    n_u = len(gains)
    tr = _tile(s, 256, SUBLANES)
    est = 2 * tr * d * (4 * 3 + 2 * n_u)
    row = pl.BlockSpec((tr, d), lambda i: (i, 0))
    vec = pl.BlockSpec((1, d), lambda i: (0, 0))
    outs = pl.pallas_call(
        functools.partial(_resnorm_kernel, n_u=n_u),
        grid=(s // tr,),
        in_specs=[row, row, vec] + [vec] * n_u,
        out_specs=[row] + [row] * n_u,
        out_shape=[jax.ShapeDtypeStruct((s, d), F32)] + [jax.ShapeDtypeStruct((s, d), BF16)] * n_u,
        compiler_params=_params(("parallel",), est),
        name="resnorm",
    )(r, m, gpost.reshape(1, d), *[g.reshape(1, d) for g in gains])
    return outs


def _matmul_kernel(a_ref, w_ref, *rest, has_bias):
    o_ref = rest[-1]
    acc = jnp.dot(a_ref[...], w_ref[...].astype(BF16), preferred_element_type=F32)
    if has_bias:
        acc = acc + rest[0][...]
    o_ref[...] = acc.astype(o_ref.dtype)


def _matmul(a, w, bias, out_dtype, tm_pref, tn_pref, name, a_buffers=2, col0=0, n=None, layer=None):
    m, k = a.shape
    assert (w.ndim == 3) == (layer is not None)
    n = w.shape[-1] if n is None else n
    tm = _tile(m, tm_pref, SUBLANES)
    tn = _tile(math.gcd(n, col0) if col0 else n, tn_pref, LANES)
    assert n % tn == 0 and col0 % tn == 0
    joff = col0 // tn
    osz = jnp.dtype(out_dtype).itemsize
    wsz = jnp.dtype(w.dtype).itemsize
    est = a_buffers * tm * k * 2 + 2 * (k * tn * wsz + tm * tn * osz) + tm * tn * 4 + (k * tn * 2 if wsz != 2 else 0)
    a_mode = {} if a_buffers == 2 else {"pipeline_mode": pl.Buffered(a_buffers)}
    if layer is None:
        w_spec = pl.BlockSpec((k, tn), lambda i, j: (0, j + joff))
    else:
        w_spec = pl.BlockSpec((None, k, tn), lambda i, j: (layer, 0, j + joff))
    in_specs = [pl.BlockSpec((tm, k), lambda i, j: (i, 0), **a_mode), w_spec]
    args = [a, w]
    if bias is not None:
        in_specs.append(pl.BlockSpec((1, tn), lambda i, j: (0, j)))
        args.append(bias.reshape(1, n).astype(F32))
    return pl.pallas_call(
        functools.partial(_matmul_kernel, has_bias=bias is not None),
        grid=(m // tm, n // tn),
        in_specs=in_specs,
        out_specs=pl.BlockSpec((tm, tn), lambda i, j: (i, j)),
        out_shape=jax.ShapeDtypeStruct((m, n), out_dtype),
        compiler_params=_params(("parallel", "parallel"), est),
        name=name,
    )(*args)


def _zero_after(x):
    w = pltpu.bitcast(x, jnp.uint32)
    zero = (w >> 16) >> 16
    return zero if zero.shape[0] == BF16_TILE_ROWS else jnp.concatenate([zero, zero], axis=0)


def _conv_mm_kernel(u_ref, *refs, nj, n_w, taps, gate):
    w_refs = refs[0:n_w]
    cw_refs = refs[n_w:2 * n_w]
    cb_refs = refs[2 * n_w:3 * n_w]
    o_ref, halo, ext_a, ext_b, wcat = refs[3 * n_w:]
    t = pl.program_id(0)
    tm = u_ref.shape[0]
    d = u_ref.shape[1]
    tn = w_refs[0].shape[1]
    nch = d // MXU_K_CHUNK
    rows = tm // nch
    jp = lax.rem(jnp.maximum(t - 1, 0), nj)

    @pl.when(t == 0)
    def _():
        halo[...] = jnp.zeros(halo.shape, F32)
        ext_b[...] = jnp.zeros(ext_b.shape, F32)

    def step(ext_mm, ext_ep):
        cw = jnp.concatenate([r[...] for r in cw_refs], axis=1)
        cb = jnp.concatenate([r[...] for r in cb_refs], axis=1)
        prev = halo[jp]
        for c in range(nch):
            cur = ext_ep[c * rows:(c + 1) * rows, :]
            both = jnp.concatenate([prev, cur], axis=0)
            acc = cb
            for k in range(taps - 1):
                r0 = SUBLANES - (taps - 1) + k
                acc = acc + both[r0:r0 + rows, :] * cw[k:k + 1, :]
            acc = acc + cur * cw[taps - 1:taps, :]
            if gate:
                out = (acc[:, 0:tn] * _sigmoid(acc[:, 0:tn])) * acc[:, tn:2 * tn]
            else:
                out = acc * _sigmoid(acc)
            out = out.astype(o_ref.dtype)
            o_ref[c * rows:(c + 1) * rows, :] = out
            prev = cur[rows - SUBLANES:rows, :]

            zero = _zero_after(out[0:BF16_TILE_ROWS, 0:tn])
            k0 = c * MXU_K_CHUNK
            for idx, w_ref in enumerate(w_refs):
                top = pltpu.bitcast(pltpu.bitcast(w_ref[k0:k0 + BF16_TILE_ROWS, :], jnp.uint32) | zero, F32)
                wcat[k0:k0 + BF16_TILE_ROWS, idx * tn:(idx + 1) * tn] = top.astype(BF16)
                wcat[k0 + BF16_TILE_ROWS:k0 + MXU_K_CHUNK, idx * tn:(idx + 1) * tn] = (
                    w_ref[k0 + BF16_TILE_ROWS:k0 + MXU_K_CHUNK, :].astype(BF16))
        halo[jp] = prev
        ext_mm[...] = jnp.dot(u_ref[...], wcat[...], preferred_element_type=F32)

    @pl.when(lax.rem(t, 2) == 0)
    def _():
        step(ext_a, ext_b)

    @pl.when(lax.rem(t, 2) == 1)
    def _():
        step(ext_b, ext_a)


def _conv_matmul(u, w, layer, n_cols, col_blocks, conv_w, conv_b, tn, gate, out_dtype, name, conv_col_blocks=None):
    s, d = u.shape
    n_w = len(col_blocks)
    assert n_w == (2 if gate else 1) and d % MXU_K_CHUNK == 0
    taps = conv_w.shape[0]
    assert n_cols % tn == 0
    nj = n_cols // tn
    nch = d // MXU_K_CHUNK
    tm = _tile(s, 1024, nch * BF16_TILE_ROWS)
    wsz = jnp.dtype(w.dtype).itemsize
    osz = jnp.dtype(out_dtype).itemsize
    wt = n_w * tn
    wo = tn if gate else wt
    est = 2 * (tm * d * 2 + d * wt * wsz + tm * wo * osz) + d * wt * 2 + nj * SUBLANES * wt * 4 + 4 * tm * wt * 4
    n_tiles = (s // tm) * nj

    def mm_tile(t):
        return jnp.minimum(t, n_tiles - 1)

    def ep_tile(t):
        return jnp.maximum(t - 1, 0)

    w_specs = [pl.BlockSpec((None, d, tn), lambda t, o=o: (layer, 0, mm_tile(t) % nj + o)) for o in col_blocks]
    conv_col_blocks = col_blocks if conv_col_blocks is None else conv_col_blocks
    cw_specs = [pl.BlockSpec((taps, tn), lambda t, o=o: (0, ep_tile(t) % nj + o)) for o in conv_col_blocks]
    cb_specs = [pl.BlockSpec((1, tn), lambda t, o=o: (0, ep_tile(t) % nj + o)) for o in conv_col_blocks]
    return pl.pallas_call(
        functools.partial(_conv_mm_kernel, nj=nj, n_w=n_w, taps=taps, gate=gate),
        grid=(n_tiles + 1,),
        in_specs=[pl.BlockSpec((tm, d), lambda t: (mm_tile(t) // nj, 0))] + w_specs + cw_specs + cb_specs,
        out_specs=pl.BlockSpec((tm, wo), lambda t: (ep_tile(t) // nj, ep_tile(t) % nj)),
        out_shape=jax.ShapeDtypeStruct((s, nj * wo), out_dtype),
        scratch_shapes=[
            pltpu.VMEM((nj, SUBLANES, wt), F32),
            pltpu.VMEM((tm, wt), F32),
            pltpu.VMEM((tm, wt), F32),
            pltpu.VMEM((d, wt), BF16),
        ],
        compiler_params=_params(("arbitrary",), est),
        name=name,
    )(u, *([w] * n_w), *([conv_w] * n_w), *([conv_b] * n_w))


def _ssd_kernel(z_ref, x_ref, b_ref, c_ref, dt_ref, dtt_ref,
                dtb_row_ref, dtb_col_ref, alog_row_ref, alog_col_ref, dskip_ref, gnorm_ref, expand_ref,
                y_ref, state_ref, *, heads, groups):
    chunk = pl.program_id(0)

    @pl.when(chunk == 0)
    def _():
        state_ref[...] = jnp.zeros(state_ref.shape, F32)

    width = heads * SSM_HEAD_DIM
    for gi in range(groups):
        cols = slice(gi * width, (gi + 1) * width)
        ncols = slice(gi * D_STATE, (gi + 1) * D_STATE)
        y_ref[:, cols] = _ssd_group_chunk(
            z_ref[:, cols], x_ref[:, cols], b_ref[:, ncols], c_ref[:, ncols], dt_ref[gi], dtt_ref[gi],
            dtb_row_ref[gi], dtb_col_ref[gi], alog_row_ref[gi], alog_col_ref[gi], dskip_ref[:, cols],
            gnorm_ref[:, cols], expand_ref, state_ref.at[gi], heads).astype(y_ref.dtype)


def _ssd_group_chunk(zs, xs, bc, cc, dt, dt_t, dtb_row, dtb_col, alog_row, alog_col, dskip, gnorm, expand_ref,
                     state_ref, heads):
    cl = SSM_CHUNK
    width = heads * SSM_HEAD_DIM

    li = lax.broadcasted_iota(jnp.int32, (cl, cl), 0)
    si = lax.broadcasted_iota(jnp.int32, (cl, cl), 1)
    tril = li >= si

    dtv = _softplus(dt + dtb_row)
    a = dtv * (-jnp.exp(alog_row))
    ones_lo = tril.astype(F32).astype(BF16)
    ones_up = (li <= si).astype(F32).astype(BF16)
    acum = jnp.dot(jnp.concatenate([ones_lo] * 3, axis=1), jnp.concatenate(_split3(a), axis=0),
                   preferred_element_type=F32)
    dtv_t = _softplus(dt_t + dtb_col)
    a_t = dtv_t * (-jnp.exp(alog_col))
    acum_t = jnp.dot(jnp.concatenate(_split3(a_t), axis=1), jnp.concatenate([ones_up] * 3, axis=0),
                     preferred_element_type=F32)
    alast = acum[cl - 1:cl, :]
    exp_acum = jnp.exp(acum)
    decay_end = jnp.exp(alast - acum)

    stacked = jnp.concatenate([dtv, exp_acum, decay_end], axis=0)
    wide = jnp.dot(jnp.concatenate(_split3(stacked)[0:2], axis=1), expand_ref[...], preferred_element_type=F32)
    dt_w = wide[0:cl, :]
    exp_acum_w = wide[cl:2 * cl, :]
    decay_end_w = wide[2 * cl:3 * cl, :]

    xdt = xs * dt_w
    state = state_ref[...]
    bc_t = jnp.transpose(bc).astype(BF16)
    c_prod = jnp.dot(cc.astype(BF16), jnp.concatenate([state.astype(BF16), bc_t], axis=1),
                     preferred_element_type=F32)
    y_off = c_prod[:, 0:width] * exp_acum_w
    cb = jnp.where(tril, c_prod[:, width:width + cl], 0.0)

    lane = lax.broadcasted_iota(jnp.int32, (cl, LANES), 1)
    low = lane < SSM_HEAD_DIM
    npairs = heads // 2
    ppd = min(SSD_PAIRS_PER_DOT, npairs)
    zero_blk = jnp.zeros((cl, LANES), BF16)
    y_diag = []
    for q in range(npairs // ppd):
        ms = []
        rhs_rows = []
        for pi in range(ppd):
            p = q * ppd + pi
            for hp in range(2):
                r = 2 * p + hp
                seg = acum[:, r:r + 1] - acum_t[r:r + 1, :]
                decay = jnp.exp(jnp.minimum(seg, 0.0))
                ms.append((cb * decay).astype(BF16))
            xp = xdt[:, p * LANES:(p + 1) * LANES]
            for blk in (jnp.where(low, xp, 0.0).astype(BF16), jnp.where(low, 0.0, xp).astype(BF16)):
                rhs_rows.append(jnp.concatenate([zero_blk] * pi + [blk] + [zero_blk] * (ppd - 1 - pi), axis=1))
        y_diag.append(jnp.dot(jnp.concatenate(ms, axis=1), jnp.concatenate(rhs_rows, axis=0),
                              preferred_element_type=F32))
    y = jnp.concatenate(y_diag, axis=1) + y_off + dskip * xs

    xw = (xdt * decay_end_w).astype(BF16)
    upd = jnp.dot(bc_t, xw, preferred_element_type=F32)
    state_ref[...] = state * exp_acum_w[cl - 1:cl, :] + upd

    yz = y * zs
    ms2 = jnp.mean(yz * yz, axis=-1, keepdims=True)
    return (yz * lax.rsqrt(ms2 + SSM_NORM_EPS)) * gnorm


def _ssd(zs, xbc, dt_raw, dt_bias, a_log, d_skip, g_norm, d_inner, n_heads):
    s = zs.shape[0]
    g = SSM_GROUPS
    heads = n_heads // g
    width = d_inner // g
    assert width == heads * SSM_HEAD_DIM and heads % 2 == 0 and heads <= LANES and width % LANES == 0
    assert (heads // 2) % min(SSD_PAIRS_PER_DOT, heads // 2) == 0
    assert D_STATE == LANES and s % SSM_CHUNK == 0
    hp = max(SUBLANES, heads)
    nc = s // SSM_CHUNK
    cl = SSM_CHUNK

    dt_g = dt_raw.reshape(s, g, heads).transpose(1, 0, 2)
    dt_row = jnp.pad(dt_g, ((0, 0), (0, 0), (0, LANES - heads)))
    dt_col = jnp.pad(dt_g.transpose(0, 2, 1), ((0, 0), (0, hp - heads), (0, 0)))

    def rowvec(v):
        return jnp.pad(v.astype(F32).reshape(g, 1, heads), ((0, 0), (0, 0), (0, LANES - heads)))

    def colvec(v):
        return jnp.pad(v.astype(F32).reshape(g, heads, 1), ((0, 0), (0, hp - heads), (0, 0)))

    dskip_w = jnp.repeat(d_skip.astype(F32), SSM_HEAD_DIM).reshape(1, d_inner)
    expand2 = np.tile(np.arange(LANES)[:, None] == (np.arange(width)[None, :] // SSM_HEAD_DIM), (2, 1))

    gn = g * D_STATE
    assert d_inner % gn == 0
    est = 2 * (2 * cl * d_inner * 4 + 2 * cl * gn * 4 + g * (cl * LANES + hp * cl) * 4 + cl * d_inner * 2) \
        + g * D_STATE * width * 4 + 24 * cl * width * 4
    whole = lambda c: (0, 0, 0)
    return pl.pallas_call(
        functools.partial(_ssd_kernel, heads=heads, groups=g),
        grid=(nc,),
        in_specs=[
            pl.BlockSpec((cl, d_inner), lambda c: (c, 0)),
            pl.BlockSpec((cl, d_inner), lambda c: (c, 0)),
            pl.BlockSpec((cl, gn), lambda c: (c, d_inner // gn)),
            pl.BlockSpec((cl, gn), lambda c: (c, d_inner // gn + 1)),
            pl.BlockSpec((g, cl, LANES), lambda c: (0, c, 0)),
            pl.BlockSpec((g, hp, cl), lambda c: (0, 0, c)),
            pl.BlockSpec((g, 1, LANES), whole),
            pl.BlockSpec((g, hp, 1), whole),
            pl.BlockSpec((g, 1, LANES), whole),
            pl.BlockSpec((g, hp, 1), whole),
            pl.BlockSpec((1, d_inner), lambda c: (0, 0)),
            pl.BlockSpec((1, d_inner), lambda c: (0, 0)),
            pl.BlockSpec((2 * LANES, width), lambda c: (0, 0)),
        ],
        out_specs=pl.BlockSpec((cl, d_inner), lambda c: (c, 0)),
        out_shape=jax.ShapeDtypeStruct((s, d_inner), BF16),
        scratch_shapes=[
            pltpu.VMEM((g, D_STATE, width), F32),
        ],
        compiler_params=_params(("arbitrary",), est),
        name="ssd_scan",
    )(zs, xbc, xbc, xbc, dt_row, dt_col,
      rowvec(dt_bias), colvec(dt_bias), rowvec(a_log), colvec(a_log), dskip_w, g_norm.reshape(1, d_inner),
      jnp.asarray(expand2, BF16))


def _t5_bucket_table():
    q = np.arange(ATTN_BLOCK)[:, None]
    k = np.arange(2 * ATTN_BLOCK)[None, :]
    rel = np.maximum(q - k + ATTN_BLOCK, 0)
    max_exact = N_BUCKETS // 2
    relf = np.maximum(rel, 1).astype(np.float32)
    large = max_exact + (np.log(relf / np.float32(max_exact)) / np.float32(math.log(MAX_DISTANCE / max_exact))
                         * np.float32(N_BUCKETS - max_exact)).astype(np.int32)
    large = np.minimum(large, N_BUCKETS - 1)
    return np.where(rel < max_exact, rel, large).astype(np.int32)


def _attn_kernel(relb_ref, sink_ref, bucket_ref, q_ref, kp_ref, kc_ref, vp_ref, vc_ref, o_ref, bias_ref,
                 *, qpk, n_q_heads):
    gp = pl.program_id(0)
    n = pl.program_id(1)
    blk = ATTN_BLOCK
    npair = qpk // 2
    head0 = gp * 2 * qpk

    @pl.when(n == 0)
    def _():
        bucket = bucket_ref[...]
        qi = lax.broadcasted_iota(jnp.int32, (blk, 2 * blk), 0)
        ci = lax.broadcasted_iota(jnp.int32, (blk, 2 * blk), 1)
        rel = qi - ci + blk
        in_window = (rel >= 0) & (rel < WINDOW)

        def body(it, carry):
            kvh = it // npair
            pair = it - kvh * npair
            row0 = pl.multiple_of(pair * blk, blk)
            for hp in range(2):
                head = head0 + kvh * qpk + pair * 2 + hp
                acc = jnp.zeros((blk, 2 * blk), F32)
                for b in range(N_BUCKETS):
                    acc = jnp.where(bucket == b, relb_ref[b * n_q_heads + head], acc)
                bias_ref[kvh, pl.ds(row0, blk), hp * 2 * blk:(hp + 1) * 2 * blk] = jnp.where(in_window, acc, -jnp.inf)
            return carry

        lax.fori_loop(0, 2 * npair, body, 0)

    kk = jnp.concatenate([kp_ref[...], kc_ref[...]], axis=0)
    vv = jnp.concatenate([vp_ref[...], vc_ref[...]], axis=0)
    lane = lax.broadcasted_iota(jnp.int32, (2 * blk, LANES), 1)
    low = lane < ATTN_HEAD_DIM
    scale = ATTN_HEAD_DIM ** -0.5

    def pair_operand(t, kvh, mult):
        sel = jnp.where(low if kvh == 0 else jnp.logical_not(low), t, 0.0)
        dup = sel + pltpu.roll(sel, ATTN_HEAD_DIM, axis=1)
        if mult is not None:
            dup = dup * mult
        return jnp.concatenate([jnp.where(low, dup, 0.0), jnp.where(low, 0.0, dup)], axis=0).astype(BF16)

    row_bd = lax.broadcasted_iota(jnp.int32, (4 * blk, LANES), 0)
    lane_bd = lax.broadcasted_iota(jnp.int32, (4 * blk, LANES), 1)
    ones_bd = ((row_bd < 2 * blk) == (lane_bd < ATTN_HEAD_DIM)).astype(F32).astype(BF16)
    low_out = lax.broadcasted_iota(jnp.int32, (npair * blk, LANES), 1) < ATTN_HEAD_DIM

    def run(first_block):
        if first_block:
            ci = lax.broadcasted_iota(jnp.int32, (npair * blk, 2 * blk), 1)
            before_start = ci < blk
        for kvh in range(2):
            kbd = pair_operand(kk, kvh, scale)
            vbd = pair_operand(vv, kvh, None)
            base = kvh * npair
            qs = jnp.concatenate([q_ref[:, (base + i) * LANES:(base + i + 1) * LANES] for i in range(npair)], axis=0)
            s = lax.dot_general(qs, kbd, (((1,), (1,)), ((), ())), preferred_element_type=F32)
            es = []
            sink_terms = []
            for hp in range(2):
                sinks = [sink_ref[head0 + kvh * qpk + i * 2 + hp] for i in range(npair)]
                sink_col = jnp.concatenate([jnp.full((blk, 1), v, F32) for v in sinks], axis=0)
                sink_all = jnp.concatenate([jnp.full((blk, LANES), v, F32) for v in sinks], axis=0)
                sh = s[:, hp * 2 * blk:(hp + 1) * 2 * blk] + bias_ref[kvh, :, hp * 2 * blk:(hp + 1) * 2 * blk]
                if first_block:
                    sh = jnp.where(before_start, -jnp.inf, sh)
                m = jnp.maximum(jnp.max(sh, axis=-1, keepdims=True), sink_col)
                es.append(jnp.exp(sh - m).astype(BF16))
                sink_terms.append(jnp.exp(sink_all - jnp.broadcast_to(m, sink_all.shape)))
            e = jnp.concatenate(es, axis=1)
            ov = jnp.dot(e, jnp.concatenate([vbd, ones_bd], axis=1), preferred_element_type=F32)
            denom = ov[:, LANES:2 * LANES] + jnp.where(low_out, sink_terms[0], sink_terms[1])
            o = (ov[:, 0:LANES] * (1.0 / denom)).astype(o_ref.dtype)
            for i in range(npair):
                o_ref[:, (base + i) * LANES:(base + i + 1) * LANES] = o[i * blk:(i + 1) * blk, :]

    @pl.when(n == 0)
    def _():
        run(True)

    @pl.when(n != 0)
    def _():
        run(False)


def _attention(q, kv, sinks, rel_bias):
    s, qd = q.shape
    n_q_heads = qd // ATTN_HEAD_DIM
    qpk = n_q_heads // N_KV_HEADS
    assert qpk % 2 == 0 and N_KV_HEADS % 2 == 0 and 2 * ATTN_HEAD_DIM == LANES and s % ATTN_BLOCK == 0
    blk = ATTN_BLOCK
    nb = s // blk
    ngp = N_KV_HEADS // 2
    qw = 2 * qpk * ATTN_HEAD_DIM
    voff = N_KV_HEADS * ATTN_HEAD_DIM // LANES
    bucket = jnp.asarray(_t5_bucket_table())
    est = 2 * (2 * blk * qw * 2 + 4 * blk * LANES * 4 + blk * 2 * blk * 4) + 2 * qpk * blk * 2 * blk * 4 \
        + 24 * blk * 4 * blk * 4
    smem = pl.BlockSpec(memory_space=pltpu.SMEM)
    return pl.pallas_call(
        functools.partial(_attn_kernel, qpk=qpk, n_q_heads=n_q_heads),
        grid=(ngp, nb),
        in_specs=[
            smem, smem,
            pl.BlockSpec((blk, 2 * blk), lambda g, n: (0, 0)),
            pl.BlockSpec((blk, qw), lambda g, n: (n, g)),
            pl.BlockSpec((blk, LANES), lambda g, n: (jnp.maximum(n - 1, 0), g)),
            pl.BlockSpec((blk, LANES), lambda g, n: (n, g)),
            pl.BlockSpec((blk, LANES), lambda g, n: (jnp.maximum(n - 1, 0), voff + g)),
            pl.BlockSpec((blk, LANES), lambda g, n: (n, voff + g)),
        ],
        out_specs=pl.BlockSpec((blk, qw), lambda g, n: (n, g)),
        out_shape=jax.ShapeDtypeStruct((s, qd), BF16),
        scratch_shapes=[pltpu.VMEM((2, (qpk // 2) * blk, 4 * blk), F32)],
        compiler_params=_params(("parallel", "arbitrary"), est),
        name="swa_attention",
    )(rel_bias.astype(F32).reshape(-1), sinks.astype(F32).reshape(-1), bucket, q, kv, kv, kv, kv)


def kernel(x, norm_mix_pre, norm_mix_post, norm_ffn_pre, norm_ffn_post, ssm_w_in, ssm_conv_w, ssm_conv_b, ssm_dt_bias, ssm_a_log, ssm_d, ssm_norm, ssm_w_out, kv_norm, w_kv, b_kv, attn_w_q, attn_b_q, attn_sinks, attn_w_o, attn_b_o, rel_bias, ffn_w_up, ffn_conv_w, ffn_conv_b, ffn_w_down):
    bsz, s, d = x.shape
    assert bsz == 1 and norm_mix_pre.shape[0] == 2
    d_inner = ssm_norm.shape[-1]
    n_heads = ssm_dt_bias.shape[-1]
    zxbc_dim = ssm_w_in.shape[-1] - n_heads

    def ffn(h_in_u, layer):
        d_ff = ffn_w_up.shape[-1] // 2
        tn = _tile(d_ff, 256, LANES)
        hff = _conv_matmul(h_in_u, ffn_w_up, layer, d_ff, [0, d_ff // tn], ffn_conv_w[layer],
                           ffn_conv_b[layer].reshape(1, -1), tn, True, BF16, "ffn_up")
        return _matmul(hff, ffn_w_down, None, F32, 1024, 256, "ffn_down", a_buffers=1, layer=layer)

    h0 = x.reshape(s, d)

    u = _prenorm(h0, norm_mix_pre[0])
    conv_dim = zxbc_dim - d_inner
    tn_in = _tile(math.gcd(d_inner, conv_dim), 512, LANES)
    zs = _conv_matmul(u, ssm_w_in, 0, d_inner, [0], jnp.ones((1, d_inner), F32), jnp.zeros((1, d_inner), F32),
                      tn_in, False, F32, "in_proj_z")
    xbc = _conv_matmul(u, ssm_w_in, 0, conv_dim, [d_inner // tn_in], ssm_conv_w[0].astype(F32),
                       ssm_conv_b[0].astype(F32).reshape(1, -1), tn_in, False, F32, "in_proj_xbc",
                       conv_col_blocks=[0])
    dt_raw = _matmul(u, ssm_w_in, None, F32, 1024, 128, "dt_proj", col0=zxbc_dim, n=n_heads, layer=0)
    y = _ssd(zs, xbc, dt_raw, ssm_dt_bias[0], ssm_a_log[0], ssm_d[0], ssm_norm[0], d_inner, n_heads)
    mix = _matmul(y, ssm_w_out, None, F32, 1024, 256, "out_proj", a_buffers=1, layer=0)
    h1, u = _resnorm(h0, mix, norm_mix_post[0], [norm_ffn_pre[0]])
    f = ffn(u, 0)

    h2, ukv, uq = _resnorm(h1, f, norm_ffn_post[0], [kv_norm, norm_mix_pre[1]])
    kv = _matmul(ukv, w_kv, b_kv, F32, 1024, 512, "kv_proj")
    q = _matmul(uq, attn_w_q, attn_b_q[0], BF16, 1024, 512, "q_proj", layer=0)
    o = _attention(q, kv, attn_sinks[0], rel_bias)
    mix = _matmul(o, attn_w_o, attn_b_o[0], F32, 1024, 512, "o_proj", layer=0)
    h3, u = _resnorm(h2, mix, norm_mix_post[1], [norm_ffn_pre[1]])
    f = ffn(u, 1)
    (h4,) = _resnorm(h3, f, norm_ffn_post[1], [])
    return h4.reshape(bsz, s, d)
```

```python
import functools
import math

import numpy as np
import jax
import jax.numpy as jnp
from jax import lax
from jax.experimental import pallas as pl
from jax.experimental.pallas import tpu as pltpu

EPS = 1e-6
SSM_NORM_EPS = 1e-5
SSM_HEAD_DIM = 64
SSM_GROUPS = 8
D_STATE = 128
SSM_CONV = 4
SSM_CHUNK = 128
SSD_PAIRS_PER_DOT = 2
ATTN_HEAD_DIM = 64
N_KV_HEADS = 8
WINDOW = 128
ATTN_BLOCK = 128
N_BUCKETS = 32
MAX_DISTANCE = 128
FFN_CONV = 3

LANES = 128
SUBLANES = 8
BF16_TILE_ROWS = 16
MXU_K_CHUNK = 256
VMEM_CAP_BYTES = 60 * 1024 * 1024

F32 = jnp.float32
BF16 = jnp.bfloat16


def _vmem_limit(est_bytes):
    return int(min(VMEM_CAP_BYTES, max(32 * 1024 * 1024, est_bytes * 5 // 4 + (4 << 20))))


def _params(semantics, est_bytes, flags=None):
    return pltpu.CompilerParams(dimension_semantics=semantics, vmem_limit_bytes=_vmem_limit(est_bytes), flags=flags)


def _tile(dim, pref, align):
    if dim <= pref:
        return dim
    t = (pref // align) * align
    while t >= align:
        if dim % t == 0:
            return t
        t -= align
    raise ValueError(f"no tile for {dim} (pref {pref}, align {align})")


def _sigmoid(x):
    return 0.5 * jnp.tanh(0.5 * x) + 0.5


def _softplus(x):
    return jnp.maximum(x, 0.0) + jnp.log1p(jnp.exp(-jnp.abs(x)))


def _split3(v):
    hi = v.astype(BF16)
    r1 = v - hi.astype(F32)
    mid = r1.astype(BF16)
    lo = (r1 - mid.astype(F32)).astype(BF16)
    return [hi, mid, lo]


def _rms(x, g, eps):
    ms = jnp.mean(x * x, axis=-1, keepdims=True)
    return (x * lax.rsqrt(ms + eps)) * g


def _prenorm_kernel(x_ref, g_ref, u_ref):
    u_ref[...] = _rms(x_ref[...], g_ref[...], EPS).astype(u_ref.dtype)


def _prenorm(x, g):
    s, d = x.shape
    tr = _tile(s, 256, SUBLANES)
    est = 2 * tr * d * (4 + 2)
    return pl.pallas_call(
        _prenorm_kernel,
        grid=(s // tr,),
        in_specs=[pl.BlockSpec((tr, d), lambda i: (i, 0)), pl.BlockSpec((1, d), lambda i: (0, 0))],
        out_specs=pl.BlockSpec((tr, d), lambda i: (i, 0)),
        out_shape=jax.ShapeDtypeStruct((s, d), BF16),
        compiler_params=_params(("parallel",), est),
        name="prenorm",
    )(x, g.reshape(1, d))


def _resnorm_kernel(r_ref, m_ref, gpost_ref, *rest, n_u):
    g_refs = rest[:n_u]
    h_ref = rest[n_u]
    u_refs = rest[n_u + 1:]
    h = r_ref[...] + _rms(m_ref[...], gpost_ref[...], EPS)
    h_ref[...] = h
    if n_u:
        ms = jnp.mean(h * h, axis=-1, keepdims=True)
        hn = h * lax.rsqrt(ms + EPS)
        for g_ref, u_ref in zip(g_refs, u_refs):
            u_ref[...] = (hn * g_ref[...]).astype(u_ref.dtype)


def _resnorm(r, m, gpost, gains):
    s, d = r.shape
    n_u = len(gains)
    tr = _tile(s, 256, SUBLANES)
    est = 2 * tr * d * (4 * 3 + 2 * n_u)
    row = pl.BlockSpec((tr, d), lambda i: (i, 0))
    vec = pl.BlockSpec((1, d), lambda i: (0, 0))
    outs = pl.pallas_call(
        functools.partial(_resnorm_kernel, n_u=n_u),
        grid=(s // tr,),
        in_specs=[row, row, vec] + [vec] * n_u,
        out_specs=[row] + [row] * n_u,
        out_shape=[jax.ShapeDtypeStruct((s, d), F32)] + [jax.ShapeDtypeStruct((s, d), BF16)] * n_u,
        compiler_params=_params(("parallel",), est),
        name="resnorm",
    )(r, m, gpost.reshape(1, d), *[g.reshape(1, d) for g in gains])
    return outs


def _matmul_kernel(a_ref, w_ref, *rest, has_bias, silu):
    o_ref = rest[-1]
    acc = jnp.dot(a_ref[...], w_ref[...].astype(BF16), preferred_element_type=F32)
    if has_bias:
        acc = acc + rest[0][...]
    if silu:
        acc = acc * _sigmoid(acc)
    o_ref[...] = acc.astype(o_ref.dtype)


def _matmul(a, w, bias, out_dtype, tm_pref, tn_pref, name, a_buffers=2, col0=0, n=None, layer=None, silu=False):
    m, k = a.shape
    assert (w.ndim == 3) == (layer is not None)
    n = w.shape[-1] if n is None else n
    tm = _tile(m, tm_pref, SUBLANES)
    tn = _tile(math.gcd(n, col0) if col0 else n, tn_pref, LANES)
    assert n % tn == 0 and col0 % tn == 0
    joff = col0 // tn
    osz = jnp.dtype(out_dtype).itemsize
    wsz = jnp.dtype(w.dtype).itemsize
    est = a_buffers * tm * k * 2 + 2 * (k * tn * wsz + tm * tn * osz) + tm * tn * 4 + (k * tn * 2 if wsz != 2 else 0)
    a_mode = {} if a_buffers == 2 else {"pipeline_mode": pl.Buffered(a_buffers)}
    if layer is None:
        w_spec = pl.BlockSpec((k, tn), lambda i, j: (0, j + joff))
    else:
        w_spec = pl.BlockSpec((None, k, tn), lambda i, j: (layer, 0, j + joff))
    in_specs = [pl.BlockSpec((tm, k), lambda i, j: (i, 0), **a_mode), w_spec]
    args = [a, w]
    if bias is not None:
        in_specs.append(pl.BlockSpec((1, tn), lambda i, j: (0, j)))
        args.append(bias.reshape(1, n).astype(F32))
    return pl.pallas_call(
        functools.partial(_matmul_kernel, has_bias=bias is not None, silu=silu),
        grid=(m // tm, n // tn),
        in_specs=in_specs,
        out_specs=pl.BlockSpec((tm, tn), lambda i, j: (i, j)),
        out_shape=jax.ShapeDtypeStruct((m, n), out_dtype),
        compiler_params=_params(("parallel", "parallel"), est),
        name=name,
    )(*args)


def _zero_after(x):
    w = pltpu.bitcast(x, jnp.uint32)
    zero = (w >> 16) >> 16
    return zero if zero.shape[0] == BF16_TILE_ROWS else jnp.concatenate([zero, zero], axis=0)


def _conv_mm_kernel(u_ref, *refs, nj, n_w, taps, gate):
    w_refs = refs[0:n_w]
    cw_refs = refs[n_w:2 * n_w]
    cb_refs = refs[2 * n_w:3 * n_w]
    o_ref, halo, ext_a, ext_b, wcat = refs[3 * n_w:]
    t = pl.program_id(0)
    tm = u_ref.shape[0]
    d = u_ref.shape[1]
    tn = w_refs[0].shape[1]
    nch = d // MXU_K_CHUNK
    rows = tm // nch
    jp = lax.rem(jnp.maximum(t - 1, 0), nj)

    @pl.when(t == 0)
    def _():
        halo[...] = jnp.zeros(halo.shape, F32)
        ext_b[...] = jnp.zeros(ext_b.shape, F32)

    def step(ext_mm, ext_ep):
        cw = jnp.concatenate([r[...] for r in cw_refs], axis=1)
        cb = jnp.concatenate([r[...] for r in cb_refs], axis=1)
        prev = halo[jp]
        for c in range(nch):
            cur = ext_ep[c * rows:(c + 1) * rows, :]
            both = jnp.concatenate([prev, cur], axis=0)
            acc = cb
            for k in range(taps - 1):
                r0 = SUBLANES - (taps - 1) + k
                acc = acc + both[r0:r0 + rows, :] * cw[k:k + 1, :]
            acc = acc + cur * cw[taps - 1:taps, :]
            if gate:
                out = (acc[:, 0:tn] * _sigmoid(acc[:, 0:tn])) * acc[:, tn:2 * tn]
            else:
                out = acc * _sigmoid(acc)
            out = out.astype(o_ref.dtype)
            o_ref[c * rows:(c + 1) * rows, :] = out
            prev = cur[rows - SUBLANES:rows, :]

            zero = _zero_after(out[0:BF16_TILE_ROWS, 0:tn])
            k0 = c * MXU_K_CHUNK
            for idx, w_ref in enumerate(w_refs):
                top = pltpu.bitcast(pltpu.bitcast(w_ref[k0:k0 + BF16_TILE_ROWS, :], jnp.uint32) | zero, F32)
                wcat[k0:k0 + BF16_TILE_ROWS, idx * tn:(idx + 1) * tn] = top.astype(BF16)
                wcat[k0 + BF16_TILE_ROWS:k0 + MXU_K_CHUNK, idx * tn:(idx + 1) * tn] = (
                    w_ref[k0 + BF16_TILE_ROWS:k0 + MXU_K_CHUNK, :].astype(BF16))
        halo[jp] = prev
        ext_mm[...] = jnp.dot(u_ref[...], wcat[...], preferred_element_type=F32)

    @pl.when(lax.rem(t, 2) == 0)
    def _():
        step(ext_a, ext_b)

    @pl.when(lax.rem(t, 2) == 1)
    def _():
        step(ext_b, ext_a)


def _conv_matmul(u, w, layer, n_cols, col_blocks, conv_w, conv_b, tn, gate, out_dtype, name, conv_col_blocks=None):
    s, d = u.shape
    n_w = len(col_blocks)
    assert n_w == (2 if gate else 1) and d % MXU_K_CHUNK == 0
    taps = conv_w.shape[0]
    assert n_cols % tn == 0
    nj = n_cols // tn
    nch = d // MXU_K_CHUNK
    tm = _tile(s, 1024, nch * BF16_TILE_ROWS)
    wsz = jnp.dtype(w.dtype).itemsize
    osz = jnp.dtype(out_dtype).itemsize
    wt = n_w * tn
    wo = tn if gate else wt
    est = 2 * (tm * d * 2 + d * wt * wsz + tm * wo * osz) + d * wt * 2 + nj * SUBLANES * wt * 4 + 4 * tm * wt * 4
    n_tiles = (s // tm) * nj

    def mm_tile(t):
        return jnp.minimum(t, n_tiles - 1)

    def ep_tile(t):
        return jnp.maximum(t - 1, 0)

    w_specs = [pl.BlockSpec((None, d, tn), lambda t, o=o: (layer, 0, mm_tile(t) % nj + o)) for o in col_blocks]
    conv_col_blocks = col_blocks if conv_col_blocks is None else conv_col_blocks
    cw_specs = [pl.BlockSpec((taps, tn), lambda t, o=o: (0, ep_tile(t) % nj + o)) for o in conv_col_blocks]
    cb_specs = [pl.BlockSpec((1, tn), lambda t, o=o: (0, ep_tile(t) % nj + o)) for o in conv_col_blocks]
    return pl.pallas_call(
        functools.partial(_conv_mm_kernel, nj=nj, n_w=n_w, taps=taps, gate=gate),
        grid=(n_tiles + 1,),
        in_specs=[pl.BlockSpec((tm, d), lambda t: (mm_tile(t) // nj, 0))] + w_specs + cw_specs + cb_specs,
        out_specs=pl.BlockSpec((tm, wo), lambda t: (ep_tile(t) // nj, ep_tile(t) % nj)),
        out_shape=jax.ShapeDtypeStruct((s, nj * wo), out_dtype),
        scratch_shapes=[
            pltpu.VMEM((nj, SUBLANES, wt), F32),
            pltpu.VMEM((tm, wt), F32),
            pltpu.VMEM((tm, wt), F32),
            pltpu.VMEM((d, wt), BF16),
        ],
        compiler_params=_params(("arbitrary",), est),
        name=name,
    )(u, *([w] * n_w), *([conv_w] * n_w), *([conv_b] * n_w))


def _ssd_kernel(z_ref, x_ref, b_ref, c_ref, dt_ref, dtb_ref, alog_ref, dskip_ref, gnorm_ref, expand_ref,
                y_ref, state_ref, *, heads, groups):
    chunk = pl.program_id(0)

    @pl.when(chunk == 0)
    def _():
        state_ref[...] = jnp.zeros(state_ref.shape, F32)

    dtv_all = _softplus(dt_ref[...] + dtb_ref[...])
    a_all = dtv_all * (-jnp.exp(alog_ref[...]))
    a_all_t = jnp.transpose(a_all)
    n_all = dtv_all.shape[1]

    width = heads * SSM_HEAD_DIM
    for gi in range(groups):
        cols = slice(gi * width, (gi + 1) * width)
        ncols = slice(gi * D_STATE, (gi + 1) * D_STATE)
        shift = (n_all - gi * heads) % n_all
        dtv = pltpu.roll(dtv_all, shift, axis=1) if shift else dtv_all
        a = pltpu.roll(a_all, shift, axis=1) if shift else a_all
        y_ref[:, cols] = _ssd_group_chunk(
            z_ref[:, cols], x_ref[:, cols], b_ref[:, ncols], c_ref[:, ncols], dtv, a,
            a_all_t[gi * heads:(gi + 1) * heads, :], dskip_ref[:, cols], gnorm_ref[:, cols], expand_ref,
            state_ref.at[gi], heads).astype(y_ref.dtype)


def _ssd_group_chunk(zs, xs, bc, cc, dtv, a, a_t, dskip, gnorm, expand_ref, state_ref, heads):
    cl = SSM_CHUNK
    width = heads * SSM_HEAD_DIM

    li = lax.broadcasted_iota(jnp.int32, (cl, cl), 0)
    si = lax.broadcasted_iota(jnp.int32, (cl, cl), 1)
    tril = li >= si

    ones_lo = tril.astype(F32).astype(BF16)
    ones_up = (li <= si).astype(F32).astype(BF16)
    acum = jnp.dot(jnp.concatenate([ones_lo] * 3, axis=1), jnp.concatenate(_split3(a), axis=0),
                   preferred_element_type=F32)
    acum_t = jnp.dot(jnp.concatenate(_split3(a_t), axis=1), jnp.concatenate([ones_up] * 3, axis=0),
                     preferred_element_type=F32)
    alast = acum[cl - 1:cl, :]
    exp_acum = jnp.exp(acum)
    decay_end = jnp.exp(alast - acum)

    stacked = jnp.concatenate([dtv, exp_acum, decay_end], axis=0)
    wide = jnp.dot(jnp.concatenate(_split3(stacked)[0:2], axis=1), expand_ref[...], preferred_element_type=F32)
    dt_w = wide[0:cl, :]
    exp_acum_w = wide[cl:2 * cl, :]
    decay_end_w = wide[2 * cl:3 * cl, :]

    xdt = xs * dt_w
    state = state_ref[...]
    bc_t = jnp.transpose(bc).astype(BF16)
    c_prod = jnp.dot(cc.astype(BF16), jnp.concatenate([state.astype(BF16), bc_t], axis=1),
                     preferred_element_type=F32)
    y_off = c_prod[:, 0:width] * exp_acum_w
    cb = jnp.where(tril, c_prod[:, width:width + cl], 0.0)

    lane = lax.broadcasted_iota(jnp.int32, (cl, LANES), 1)
    low = lane < SSM_HEAD_DIM
    npairs = heads // 2
    ppd = min(SSD_PAIRS_PER_DOT, npairs)
    zero_blk = jnp.zeros((cl, LANES), BF16)
    y_diag = []
    for q in range(npairs // ppd):
        ms = []
        rhs_rows = []
        for pi in range(ppd):
            p = q * ppd + pi
            for hp in range(2):
                r = 2 * p + hp
                seg = acum[:, r:r + 1] - acum_t[r:r + 1, :]
                decay = jnp.exp(jnp.minimum(seg, 0.0))
                ms.append((cb * decay).astype(BF16))
            xp = xdt[:, p * LANES:(p + 1) * LANES]
            for blk in (jnp.where(low, xp, 0.0).astype(BF16), jnp.where(low, 0.0, xp).astype(BF16)):
                rhs_rows.append(jnp.concatenate([zero_blk] * pi + [blk] + [zero_blk] * (ppd - 1 - pi), axis=1))
        y_diag.append(jnp.dot(jnp.concatenate(ms, axis=1), jnp.concatenate(rhs_rows, axis=0),
                              preferred_element_type=F32))
    y = jnp.concatenate(y_diag, axis=1) + y_off + dskip * xs

    xw = (xdt * decay_end_w).astype(BF16)
    upd = jnp.dot(bc_t, xw, preferred_element_type=F32)
    state_ref[...] = state * exp_acum_w[cl - 1:cl, :] + upd

    yz = y * zs
    ms2 = jnp.mean(yz * yz, axis=-1, keepdims=True)
    return (yz * lax.rsqrt(ms2 + SSM_NORM_EPS)) * gnorm


def _ssd(zs, xbc, dt_raw, dt_bias, a_log, d_skip, g_norm, d_inner, n_heads):
    s = zs.shape[0]
    g = SSM_GROUPS
    heads = n_heads // g
    width = d_inner // g
    assert width == heads * SSM_HEAD_DIM and heads % 2 == 0 and heads <= LANES and width % LANES == 0
    assert (heads // 2) % min(SSD_PAIRS_PER_DOT, heads // 2) == 0
    assert D_STATE == LANES and s % SSM_CHUNK == 0
    dskip_w = jnp.repeat(d_skip.astype(F32), SSM_HEAD_DIM).reshape(1, d_inner)
    nc = s // SSM_CHUNK
    cl = SSM_CHUNK
    expand2 = np.tile(np.arange(n_heads)[:, None] == (np.arange(width)[None, :] // SSM_HEAD_DIM), (2, 1))

    gn = g * D_STATE
    assert d_inner % gn == 0
    est = 2 * (2 * cl * d_inner * 4 + 2 * cl * gn * 4 + cl * n_heads * 4 + cl * d_inner * 2) \
        + g * D_STATE * width * 4 + 24 * cl * width * 4
    return pl.pallas_call(
        functools.partial(_ssd_kernel, heads=heads, groups=g),
        grid=(nc,),
        in_specs=[
            pl.BlockSpec((cl, d_inner), lambda c: (c, 0)),
            pl.BlockSpec((cl, d_inner), lambda c: (c, 0)),
            pl.BlockSpec((cl, gn), lambda c: (c, d_inner // gn)),
            pl.BlockSpec((cl, gn), lambda c: (c, d_inner // gn + 1)),
            pl.BlockSpec((cl, n_heads), lambda c: (c, 0)),
            pl.BlockSpec((1, n_heads), lambda c: (0, 0)),
            pl.BlockSpec((1, n_heads), lambda c: (0, 0)),
            pl.BlockSpec((1, d_inner), lambda c: (0, 0)),
            pl.BlockSpec((1, d_inner), lambda c: (0, 0)),
            pl.BlockSpec((2 * n_heads, width), lambda c: (0, 0)),
        ],
        out_specs=pl.BlockSpec((cl, d_inner), lambda c: (c, 0)),
        out_shape=jax.ShapeDtypeStruct((s, d_inner), BF16),
        scratch_shapes=[
            pltpu.VMEM((g, D_STATE, width), F32),
        ],
        compiler_params=_params(("arbitrary",), est),
        name="ssd_scan",
    )(zs, xbc, xbc, xbc, dt_raw, dt_bias.astype(F32).reshape(1, n_heads), a_log.astype(F32).reshape(1, n_heads),
      dskip_w, g_norm.reshape(1, d_inner), jnp.asarray(expand2, BF16))


def _t5_bucket_table():
    q = np.arange(ATTN_BLOCK)[:, None]
    k = np.arange(2 * ATTN_BLOCK)[None, :]
    rel = np.maximum(q - k + ATTN_BLOCK, 0)
    max_exact = N_BUCKETS // 2
    relf = np.maximum(rel, 1).astype(np.float32)
    large = max_exact + (np.log(relf / np.float32(max_exact)) / np.float32(math.log(MAX_DISTANCE / max_exact))
                         * np.float32(N_BUCKETS - max_exact)).astype(np.int32)
    large = np.minimum(large, N_BUCKETS - 1)
    return np.where(rel < max_exact, rel, large).astype(np.int32)


def _attn_kernel(relb_ref, sink_ref, bucket_ref, q_ref, kp_ref, kc_ref, vp_ref, vc_ref, o_ref, bias_ref,
                 *, qpk, n_q_heads):
    gp = pl.program_id(0)
    n = pl.program_id(1)
    blk = ATTN_BLOCK
    npair = qpk // 2
    head0 = gp * 2 * qpk

    @pl.when(n == 0)
    def _():
        bucket = bucket_ref[...]
        qi = lax.broadcasted_iota(jnp.int32, (blk, 2 * blk), 0)
        ci = lax.broadcasted_iota(jnp.int32, (blk, 2 * blk), 1)
        rel = qi - ci + blk
        in_window = (rel >= 0) & (rel < WINDOW)

        def body(it, carry):
            kvh = it // npair
            pair = it - kvh * npair
            row0 = pl.multiple_of(pair * blk, blk)
            for hp in range(2):
                head = head0 + kvh * qpk + pair * 2 + hp
                acc = jnp.zeros((blk, 2 * blk), F32)
                for b in range(N_BUCKETS):
                    acc = jnp.where(bucket == b, relb_ref[b * n_q_heads + head], acc)
                bias_ref[kvh, pl.ds(row0, blk), hp * 2 * blk:(hp + 1) * 2 * blk] = jnp.where(in_window, acc, -jnp.inf)
            return carry

        lax.fori_loop(0, 2 * npair, body, 0)

    kk = jnp.concatenate([kp_ref[...], kc_ref[...]], axis=0)
    vv = jnp.concatenate([vp_ref[...], vc_ref[...]], axis=0)
    lane = lax.broadcasted_iota(jnp.int32, (2 * blk, LANES), 1)
    low = lane < ATTN_HEAD_DIM
    scale = ATTN_HEAD_DIM ** -0.5

    def pair_operand(t, kvh, mult):
        sel = jnp.where(low if kvh == 0 else jnp.logical_not(low), t, 0.0)
        dup = sel + pltpu.roll(sel, ATTN_HEAD_DIM, axis=1)
        if mult is not None:
            dup = dup * mult
        return jnp.concatenate([jnp.where(low, dup, 0.0), jnp.where(low, 0.0, dup)], axis=0).astype(BF16)

    row_bd = lax.broadcasted_iota(jnp.int32, (4 * blk, LANES), 0)
    lane_bd = lax.broadcasted_iota(jnp.int32, (4 * blk, LANES), 1)
    ones_bd = ((row_bd < 2 * blk) == (lane_bd < ATTN_HEAD_DIM)).astype(F32).astype(BF16)
    low_out = lax.broadcasted_iota(jnp.int32, (npair * blk, LANES), 1) < ATTN_HEAD_DIM

    def run(first_block):
        if first_block:
            ci = lax.broadcasted_iota(jnp.int32, (npair * blk, 2 * blk), 1)
            before_start = ci < blk
        for kvh in range(2):
            kbd = pair_operand(kk, kvh, scale)
            vbd = pair_operand(vv, kvh, None)
            base = kvh * npair
            qs = jnp.concatenate([q_ref[:, (base + i) * LANES:(base + i + 1) * LANES] for i in range(npair)], axis=0)
            s = lax.dot_general(qs, kbd, (((1,), (1,)), ((), ())), preferred_element_type=F32)
            es = []
            sink_terms = []
            for hp in range(2):
                sinks = [sink_ref[head0 + kvh * qpk + i * 2 + hp] for i in range(npair)]
                sink_col = jnp.concatenate([jnp.full((blk, 1), v, F32) for v in sinks], axis=0)
                sink_all = jnp.concatenate([jnp.full((blk, LANES), v, F32) for v in sinks], axis=0)
                sh = s[:, hp * 2 * blk:(hp + 1) * 2 * blk] + bias_ref[kvh, :, hp * 2 * blk:(hp + 1) * 2 * blk]
                if first_block:
                    sh = jnp.where(before_start, -jnp.inf, sh)
                m = jnp.maximum(jnp.max(sh, axis=-1, keepdims=True), sink_col)
                es.append(jnp.exp(sh - m).astype(BF16))
                sink_terms.append(jnp.exp(sink_all - jnp.broadcast_to(m, sink_all.shape)))
            e = jnp.concatenate(es, axis=1)
            ov = jnp.dot(e, jnp.concatenate([vbd, ones_bd], axis=1), preferred_element_type=F32)
            denom = ov[:, LANES:2 * LANES] + jnp.where(low_out, sink_terms[0], sink_terms[1])
            o = (ov[:, 0:LANES] * (1.0 / denom)).astype(o_ref.dtype)
            for i in range(npair):
                o_ref[:, (base + i) * LANES:(base + i + 1) * LANES] = o[i * blk:(i + 1) * blk, :]

    @pl.when(n == 0)
    def _():
        run(True)

    @pl.when(n != 0)
    def _():
        run(False)


def _attention(q, kv, sinks, rel_bias):
    s, qd = q.shape
    n_q_heads = qd // ATTN_HEAD_DIM
    qpk = n_q_heads // N_KV_HEADS
    assert qpk % 2 == 0 and N_KV_HEADS % 2 == 0 and 2 * ATTN_HEAD_DIM == LANES and s % ATTN_BLOCK == 0
    blk = ATTN_BLOCK
    nb = s // blk
    ngp = N_KV_HEADS // 2
    qw = 2 * qpk * ATTN_HEAD_DIM
    voff = N_KV_HEADS * ATTN_HEAD_DIM // LANES
    bucket = jnp.asarray(_t5_bucket_table())
    est = 2 * (2 * blk * qw * 2 + 4 * blk * LANES * 4 + blk * 2 * blk * 4) + 2 * qpk * blk * 2 * blk * 4 \
        + 24 * blk * 4 * blk * 4
    smem = pl.BlockSpec(memory_space=pltpu.SMEM)
    return pl.pallas_call(
        functools.partial(_attn_kernel, qpk=qpk, n_q_heads=n_q_heads),
        grid=(ngp, nb),
        in_specs=[
            smem, smem,
            pl.BlockSpec((blk, 2 * blk), lambda g, n: (0, 0)),
            pl.BlockSpec((blk, qw), lambda g, n: (n, g)),
            pl.BlockSpec((blk, LANES), lambda g, n: (jnp.maximum(n - 1, 0), g)),
            pl.BlockSpec((blk, LANES), lambda g, n: (n, g)),
            pl.BlockSpec((blk, LANES), lambda g, n: (jnp.maximum(n - 1, 0), voff + g)),
            pl.BlockSpec((blk, LANES), lambda g, n: (n, voff + g)),
        ],
        out_specs=pl.BlockSpec((blk, qw), lambda g, n: (n, g)),
        out_shape=jax.ShapeDtypeStruct((s, qd), BF16),
        scratch_shapes=[pltpu.VMEM((2, (qpk // 2) * blk, 4 * blk), F32)],
        compiler_params=_params(("parallel", "arbitrary"), est),
        name="swa_attention",
    )(rel_bias.astype(F32).reshape(-1), sinks.astype(F32).reshape(-1), bucket, q, kv, kv, kv, kv)


def kernel(x, norm_mix_pre, norm_mix_post, norm_ffn_pre, norm_ffn_post, ssm_w_in, ssm_conv_w, ssm_conv_b, ssm_dt_bias, ssm_a_log, ssm_d, ssm_norm, ssm_w_out, kv_norm, w_kv, b_kv, attn_w_q, attn_b_q, attn_sinks, attn_w_o, attn_b_o, rel_bias, ffn_w_up, ffn_conv_w, ffn_conv_b, ffn_w_down):
    bsz, s, d = x.shape
    assert bsz == 1 and norm_mix_pre.shape[0] == 2
    d_inner = ssm_norm.shape[-1]
    n_heads = ssm_dt_bias.shape[-1]
    zxbc_dim = ssm_w_in.shape[-1] - n_heads

    def ffn(h_in_u, layer):
        d_ff = ffn_w_up.shape[-1] // 2
        tn = _tile(d_ff, 256, LANES)
        hff = _conv_matmul(h_in_u, ffn_w_up, layer, d_ff, [0, d_ff // tn], ffn_conv_w[layer],
                           ffn_conv_b[layer].reshape(1, -1), tn, True, BF16, "ffn_up")
        return _matmul(hff, ffn_w_down, None, F32, 1024, 256, "ffn_down", a_buffers=1, layer=layer)

    h0 = x.reshape(s, d)

    u = _prenorm(h0, norm_mix_pre[0])
    conv_dim = zxbc_dim - d_inner
    tn_in = _tile(math.gcd(d_inner, conv_dim), 512, LANES)
    zs = _matmul(u, ssm_w_in, None, F32, 1024, 512, "in_proj_z", n=d_inner, layer=0, silu=True)
    xbc = _conv_matmul(u, ssm_w_in, 0, conv_dim, [d_inner // tn_in], ssm_conv_w[0].astype(F32),
                       ssm_conv_b[0].astype(F32).reshape(1, -1), tn_in, False, F32, "in_proj_xbc",
                       conv_col_blocks=[0])
    dt_raw = _matmul(u, ssm_w_in, None, F32, 1024, 128, "dt_proj", col0=zxbc_dim, n=n_heads, layer=0)
    y = _ssd(zs, xbc, dt_raw, ssm_dt_bias[0], ssm_a_log[0], ssm_d[0], ssm_norm[0], d_inner, n_heads)
    mix = _matmul(y, ssm_w_out, None, F32, 1024, 256, "out_proj", a_buffers=1, layer=0)
    h1, u = _resnorm(h0, mix, norm_mix_post[0], [norm_ffn_pre[0]])
    f = ffn(u, 0)

    h2, ukv, uq = _resnorm(h1, f, norm_ffn_post[0], [kv_norm, norm_mix_pre[1]])
    kv = _matmul(ukv, w_kv, b_kv, F32, 1024, 512, "kv_proj")
    q = _matmul(uq, attn_w_q, attn_b_q[0], BF16, 1024, 512, "q_proj", layer=0)
    o = _attention(q, kv, attn_sinks[0], rel_bias)
    mix = _matmul(o, attn_w_o, attn_b_o[0], F32, 1024, 512, "o_proj", layer=0)
    h3, u = _resnorm(h2, mix, norm_mix_post[1], [norm_ffn_pre[1]])
    f = ffn(u, 1)
    (h4,) = _resnorm(h3, f, norm_ffn_post[1], [])
    return h4.reshape(bsz, s, d)
```

```python
import functools
import math

import numpy as np
import jax
import jax.numpy as jnp
from jax import lax
from jax.experimental import pallas as pl
from jax.experimental.pallas import tpu as pltpu

EPS = 1e-6
SSM_NORM_EPS = 1e-5
SSM_HEAD_DIM = 64
SSM_GROUPS = 8
D_STATE = 128
SSM_CONV = 4
SSM_CHUNK = 128
SSD_PAIRS_PER_DOT = 2
ATTN_HEAD_DIM = 64
N_KV_HEADS = 8
WINDOW = 128
ATTN_BLOCK = 128
N_BUCKETS = 32
MAX_DISTANCE = 128
FFN_CONV = 3

LANES = 128
SUBLANES = 8
BF16_TILE_ROWS = 16
MXU_K_CHUNK = 256
VMEM_CAP_BYTES = 60 * 1024 * 1024

F32 = jnp.float32
BF16 = jnp.bfloat16


def _vmem_limit(est_bytes):
    return int(min(VMEM_CAP_BYTES, max(32 * 1024 * 1024, est_bytes * 5 // 4 + (4 << 20))))


def _params(semantics, est_bytes, flags=None):
    return pltpu.CompilerParams(dimension_semantics=semantics, vmem_limit_bytes=_vmem_limit(est_bytes), flags=flags)


def _tile(dim, pref, align):
    if dim <= pref:
        return dim
    t = (pref // align) * align
    while t >= align:
        if dim % t == 0:
            return t
        t -= align
    raise ValueError(f"no tile for {dim} (pref {pref}, align {align})")


def _sigmoid(x):
    return 0.5 * jnp.tanh(0.5 * x) + 0.5


def _softplus(x):
    return jnp.maximum(x, 0.0) + jnp.log1p(jnp.exp(-jnp.abs(x)))


def _split3(v):
    hi = v.astype(BF16)
    r1 = v - hi.astype(F32)
    mid = r1.astype(BF16)
    lo = (r1 - mid.astype(F32)).astype(BF16)
    return [hi, mid, lo]


def _rms(x, g, eps):
    ms = jnp.mean(x * x, axis=-1, keepdims=True)
    return (x * lax.rsqrt(ms + eps)) * g


def _prenorm_kernel(x_ref, g_ref, u_ref):
    u_ref[...] = _rms(x_ref[...], g_ref[...], EPS).astype(u_ref.dtype)


def _prenorm(x, g):
    s, d = x.shape
    tr = _tile(s, 256, SUBLANES)
    est = 2 * tr * d * (4 + 2)
    return pl.pallas_call(
        _prenorm_kernel,
        grid=(s // tr,),
        in_specs=[pl.BlockSpec((tr, d), lambda i: (i, 0)), pl.BlockSpec((1, d), lambda i: (0, 0))],
        out_specs=pl.BlockSpec((tr, d), lambda i: (i, 0)),
        out_shape=jax.ShapeDtypeStruct((s, d), BF16),
        compiler_params=_params(("parallel",), est),
        name="prenorm",
    )(x, g.reshape(1, d))


def _resnorm_kernel(r_ref, m_ref, gpost_ref, *rest, n_u):
    g_refs = rest[:n_u]
    h_ref = rest[n_u]
    u_refs = rest[n_u + 1:]
    h = r_ref[...] + _rms(m_ref[...], gpost_ref[...], EPS)
    h_ref[...] = h
    if n_u:
        ms = jnp.mean(h * h, axis=-1, keepdims=True)
        hn = h * lax.rsqrt(ms + EPS)
        for g_ref, u_ref in zip(g_refs, u_refs):
            u_ref[...] = (hn * g_ref[...]).astype(u_ref.dtype)


def _resnorm(r, m, gpost, gains):
    s, d = r.shape
    n_u = len(gains)
    tr = _tile(s, 256, SUBLANES)
    est = 2 * tr * d * (4 * 3 + 2 * n_u)
    row = pl.BlockSpec((tr, d), lambda i: (i, 0))
    vec = pl.BlockSpec((1, d), lambda i: (0, 0))
    outs = pl.pallas_call(
        functools.partial(_resnorm_kernel, n_u=n_u),
        grid=(s // tr,),
        in_specs=[row, row, vec] + [vec] * n_u,
        out_specs=[row] + [row] * n_u,
        out_shape=[jax.ShapeDtypeStruct((s, d), F32)] + [jax.ShapeDtypeStruct((s, d), BF16)] * n_u,
        compiler_params=_params(("parallel",), est),
        name="resnorm",
    )(r, m, gpost.reshape(1, d), *[g.reshape(1, d) for g in gains])
    return outs


def _matmul_kernel(a_ref, w_ref, *rest, has_bias, silu):
    o_ref = rest[-1]
    acc = jnp.dot(a_ref[...], w_ref[...].astype(BF16), preferred_element_type=F32)
    if has_bias:
        acc = acc + rest[0][...]
    if silu:
        acc = acc * _sigmoid(acc)
    o_ref[...] = acc.astype(o_ref.dtype)


def _matmul(a, w, bias, out_dtype, tm_pref, tn_pref, name, a_buffers=2, col0=0, n=None, layer=None, silu=False,
            cols_outer=False):
    m, k = a.shape
    assert (w.ndim == 3) == (layer is not None)
    n = w.shape[-1] if n is None else n
    tm = _tile(m, tm_pref, SUBLANES)
    tn = _tile(math.gcd(n, col0) if col0 else n, tn_pref, LANES)
    assert n % tn == 0 and col0 % tn == 0
    joff = col0 // tn
    osz = jnp.dtype(out_dtype).itemsize
    wsz = jnp.dtype(w.dtype).itemsize
    est = a_buffers * tm * k * 2 + 2 * (k * tn * wsz + tm * tn * osz) + tm * tn * 4 + (k * tn * 2 if wsz != 2 else 0)
    a_mode = {} if a_buffers == 2 else {"pipeline_mode": pl.Buffered(a_buffers)}
    ij = (lambda p, q: (q, p)) if cols_outer else (lambda p, q: (p, q))
    if layer is None:
        w_spec = pl.BlockSpec((k, tn), lambda p, q: (0, ij(p, q)[1] + joff))
    else:
        w_spec = pl.BlockSpec((None, k, tn), lambda p, q: (layer, 0, ij(p, q)[1] + joff))
    in_specs = [pl.BlockSpec((tm, k), lambda p, q: (ij(p, q)[0], 0), **a_mode), w_spec]
    args = [a, w]
    if bias is not None:
        in_specs.append(pl.BlockSpec((1, tn), lambda p, q: (0, ij(p, q)[1])))
        args.append(bias.reshape(1, n).astype(F32))
    return pl.pallas_call(
        functools.partial(_matmul_kernel, has_bias=bias is not None, silu=silu),
        grid=(n // tn, m // tm) if cols_outer else (m // tm, n // tn),
        in_specs=in_specs,
        out_specs=pl.BlockSpec((tm, tn), lambda p, q: ij(p, q)),
        out_shape=jax.ShapeDtypeStruct((m, n), out_dtype),
        compiler_params=_params(("parallel", "parallel"), est),
        name=name,
    )(*args)


def _zero_after(x):
    w = pltpu.bitcast(x, jnp.uint32)
    zero = (w >> 16) >> 16
    return zero if zero.shape[0] == BF16_TILE_ROWS else jnp.concatenate([zero, zero], axis=0)


def _conv_mm_kernel(u_ref, *refs, nj, n_w, taps, gate):
    w_refs = refs[0:n_w]
    cw_refs = refs[n_w:2 * n_w]
    cb_refs = refs[2 * n_w:3 * n_w]
    o_ref, halo, ext_a, ext_b, wcat = refs[3 * n_w:]
    t = pl.program_id(0)
    tm = u_ref.shape[0]
    d = u_ref.shape[1]
    tn = w_refs[0].shape[1]
    nch = d // MXU_K_CHUNK
    rows = tm // nch
    jp = lax.rem(jnp.maximum(t - 1, 0), nj)

    @pl.when(t == 0)
    def _():
        halo[...] = jnp.zeros(halo.shape, F32)
        ext_b[...] = jnp.zeros(ext_b.shape, F32)

    def step(ext_mm, ext_ep):
        cw = jnp.concatenate([r[...] for r in cw_refs], axis=1)
        cb = jnp.concatenate([r[...] for r in cb_refs], axis=1)
        prev = halo[jp]
        for c in range(nch):
            cur = ext_ep[c * rows:(c + 1) * rows, :]
            both = jnp.concatenate([prev, cur], axis=0)
            acc = cb
            for k in range(taps - 1):
                r0 = SUBLANES - (taps - 1) + k
                acc = acc + both[r0:r0 + rows, :] * cw[k:k + 1, :]
            acc = acc + cur * cw[taps - 1:taps, :]
            if gate:
                out = (acc[:, 0:tn] * _sigmoid(acc[:, 0:tn])) * acc[:, tn:2 * tn]
            else:
                out = acc * _sigmoid(acc)
            out = out.astype(o_ref.dtype)
            o_ref[c * rows:(c + 1) * rows, :] = out
            prev = cur[rows - SUBLANES:rows, :]

            zero = _zero_after(out[0:BF16_TILE_ROWS, 0:tn])
            k0 = c * MXU_K_CHUNK
            for idx, w_ref in enumerate(w_refs):
                top = pltpu.bitcast(pltpu.bitcast(w_ref[k0:k0 + BF16_TILE_ROWS, :], jnp.uint32) | zero, F32)
                wcat[k0:k0 + BF16_TILE_ROWS, idx * tn:(idx + 1) * tn] = top.astype(BF16)
                wcat[k0 + BF16_TILE_ROWS:k0 + MXU_K_CHUNK, idx * tn:(idx + 1) * tn] = (
                    w_ref[k0 + BF16_TILE_ROWS:k0 + MXU_K_CHUNK, :].astype(BF16))
        halo[jp] = prev
        ext_mm[...] = jnp.dot(u_ref[...], wcat[...], preferred_element_type=F32)

    @pl.when(lax.rem(t, 2) == 0)
    def _():
        step(ext_a, ext_b)

    @pl.when(lax.rem(t, 2) == 1)
    def _():
        step(ext_b, ext_a)


def _conv_matmul(u, w, layer, n_cols, col_blocks, conv_w, conv_b, tn, gate, out_dtype, name, conv_col_blocks=None):
    s, d = u.shape
    n_w = len(col_blocks)
    assert n_w == (2 if gate else 1) and d % MXU_K_CHUNK == 0
    taps = conv_w.shape[0]
    assert n_cols % tn == 0
    nj = n_cols // tn
    nch = d // MXU_K_CHUNK
    tm = _tile(s, 1024, nch * BF16_TILE_ROWS)
    wsz = jnp.dtype(w.dtype).itemsize
    osz = jnp.dtype(out_dtype).itemsize
    wt = n_w * tn
    wo = tn if gate else wt
    est = 2 * (tm * d * 2 + d * wt * wsz + tm * wo * osz) + d * wt * 2 + nj * SUBLANES * wt * 4 + 4 * tm * wt * 4
    n_tiles = (s // tm) * nj

    def mm_tile(t):
        return jnp.minimum(t, n_tiles - 1)

    def ep_tile(t):
        return jnp.maximum(t - 1, 0)

    w_specs = [pl.BlockSpec((None, d, tn), lambda t, o=o: (layer, 0, mm_tile(t) % nj + o)) for o in col_blocks]
    conv_col_blocks = col_blocks if conv_col_blocks is None else conv_col_blocks
    cw_specs = [pl.BlockSpec((taps, tn), lambda t, o=o: (0, ep_tile(t) % nj + o)) for o in conv_col_blocks]
    cb_specs = [pl.BlockSpec((1, tn), lambda t, o=o: (0, ep_tile(t) % nj + o)) for o in conv_col_blocks]
    return pl.pallas_call(
        functools.partial(_conv_mm_kernel, nj=nj, n_w=n_w, taps=taps, gate=gate),
        grid=(n_tiles + 1,),
        in_specs=[pl.BlockSpec((tm, d), lambda t: (mm_tile(t) // nj, 0))] + w_specs + cw_specs + cb_specs,
        out_specs=pl.BlockSpec((tm, wo), lambda t: (ep_tile(t) // nj, ep_tile(t) % nj)),
        out_shape=jax.ShapeDtypeStruct((s, nj * wo), out_dtype),
        scratch_shapes=[
            pltpu.VMEM((nj, SUBLANES, wt), F32),
            pltpu.VMEM((tm, wt), F32),
            pltpu.VMEM((tm, wt), F32),
            pltpu.VMEM((d, wt), BF16),
        ],
        compiler_params=_params(("arbitrary",), est),
        name=name,
    )(u, *([w] * n_w), *([conv_w] * n_w), *([conv_b] * n_w))


def _ssd_kernel(z_ref, x_ref, b_ref, c_ref, dt_ref, dtb_ref, alog_ref, dskip_ref, gnorm_ref, expand_ref,
                y_ref, state_ref, *, heads, groups):
    chunk = pl.program_id(0)

    @pl.when(chunk == 0)
    def _():
        state_ref[...] = jnp.zeros(state_ref.shape, F32)

    dtv_all = _softplus(dt_ref[...] + dtb_ref[...])
    a_all = dtv_all * (-jnp.exp(alog_ref[...]))
    a_all_t = jnp.transpose(a_all)
    n_all = dtv_all.shape[1]

    width = heads * SSM_HEAD_DIM
    for gi in range(groups):
        cols = slice(gi * width, (gi + 1) * width)
        ncols = slice(gi * D_STATE, (gi + 1) * D_STATE)
        shift = (n_all - gi * heads) % n_all
        dtv = pltpu.roll(dtv_all, shift, axis=1) if shift else dtv_all
        a = pltpu.roll(a_all, shift, axis=1) if shift else a_all
        y_ref[:, cols] = _ssd_group_chunk(
            z_ref[:, cols], x_ref[:, cols], b_ref[:, ncols], c_ref[:, ncols], dtv, a,
            a_all_t[gi * heads:(gi + 1) * heads, :], dskip_ref[:, cols], gnorm_ref[:, cols], expand_ref,
            state_ref.at[gi], heads).astype(y_ref.dtype)


def _ssd_group_chunk(zs, xs, bc, cc, dtv, a, a_t, dskip, gnorm, expand_ref, state_ref, heads):
    cl = SSM_CHUNK
    width = heads * SSM_HEAD_DIM

    li = lax.broadcasted_iota(jnp.int32, (cl, cl), 0)
    si = lax.broadcasted_iota(jnp.int32, (cl, cl), 1)
    tril = li >= si

    ones_lo = tril.astype(F32).astype(BF16)
    ones_up = (li <= si).astype(F32).astype(BF16)
    acum = jnp.dot(jnp.concatenate([ones_lo] * 3, axis=1), jnp.concatenate(_split3(a), axis=0),
                   preferred_element_type=F32)
    acum_t = jnp.dot(jnp.concatenate(_split3(a_t), axis=1), jnp.concatenate([ones_up] * 3, axis=0),
                     preferred_element_type=F32)
    alast = acum[cl - 1:cl, :]
    exp_acum = jnp.exp(acum)
    decay_end = jnp.exp(alast - acum)

    stacked = jnp.concatenate([dtv, exp_acum, decay_end], axis=0)
    wide = jnp.dot(jnp.concatenate(_split3(stacked)[0:2], axis=1), expand_ref[...], preferred_element_type=F32)
    dt_w = wide[0:cl, :]
    exp_acum_w = wide[cl:2 * cl, :]
    decay_end_w = wide[2 * cl:3 * cl, :]

    xdt = xs * dt_w
    state = state_ref[...]
    bc_t = jnp.transpose(bc).astype(BF16)
    c_prod = jnp.dot(cc.astype(BF16), jnp.concatenate([state.astype(BF16), bc_t], axis=1),
                     preferred_element_type=F32)
    y_off = c_prod[:, 0:width] * exp_acum_w
    cb = jnp.where(tril, c_prod[:, width:width + cl], 0.0)

    lane = lax.broadcasted_iota(jnp.int32, (cl, LANES), 1)
    low = lane < SSM_HEAD_DIM
    npairs = heads // 2
    ppd = min(SSD_PAIRS_PER_DOT, npairs)
    zero_blk = jnp.zeros((cl, LANES), BF16)
    y_diag = []
    for q in range(npairs // ppd):
        ms = []
        rhs_rows = []
        for pi in range(ppd):
            p = q * ppd + pi
            for hp in range(2):
                r = 2 * p + hp
                seg = acum[:, r:r + 1] - acum_t[r:r + 1, :]
                decay = jnp.exp(jnp.minimum(seg, 0.0))
                ms.append((cb * decay).astype(BF16))
            xp = xdt[:, p * LANES:(p + 1) * LANES]
            for blk in (jnp.where(low, xp, 0.0).astype(BF16), jnp.where(low, 0.0, xp).astype(BF16)):
                rhs_rows.append(jnp.concatenate([zero_blk] * pi + [blk] + [zero_blk] * (ppd - 1 - pi), axis=1))
        y_diag.append(jnp.dot(jnp.concatenate(ms, axis=1), jnp.concatenate(rhs_rows, axis=0),
                              preferred_element_type=F32))
    y = jnp.concatenate(y_diag, axis=1) + y_off + dskip * xs

    xw = (xdt * decay_end_w).astype(BF16)
    upd = jnp.dot(bc_t, xw, preferred_element_type=F32)
    state_ref[...] = state * exp_acum_w[cl - 1:cl, :] + upd

    yz = y * zs
    ms2 = jnp.mean(yz * yz, axis=-1, keepdims=True)
    return (yz * lax.rsqrt(ms2 + SSM_NORM_EPS)) * gnorm


def _ssd(zs, xbc, dt_raw, dt_bias, a_log, d_skip, g_norm, d_inner, n_heads):
    s = zs.shape[0]
    g = SSM_GROUPS
    heads = n_heads // g
    width = d_inner // g
    assert width == heads * SSM_HEAD_DIM and heads % 2 == 0 and heads <= LANES and width % LANES == 0
    assert (heads // 2) % min(SSD_PAIRS_PER_DOT, heads // 2) == 0
    assert D_STATE == LANES and s % SSM_CHUNK == 0
    dskip_w = jnp.repeat(d_skip.astype(F32), SSM_HEAD_DIM).reshape(1, d_inner)
    nc = s // SSM_CHUNK
    cl = SSM_CHUNK
    expand2 = np.tile(np.arange(n_heads)[:, None] == (np.arange(width)[None, :] // SSM_HEAD_DIM), (2, 1))

    gn = g * D_STATE
    assert d_inner % gn == 0
    est = 2 * (2 * cl * d_inner * 4 + 2 * cl * gn * 4 + cl * n_heads * 4 + cl * d_inner * 2) \
        + g * D_STATE * width * 4 + 24 * cl * width * 4
    return pl.pallas_call(
        functools.partial(_ssd_kernel, heads=heads, groups=g),
        grid=(nc,),
        in_specs=[
            pl.BlockSpec((cl, d_inner), lambda c: (c, 0)),
            pl.BlockSpec((cl, d_inner), lambda c: (c, 0)),
            pl.BlockSpec((cl, gn), lambda c: (c, d_inner // gn)),
            pl.BlockSpec((cl, gn), lambda c: (c, d_inner // gn + 1)),
            pl.BlockSpec((cl, n_heads), lambda c: (c, 0)),
            pl.BlockSpec((1, n_heads), lambda c: (0, 0)),
            pl.BlockSpec((1, n_heads), lambda c: (0, 0)),
            pl.BlockSpec((1, d_inner), lambda c: (0, 0)),
            pl.BlockSpec((1, d_inner), lambda c: (0, 0)),
            pl.BlockSpec((2 * n_heads, width), lambda c: (0, 0)),
        ],
        out_specs=pl.BlockSpec((cl, d_inner), lambda c: (c, 0)),
        out_shape=jax.ShapeDtypeStruct((s, d_inner), BF16),
        scratch_shapes=[
            pltpu.VMEM((g, D_STATE, width), F32),
        ],
        compiler_params=_params(("arbitrary",), est),
        name="ssd_scan",
    )(zs, xbc, xbc, xbc, dt_raw, dt_bias.astype(F32).reshape(1, n_heads), a_log.astype(F32).reshape(1, n_heads),
      dskip_w, g_norm.reshape(1, d_inner), jnp.asarray(expand2, BF16))


def _t5_bucket_table():
    q = np.arange(ATTN_BLOCK)[:, None]
    k = np.arange(2 * ATTN_BLOCK)[None, :]
    rel = np.maximum(q - k + ATTN_BLOCK, 0)
    max_exact = N_BUCKETS // 2
    relf = np.maximum(rel, 1).astype(np.float32)
    large = max_exact + (np.log(relf / np.float32(max_exact)) / np.float32(math.log(MAX_DISTANCE / max_exact))
                         * np.float32(N_BUCKETS - max_exact)).astype(np.int32)
    large = np.minimum(large, N_BUCKETS - 1)
    return np.where(rel < max_exact, rel, large).astype(np.int32)


def _attn_kernel(relb_ref, sink_ref, bucket_ref, q_ref, kp_ref, kc_ref, vp_ref, vc_ref, o_ref, bias_ref,
                 *, qpk, n_q_heads):
    gp = pl.program_id(0)
    n = pl.program_id(1)
    blk = ATTN_BLOCK
    npair = qpk // 2
    head0 = gp * 2 * qpk

    @pl.when(n == 0)
    def _():
        bucket = bucket_ref[...]
        qi = lax.broadcasted_iota(jnp.int32, (blk, 2 * blk), 0)
        ci = lax.broadcasted_iota(jnp.int32, (blk, 2 * blk), 1)
        rel = qi - ci + blk
        in_window = (rel >= 0) & (rel < WINDOW)

        def body(it, carry):
            kvh = it // npair
            pair = it - kvh * npair
            row0 = pl.multiple_of(pair * blk, blk)
            for hp in range(2):
                head = head0 + kvh * qpk + pair * 2 + hp
                acc = jnp.zeros((blk, 2 * blk), F32)
                for b in range(N_BUCKETS):
                    acc = jnp.where(bucket == b, relb_ref[b * n_q_heads + head], acc)
                bias_ref[kvh, pl.ds(row0, blk), hp * 2 * blk:(hp + 1) * 2 * blk] = jnp.where(in_window, acc, -jnp.inf)
            return carry

        lax.fori_loop(0, 2 * npair, body, 0)

    kk = jnp.concatenate([kp_ref[...], kc_ref[...]], axis=0)
    vv = jnp.concatenate([vp_ref[...], vc_ref[...]], axis=0)
    lane = lax.broadcasted_iota(jnp.int32, (2 * blk, LANES), 1)
    low = lane < ATTN_HEAD_DIM
    scale = ATTN_HEAD_DIM ** -0.5

    def pair_operand(t, kvh, mult):
        sel = jnp.where(low if kvh == 0 else jnp.logical_not(low), t, 0.0)
        dup = sel + pltpu.roll(sel, ATTN_HEAD_DIM, axis=1)
        if mult is not None:
            dup = dup * mult
        return jnp.concatenate([jnp.where(low, dup, 0.0), jnp.where(low, 0.0, dup)], axis=0).astype(BF16)

    row_bd = lax.broadcasted_iota(jnp.int32, (4 * blk, LANES), 0)
    lane_bd = lax.broadcasted_iota(jnp.int32, (4 * blk, LANES), 1)
    ones_bd = ((row_bd < 2 * blk) == (lane_bd < ATTN_HEAD_DIM)).astype(F32).astype(BF16)
    low_out = lax.broadcasted_iota(jnp.int32, (npair * blk, LANES), 1) < ATTN_HEAD_DIM

    def run(first_block):
        if first_block:
            ci = lax.broadcasted_iota(jnp.int32, (npair * blk, 2 * blk), 1)
            before_start = ci < blk
        for kvh in range(2):
            kbd = pair_operand(kk, kvh, scale)
            vbd = pair_operand(vv, kvh, None)
            base = kvh * npair
            qs = jnp.concatenate([q_ref[:, (base + i) * LANES:(base + i + 1) * LANES] for i in range(npair)], axis=0)
            s = lax.dot_general(qs, kbd, (((1,), (1,)), ((), ())), preferred_element_type=F32)
            es = []
            sink_terms = []
            for hp in range(2):
                sinks = [sink_ref[head0 + kvh * qpk + i * 2 + hp] for i in range(npair)]
                sink_col = jnp.concatenate([jnp.full((blk, 1), v, F32) for v in sinks], axis=0)
                sink_all = jnp.concatenate([jnp.full((blk, LANES), v, F32) for v in sinks], axis=0)
                sh = s[:, hp * 2 * blk:(hp + 1) * 2 * blk] + bias_ref[kvh, :, hp * 2 * blk:(hp + 1) * 2 * blk]
                if first_block:
                    sh = jnp.where(before_start, -jnp.inf, sh)
                m = jnp.maximum(jnp.max(sh, axis=-1, keepdims=True), sink_col)
                es.append(jnp.exp(sh - m).astype(BF16))
                sink_terms.append(jnp.exp(sink_all - jnp.broadcast_to(m, sink_all.shape)))
            e = jnp.concatenate(es, axis=1)
            ov = jnp.dot(e, jnp.concatenate([vbd, ones_bd], axis=1), preferred_element_type=F32)
            denom = ov[:, LANES:2 * LANES] + jnp.where(low_out, sink_terms[0], sink_terms[1])
            o = (ov[:, 0:LANES] * (1.0 / denom)).astype(o_ref.dtype)
            for i in range(npair):
                o_ref[:, (base + i) * LANES:(base + i + 1) * LANES] = o[i * blk:(i + 1) * blk, :]

    @pl.when(n == 0)
    def _():
        run(True)

    @pl.when(n != 0)
    def _():
        run(False)


def _attention(q, kv, sinks, rel_bias):
    s, qd = q.shape
    n_q_heads = qd // ATTN_HEAD_DIM
    qpk = n_q_heads // N_KV_HEADS
    assert qpk % 2 == 0 and N_KV_HEADS % 2 == 0 and 2 * ATTN_HEAD_DIM == LANES and s % ATTN_BLOCK == 0
    blk = ATTN_BLOCK
    nb = s // blk
    ngp = N_KV_HEADS // 2
    qw = 2 * qpk * ATTN_HEAD_DIM
    voff = N_KV_HEADS * ATTN_HEAD_DIM // LANES
    bucket = jnp.asarray(_t5_bucket_table())
    est = 2 * (2 * blk * qw * 2 + 4 * blk * LANES * 4 + blk * 2 * blk * 4) + 2 * qpk * blk * 2 * blk * 4 \
        + 24 * blk * 4 * blk * 4
    smem = pl.BlockSpec(memory_space=pltpu.SMEM)
    return pl.pallas_call(
        functools.partial(_attn_kernel, qpk=qpk, n_q_heads=n_q_heads),
        grid=(ngp, nb),
        in_specs=[
            smem, smem,
            pl.BlockSpec((blk, 2 * blk), lambda g, n: (0, 0)),
            pl.BlockSpec((blk, qw), lambda g, n: (n, g)),
            pl.BlockSpec((blk, LANES), lambda g, n: (jnp.maximum(n - 1, 0), g)),
            pl.BlockSpec((blk, LANES), lambda g, n: (n, g)),
            pl.BlockSpec((blk, LANES), lambda g, n: (jnp.maximum(n - 1, 0), voff + g)),
            pl.BlockSpec((blk, LANES), lambda g, n: (n, voff + g)),
        ],
        out_specs=pl.BlockSpec((blk, qw), lambda g, n: (n, g)),
        out_shape=jax.ShapeDtypeStruct((s, qd), BF16),
        scratch_shapes=[pltpu.VMEM((2, (qpk // 2) * blk, 4 * blk), F32)],
        compiler_params=_params(("parallel", "arbitrary"), est),
        name="swa_attention",
    )(rel_bias.astype(F32).reshape(-1), sinks.astype(F32).reshape(-1), bucket, q, kv, kv, kv, kv)


def kernel(x, norm_mix_pre, norm_mix_post, norm_ffn_pre, norm_ffn_post, ssm_w_in, ssm_conv_w, ssm_conv_b, ssm_dt_bias, ssm_a_log, ssm_d, ssm_norm, ssm_w_out, kv_norm, w_kv, b_kv, attn_w_q, attn_b_q, attn_sinks, attn_w_o, attn_b_o, rel_bias, ffn_w_up, ffn_conv_w, ffn_conv_b, ffn_w_down):
    bsz, s, d = x.shape
    assert bsz == 1 and norm_mix_pre.shape[0] == 2
    d_inner = ssm_norm.shape[-1]
    n_heads = ssm_dt_bias.shape[-1]
    zxbc_dim = ssm_w_in.shape[-1] - n_heads

    def ffn(h_in_u, layer):
        d_ff = ffn_w_up.shape[-1] // 2
        tn = _tile(d_ff, 256, LANES)
        hff = _conv_matmul(h_in_u, ffn_w_up, layer, d_ff, [0, d_ff // tn], ffn_conv_w[layer],
                           ffn_conv_b[layer].reshape(1, -1), tn, True, BF16, "ffn_up")
        return _matmul(hff, ffn_w_down, None, F32, 1024, 256, "ffn_down", a_buffers=1, layer=layer)

    h0 = x.reshape(s, d)

    u = _prenorm(h0, norm_mix_pre[0])
    conv_dim = zxbc_dim - d_inner
    tn_in = _tile(math.gcd(d_inner, conv_dim), 512, LANES)
    zs = _matmul(u, ssm_w_in, None, F32, 1024, 512, "in_proj_z", n=d_inner, layer=0, silu=True)
    xbc = _conv_matmul(u, ssm_w_in, 0, conv_dim, [d_inner // tn_in], ssm_conv_w[0].astype(F32),
                       ssm_conv_b[0].astype(F32).reshape(1, -1), tn_in, False, F32, "in_proj_xbc",
                       conv_col_blocks=[0])
    dt_raw = _matmul(u, ssm_w_in, None, F32, 1024, 128, "dt_proj", col0=zxbc_dim, n=n_heads, layer=0)
    y = _ssd(zs, xbc, dt_raw, ssm_dt_bias[0], ssm_a_log[0], ssm_d[0], ssm_norm[0], d_inner, n_heads)
    mix = _matmul(y, ssm_w_out, None, F32, 1024, 256, "out_proj", layer=0)
    h1, u = _resnorm(h0, mix, norm_mix_post[0], [norm_ffn_pre[0]])
    f = ffn(u, 0)

    h2, ukv, uq = _resnorm(h1, f, norm_ffn_post[0], [kv_norm, norm_mix_pre[1]])
    kv = _matmul(ukv, w_kv, b_kv, F32, 1024, 512, "kv_proj", cols_outer=True)
    q = _matmul(uq, attn_w_q, attn_b_q[0], BF16, 1024, 512, "q_proj", layer=0)
    o = _attention(q, kv, attn_sinks[0], rel_bias)
    mix = _matmul(o, attn_w_o, attn_b_o[0], F32, 1024, 512, "o_proj", layer=0)
    h3, u = _resnorm(h2, mix, norm_mix_post[1], [norm_ffn_pre[1]])
    f = ffn(u, 1)
    (h4,) = _resnorm(h3, f, norm_ffn_post[1], [])
    return h4.reshape(bsz, s, d)
```

```python
import functools
import math

import numpy as np
import jax
import jax.numpy as jnp
from jax import lax
from jax.experimental import pallas as pl
from jax.experimental.pallas import tpu as pltpu

EPS = 1e-6
SSM_NORM_EPS = 1e-5
SSM_HEAD_DIM = 64
SSM_GROUPS = 8
D_STATE = 128
SSM_CONV = 4
SSM_CHUNK = 128
SSD_PAIRS_PER_DOT = 2
ATTN_HEAD_DIM = 64
N_KV_HEADS = 8
WINDOW = 128
ATTN_BLOCK = 128
N_BUCKETS = 32
MAX_DISTANCE = 128
FFN_CONV = 3

LANES = 128
SUBLANES = 8
BF16_TILE_ROWS = 16
MXU_K_CHUNK = 256
VMEM_CAP_BYTES = 60 * 1024 * 1024

F32 = jnp.float32
BF16 = jnp.bfloat16


def _vmem_limit(est_bytes):
    return int(min(VMEM_CAP_BYTES, max(32 * 1024 * 1024, est_bytes * 5 // 4 + (4 << 20))))


def _params(semantics, est_bytes, flags=None):
    return pltpu.CompilerParams(dimension_semantics=semantics, vmem_limit_bytes=_vmem_limit(est_bytes), flags=flags)


def _tile(dim, pref, align):
    if dim <= pref:
        return dim
    t = (pref // align) * align
    while t >= align:
        if dim % t == 0:
            return t
        t -= align
    raise ValueError(f"no tile for {dim} (pref {pref}, align {align})")


def _sigmoid(x):
    return 0.5 * jnp.tanh(0.5 * x) + 0.5


def _softplus(x):
    return jnp.maximum(x, 0.0) + jnp.log1p(jnp.exp(-jnp.abs(x)))


def _split3(v):
    hi = v.astype(BF16)
    r1 = v - hi.astype(F32)
    mid = r1.astype(BF16)
    lo = (r1 - mid.astype(F32)).astype(BF16)
    return [hi, mid, lo]


def _rms(x, g, eps):
    ms = jnp.mean(x * x, axis=-1, keepdims=True)
    return (x * lax.rsqrt(ms + eps)) * g


def _prenorm_kernel(x_ref, g_ref, u_ref):
    u_ref[...] = _rms(x_ref[...], g_ref[...], EPS).astype(u_ref.dtype)


def _prenorm(x, g):
    s, d = x.shape
    tr = _tile(s, 256, SUBLANES)
    est = 2 * tr * d * (4 + 2)
    return pl.pallas_call(
        _prenorm_kernel,
        grid=(s // tr,),
        in_specs=[pl.BlockSpec((tr, d), lambda i: (i, 0)), pl.BlockSpec((1, d), lambda i: (0, 0))],
        out_specs=pl.BlockSpec((tr, d), lambda i: (i, 0)),
        out_shape=jax.ShapeDtypeStruct((s, d), BF16),
        compiler_params=_params(("parallel",), est),
        name="prenorm",
    )(x, g.reshape(1, d))


def _resnorm_kernel(r_ref, m_ref, gpost_ref, *rest, n_u):
    g_refs = rest[:n_u]
    h_ref = rest[n_u]
    u_refs = rest[n_u + 1:]
    h = r_ref[...] + _rms(m_ref[...], gpost_ref[...], EPS)
    h_ref[...] = h
    if n_u:
        ms = jnp.mean(h * h, axis=-1, keepdims=True)
        hn = h * lax.rsqrt(ms + EPS)
        for g_ref, u_ref in zip(g_refs, u_refs):
            u_ref[...] = (hn * g_ref[...]).astype(u_ref.dtype)


def _resnorm(r, m, gpost, gains, rows=None):
    s, d = r.shape
    n_u = len(gains)
    s_out = s
    s = s if rows is None else rows
    tr = _tile(s, 256, SUBLANES)
    est = 2 * tr * d * (4 * 3 + 2 * n_u)
    row = pl.BlockSpec((tr, d), lambda i: (i, 0))
    vec = pl.BlockSpec((1, d), lambda i: (0, 0))
    outs = pl.pallas_call(
        functools.partial(_resnorm_kernel, n_u=n_u),
        grid=(s // tr,),
        in_specs=[row, row, vec] + [vec] * n_u,
        out_specs=[row] + [row] * n_u,
        out_shape=[jax.ShapeDtypeStruct((s_out, d), F32)] + [jax.ShapeDtypeStruct((s, d), BF16)] * n_u,
        compiler_params=_params(("parallel",), est),
        name="resnorm",
    )(r, m, gpost.reshape(1, d), *[g.reshape(1, d) for g in gains])
    return outs


def _matmul_kernel(a_ref, w_ref, *rest, has_bias, silu):
    o_ref = rest[-1]
    acc = jnp.dot(a_ref[...], w_ref[...].astype(BF16), preferred_element_type=F32)
    if has_bias:
        acc = acc + rest[0][...]
    if silu:
        acc = acc * _sigmoid(acc)
    o_ref[...] = acc.astype(o_ref.dtype)


def _matmul(a, w, bias, out_dtype, tm_pref, tn_pref, name, a_buffers=2, col0=0, n=None, layer=None, silu=False,
            cols_outer=False):
    m, k = a.shape
    assert (w.ndim == 3) == (layer is not None)
    n = w.shape[-1] if n is None else n
    tm = _tile(m, tm_pref, SUBLANES)
    tn = _tile(math.gcd(n, col0) if col0 else n, tn_pref, LANES)
    assert n % tn == 0 and col0 % tn == 0
    joff = col0 // tn
    osz = jnp.dtype(out_dtype).itemsize
    wsz = jnp.dtype(w.dtype).itemsize
    est = a_buffers * tm * k * 2 + 2 * (k * tn * wsz + tm * tn * osz) + tm * tn * 4 + (k * tn * 2 if wsz != 2 else 0)
    a_mode = {} if a_buffers == 2 else {"pipeline_mode": pl.Buffered(a_buffers)}
    ij = (lambda p, q: (q, p)) if cols_outer else (lambda p, q: (p, q))
    if layer is None:
        w_spec = pl.BlockSpec((k, tn), lambda p, q: (0, ij(p, q)[1] + joff))
    else:
        w_spec = pl.BlockSpec((None, k, tn), lambda p, q: (layer, 0, ij(p, q)[1] + joff))
    in_specs = [pl.BlockSpec((tm, k), lambda p, q: (ij(p, q)[0], 0), **a_mode), w_spec]
    args = [a, w]
    if bias is not None:
        in_specs.append(pl.BlockSpec((1, tn), lambda p, q: (0, ij(p, q)[1])))
        args.append(bias.reshape(1, n).astype(F32))
    return pl.pallas_call(
        functools.partial(_matmul_kernel, has_bias=bias is not None, silu=silu),
        grid=(n // tn, m // tm) if cols_outer else (m // tm, n // tn),
        in_specs=in_specs,
        out_specs=pl.BlockSpec((tm, tn), lambda p, q: ij(p, q)),
        out_shape=jax.ShapeDtypeStruct((m, n), out_dtype),
        compiler_params=_params(("parallel", "parallel"), est),
        name=name,
    )(*args)


def _zero_after(x):
    w = pltpu.bitcast(x, jnp.uint32)
    zero = (w >> 16) >> 16
    return zero if zero.shape[0] == BF16_TILE_ROWS else jnp.concatenate([zero, zero], axis=0)


def _conv_mm_kernel(*refs, nj, n_w, taps, gate, n_tiles, norm_rows):
    if norm_rows:
        u_ref, r_ref, m_ref, gpost_ref, gpre_ref, _ = refs[0:6]
        refs = refs[6:]
    else:
        u_ref = refs[0]
        refs = refs[1:]
    w_refs = refs[0:n_w]
    cw_refs = refs[n_w:2 * n_w]
    cb_refs = refs[2 * n_w:3 * n_w]
    if norm_rows:
        o_ref, h_ref, halo, ext_a, ext_b, wcat, u_cur, u_next = refs[3 * n_w:]
    else:
        o_ref, halo, ext_a, ext_b, wcat = refs[3 * n_w:]
    t = pl.program_id(0)
    tm = u_ref.shape[0]
    d = u_ref.shape[1]
    tn = w_refs[0].shape[1]
    nch = d // MXU_K_CHUNK
    rows = tm // nch
    jp = lax.rem(jnp.maximum(t - 1, 0), nj)

    @pl.when(t == 0)
    def _():
        halo[...] = jnp.zeros(halo.shape, F32)
        ext_b[...] = jnp.zeros(ext_b.shape, F32)

    if norm_rows:
        tile = jnp.minimum(t, n_tiles - 1)
        jm = lax.rem(tile, nj)
        new_row_tile = (jm == 0) & (t < n_tiles)

        @pl.when(new_row_tile & (tile == 0))
        def _():
            u_cur[...] = u_ref[...]

        @pl.when(new_row_tile & (tile > 0))
        def _():
            u_cur[...] = u_next[...]

        lhs_ref = u_cur
    else:
        lhs_ref = u_ref

    def step(ext_mm, ext_ep):
        norm_zero = None
        if norm_rows:
            h = r_ref[...] + _rms(m_ref[...], gpost_ref[...], EPS)
            h_ref[...] = h
            u_new = _rms(h, gpre_ref[...], EPS).astype(BF16)
            row0 = pl.multiple_of(jnp.minimum(jm, tm // norm_rows - 1) * norm_rows, norm_rows)
            u_next[pl.ds(row0, norm_rows), :] = u_new
            norm_zero = _zero_after(u_new[0:BF16_TILE_ROWS, 0:tn])
        cw = jnp.concatenate([r[...] for r in cw_refs], axis=1)
        cb = jnp.concatenate([r[...] for r in cb_refs], axis=1)
        prev = halo[jp]
        for c in range(nch):
            cur = ext_ep[c * rows:(c + 1) * rows, :]
            both = jnp.concatenate([prev, cur], axis=0)
            acc = cb
            for k in range(taps - 1):
                r0 = SUBLANES - (taps - 1) + k
                acc = acc + both[r0:r0 + rows, :] * cw[k:k + 1, :]
            acc = acc + cur * cw[taps - 1:taps, :]
            if gate:
                out = (acc[:, 0:tn] * _sigmoid(acc[:, 0:tn])) * acc[:, tn:2 * tn]
            else:
                out = acc * _sigmoid(acc)
            out = out.astype(o_ref.dtype)
            o_ref[c * rows:(c + 1) * rows, :] = out
            prev = cur[rows - SUBLANES:rows, :]

            zero = _zero_after(out[0:BF16_TILE_ROWS, 0:tn])
            if norm_zero is not None and c == nch - 1:
                zero = zero | norm_zero
            k0 = c * MXU_K_CHUNK
            for idx, w_ref in enumerate(w_refs):
                top = pltpu.bitcast(pltpu.bitcast(w_ref[k0:k0 + BF16_TILE_ROWS, :], jnp.uint32) | zero, F32)
                wcat[k0:k0 + BF16_TILE_ROWS, idx * tn:(idx + 1) * tn] = top.astype(BF16)
                wcat[k0 + BF16_TILE_ROWS:k0 + MXU_K_CHUNK, idx * tn:(idx + 1) * tn] = (
                    w_ref[k0 + BF16_TILE_ROWS:k0 + MXU_K_CHUNK, :].astype(BF16))
        halo[jp] = prev
        ext_mm[...] = jnp.dot(lhs_ref[...], wcat[...], preferred_element_type=F32)

    @pl.when(lax.rem(t, 2) == 0)
    def _():
        step(ext_a, ext_b)

    @pl.when(lax.rem(t, 2) == 1)
    def _():
        step(ext_b, ext_a)


def _conv_matmul(u, w, layer, n_cols, col_blocks, conv_w, conv_b, tn, gate, out_dtype, name, conv_col_blocks=None,
                 norm=None):
    d = u.shape[1]
    s = u.shape[0] if norm is None else norm[0].shape[0]
    n_w = len(col_blocks)
    assert n_w == (2 if gate else 1) and d % MXU_K_CHUNK == 0
    taps = conv_w.shape[0]
    assert n_cols % tn == 0
    nj = n_cols // tn
    nch = d // MXU_K_CHUNK
    tm = _tile(s, 1024, nch * BF16_TILE_ROWS)
    wsz = jnp.dtype(w.dtype).itemsize
    osz = jnp.dtype(out_dtype).itemsize
    wt = n_w * tn
    wo = tn if gate else wt
    est = 2 * (tm * d * 2 + d * wt * wsz + tm * wo * osz) + d * wt * 2 + nj * SUBLANES * wt * 4 + 4 * tm * wt * 4
    n_tiles = (s // tm) * nj

    def mm_tile(t):
        return jnp.minimum(t, n_tiles - 1)

    def ep_tile(t):
        return jnp.maximum(t - 1, 0)

    if norm is None:
        norm_rows = 0
        lead_specs = [pl.BlockSpec((tm, d), lambda t: (mm_tile(t) // nj, 0))]
        lead_args = [u]
        extra_out_specs, extra_out_shapes, extra_scratch, aliases = [], [], [], {}
    else:
        r, m, gpost, gpre, h_head = norm
        assert u.shape == (tm, d)
        norm_rows = next(c for c in range(BF16_TILE_ROWS, tm + 1, BF16_TILE_ROWS) if tm % c == 0 and tm // c <= nj)
        per_tile = tm // norm_rows
        n_row_tiles = s // tm

        def norm_block(t):
            tile = mm_tile(t)
            return jnp.minimum(tile // nj + 1, n_row_tiles - 1) * per_tile + jnp.minimum(tile % nj, per_tile - 1)

        rows_spec = pl.BlockSpec((norm_rows, d), lambda t: (norm_block(t), 0))
        vec_spec = pl.BlockSpec((1, d), lambda t: (0, 0))
        lead_specs = [pl.BlockSpec((tm, d), lambda t: (0, 0), pipeline_mode=pl.Buffered(1)), rows_spec, rows_spec,
                      vec_spec, vec_spec, pl.BlockSpec(memory_space=pl.ANY)]
        lead_args = [u, r, m, gpost.reshape(1, d), gpre.reshape(1, d), h_head]
        extra_out_specs = [rows_spec]
        extra_out_shapes = [jax.ShapeDtypeStruct((s, d), F32)]
        extra_scratch = [pltpu.VMEM((tm, d), BF16), pltpu.VMEM((tm, d), BF16)]
        aliases = {5: 1}
        est += tm * d * 2 + 6 * norm_rows * d * 4
    w_specs = [pl.BlockSpec((None, d, tn), lambda t, o=o: (layer, 0, mm_tile(t) % nj + o)) for o in col_blocks]
    conv_col_blocks = col_blocks if conv_col_blocks is None else conv_col_blocks
    cw_specs = [pl.BlockSpec((taps, tn), lambda t, o=o: (0, ep_tile(t) % nj + o)) for o in conv_col_blocks]
    cb_specs = [pl.BlockSpec((1, tn), lambda t, o=o: (0, ep_tile(t) % nj + o)) for o in conv_col_blocks]
    outs = pl.pallas_call(
        functools.partial(_conv_mm_kernel, nj=nj, n_w=n_w, taps=taps, gate=gate, n_tiles=n_tiles,
                          norm_rows=norm_rows),
        grid=(n_tiles + 1,),
        in_specs=lead_specs + w_specs + cw_specs + cb_specs,
        out_specs=[pl.BlockSpec((tm, wo), lambda t: (ep_tile(t) // nj, ep_tile(t) % nj))] + extra_out_specs,
        out_shape=[jax.ShapeDtypeStruct((s, nj * wo), out_dtype)] + extra_out_shapes,
        scratch_shapes=[
            pltpu.VMEM((nj, SUBLANES, wt), F32),
            pltpu.VMEM((tm, wt), F32),
            pltpu.VMEM((tm, wt), F32),
            pltpu.VMEM((d, wt), BF16),
        ] + extra_scratch,
        input_output_aliases=aliases,
        compiler_params=_params(("arbitrary",), est),
        name=name,
    )(*lead_args, *([w] * n_w), *([conv_w] * n_w), *([conv_b] * n_w))
    return outs[0] if norm is None else outs


def _ssd_kernel(z_ref, x_ref, b_ref, c_ref, dt_ref, dtb_ref, alog_ref, dskip_ref, gnorm_ref, expand_ref,
                y_ref, state_ref, *, heads, groups):
    chunk = pl.program_id(0)

    @pl.when(chunk == 0)
    def _():
        state_ref[...] = jnp.zeros(state_ref.shape, F32)

    dtv_all = _softplus(dt_ref[...] + dtb_ref[...])
    a_all = dtv_all * (-jnp.exp(alog_ref[...]))
    a_all_t = jnp.transpose(a_all)
    n_all = dtv_all.shape[1]

    width = heads * SSM_HEAD_DIM
    for gi in range(groups):
        cols = slice(gi * width, (gi + 1) * width)
        ncols = slice(gi * D_STATE, (gi + 1) * D_STATE)
        shift = (n_all - gi * heads) % n_all
        dtv = pltpu.roll(dtv_all, shift, axis=1) if shift else dtv_all
        a = pltpu.roll(a_all, shift, axis=1) if shift else a_all
        y_ref[:, cols] = _ssd_group_chunk(
            z_ref[:, cols], x_ref[:, cols], b_ref[:, ncols], c_ref[:, ncols], dtv, a,
            a_all_t[gi * heads:(gi + 1) * heads, :], dskip_ref[:, cols], gnorm_ref[:, cols], expand_ref,
            state_ref.at[gi], heads).astype(y_ref.dtype)


def _ssd_group_chunk(zs, xs, bc, cc, dtv, a, a_t, dskip, gnorm, expand_ref, state_ref, heads):
    cl = SSM_CHUNK
    width = heads * SSM_HEAD_DIM

    li = lax.broadcasted_iota(jnp.int32, (cl, cl), 0)
    si = lax.broadcasted_iota(jnp.int32, (cl, cl), 1)
    tril = li >= si

    ones_lo = tril.astype(F32).astype(BF16)
    ones_up = (li <= si).astype(F32).astype(BF16)
    acum = jnp.dot(jnp.concatenate([ones_lo] * 3, axis=1), jnp.concatenate(_split3(a), axis=0),
                   preferred_element_type=F32)
    acum_t = jnp.dot(jnp.concatenate(_split3(a_t), axis=1), jnp.concatenate([ones_up] * 3, axis=0),
                     preferred_element_type=F32)
    alast = acum[cl - 1:cl, :]
    exp_acum = jnp.exp(acum)
    decay_end = jnp.exp(alast - acum)

    stacked = jnp.concatenate([dtv, exp_acum, decay_end], axis=0)
    wide = jnp.dot(jnp.concatenate(_split3(stacked)[0:2], axis=1), expand_ref[...], preferred_element_type=F32)
    dt_w = wide[0:cl, :]
    exp_acum_w = wide[cl:2 * cl, :]
    decay_end_w = wide[2 * cl:3 * cl, :]

    xdt = xs * dt_w
    state = state_ref[...]
    bc_t = jnp.transpose(bc).astype(BF16)
    c_prod = jnp.dot(cc.astype(BF16), jnp.concatenate([state.astype(BF16), bc_t], axis=1),
                     preferred_element_type=F32)
    y_off = c_prod[:, 0:width] * exp_acum_w
    cb = jnp.where(tril, c_prod[:, width:width + cl], 0.0)

    lane = lax.broadcasted_iota(jnp.int32, (cl, LANES), 1)
    low = lane < SSM_HEAD_DIM
    npairs = heads // 2
    ppd = min(SSD_PAIRS_PER_DOT, npairs)
    zero_blk = jnp.zeros((cl, LANES), BF16)
    y_diag = []
    for q in range(npairs // ppd):
        ms = []
        rhs_rows = []
        for pi in range(ppd):
            p = q * ppd + pi
            for hp in range(2):
                r = 2 * p + hp
                seg = acum[:, r:r + 1] - acum_t[r:r + 1, :]
                decay = jnp.exp(jnp.minimum(seg, 0.0))
                ms.append((cb * decay).astype(BF16))
            xp = xdt[:, p * LANES:(p + 1) * LANES]
            for blk in (jnp.where(low, xp, 0.0).astype(BF16), jnp.where(low, 0.0, xp).astype(BF16)):
                rhs_rows.append(jnp.concatenate([zero_blk] * pi + [blk] + [zero_blk] * (ppd - 1 - pi), axis=1))
        y_diag.append(jnp.dot(jnp.concatenate(ms, axis=1), jnp.concatenate(rhs_rows, axis=0),
                              preferred_element_type=F32))
    y = jnp.concatenate(y_diag, axis=1) + y_off + dskip * xs

    xw = (xdt * decay_end_w).astype(BF16)
    upd = jnp.dot(bc_t, xw, preferred_element_type=F32)
    state_ref[...] = state * exp_acum_w[cl - 1:cl, :] + upd

    yz = y * zs
    ms2 = jnp.mean(yz * yz, axis=-1, keepdims=True)
    return (yz * lax.rsqrt(ms2 + SSM_NORM_EPS)) * gnorm


def _ssd(zs, xbc, dt_raw, dt_bias, a_log, d_skip, g_norm, d_inner, n_heads):
    s = zs.shape[0]
    g = SSM_GROUPS
    heads = n_heads // g
    width = d_inner // g
    assert width == heads * SSM_HEAD_DIM and heads % 2 == 0 and heads <= LANES and width % LANES == 0
    assert (heads // 2) % min(SSD_PAIRS_PER_DOT, heads // 2) == 0
    assert D_STATE == LANES and s % SSM_CHUNK == 0
    dskip_w = jnp.repeat(d_skip.astype(F32), SSM_HEAD_DIM).reshape(1, d_inner)
    nc = s // SSM_CHUNK
    cl = SSM_CHUNK
    expand2 = np.tile(np.arange(n_heads)[:, None] == (np.arange(width)[None, :] // SSM_HEAD_DIM), (2, 1))

    gn = g * D_STATE
    assert d_inner % gn == 0
    est = 2 * (2 * cl * d_inner * 4 + 2 * cl * gn * 4 + cl * n_heads * 4 + cl * d_inner * 2) \
        + g * D_STATE * width * 4 + 24 * cl * width * 4
    return pl.pallas_call(
        functools.partial(_ssd_kernel, heads=heads, groups=g),
        grid=(nc,),
        in_specs=[
            pl.BlockSpec((cl, d_inner), lambda c: (c, 0)),
            pl.BlockSpec((cl, d_inner), lambda c: (c, 0)),
            pl.BlockSpec((cl, gn), lambda c: (c, d_inner // gn)),
            pl.BlockSpec((cl, gn), lambda c: (c, d_inner // gn + 1)),
            pl.BlockSpec((cl, n_heads), lambda c: (c, 0)),
            pl.BlockSpec((1, n_heads), lambda c: (0, 0)),
            pl.BlockSpec((1, n_heads), lambda c: (0, 0)),
            pl.BlockSpec((1, d_inner), lambda c: (0, 0)),
            pl.BlockSpec((1, d_inner), lambda c: (0, 0)),
            pl.BlockSpec((2 * n_heads, width), lambda c: (0, 0)),
        ],
        out_specs=pl.BlockSpec((cl, d_inner), lambda c: (c, 0)),
        out_shape=jax.ShapeDtypeStruct((s, d_inner), BF16),
        scratch_shapes=[
            pltpu.VMEM((g, D_STATE, width), F32),
        ],
        compiler_params=_params(("arbitrary",), est),
        name="ssd_scan",
    )(zs, xbc, xbc, xbc, dt_raw, dt_bias.astype(F32).reshape(1, n_heads), a_log.astype(F32).reshape(1, n_heads),
      dskip_w, g_norm.reshape(1, d_inner), jnp.asarray(expand2, BF16))


def _t5_bucket_table():
    q = np.arange(ATTN_BLOCK)[:, None]
    k = np.arange(2 * ATTN_BLOCK)[None, :]
    rel = np.maximum(q - k + ATTN_BLOCK, 0)
    max_exact = N_BUCKETS // 2
    relf = np.maximum(rel, 1).astype(np.float32)
    large = max_exact + (np.log(relf / np.float32(max_exact)) / np.float32(math.log(MAX_DISTANCE / max_exact))
                         * np.float32(N_BUCKETS - max_exact)).astype(np.int32)
    large = np.minimum(large, N_BUCKETS - 1)
    return np.where(rel < max_exact, rel, large).astype(np.int32)


def _attn_kernel(relb_ref, sink_ref, bucket_ref, q_ref, kp_ref, kc_ref, vp_ref, vc_ref, o_ref, bias_ref,
                 *, qpk, n_q_heads):
    gp = pl.program_id(0)
    n = pl.program_id(1)
    blk = ATTN_BLOCK
    npair = qpk // 2
    head0 = gp * 2 * qpk

    @pl.when(n == 0)
    def _():
        bucket = bucket_ref[...]
        qi = lax.broadcasted_iota(jnp.int32, (blk, 2 * blk), 0)
        ci = lax.broadcasted_iota(jnp.int32, (blk, 2 * blk), 1)
        rel = qi - ci + blk
        in_window = (rel >= 0) & (rel < WINDOW)

        def body(it, carry):
            kvh = it // npair
            pair = it - kvh * npair
            row0 = pl.multiple_of(pair * blk, blk)
            for hp in range(2):
                head = head0 + kvh * qpk + pair * 2 + hp
                acc = jnp.zeros((blk, 2 * blk), F32)
                for b in range(N_BUCKETS):
                    acc = jnp.where(bucket == b, relb_ref[b * n_q_heads + head], acc)
                bias_ref[kvh, pl.ds(row0, blk), hp * 2 * blk:(hp + 1) * 2 * blk] = jnp.where(in_window, acc, -jnp.inf)
            return carry

        lax.fori_loop(0, 2 * npair, body, 0)

    kk = jnp.concatenate([kp_ref[...], kc_ref[...]], axis=0)
    vv = jnp.concatenate([vp_ref[...], vc_ref[...]], axis=0)
    lane = lax.broadcasted_iota(jnp.int32, (2 * blk, LANES), 1)
    low = lane < ATTN_HEAD_DIM
    scale = ATTN_HEAD_DIM ** -0.5

    def pair_operand(t, kvh, mult):
        sel = jnp.where(low if kvh == 0 else jnp.logical_not(low), t, 0.0)
        dup = sel + pltpu.roll(sel, ATTN_HEAD_DIM, axis=1)
        if mult is not None:
            dup = dup * mult
        return jnp.concatenate([jnp.where(low, dup, 0.0), jnp.where(low, 0.0, dup)], axis=0).astype(BF16)

    row_bd = lax.broadcasted_iota(jnp.int32, (4 * blk, LANES), 0)
    lane_bd = lax.broadcasted_iota(jnp.int32, (4 * blk, LANES), 1)
    ones_bd = ((row_bd < 2 * blk) == (lane_bd < ATTN_HEAD_DIM)).astype(F32).astype(BF16)
    low_out = lax.broadcasted_iota(jnp.int32, (npair * blk, LANES), 1) < ATTN_HEAD_DIM

    def run(first_block):
        if first_block:
            ci = lax.broadcasted_iota(jnp.int32, (npair * blk, 2 * blk), 1)
            before_start = ci < blk
        for kvh in range(2):
            kbd = pair_operand(kk, kvh, scale)
            vbd = pair_operand(vv, kvh, None)
            base = kvh * npair
            qs = jnp.concatenate([q_ref[:, (base + i) * LANES:(base + i + 1) * LANES] for i in range(npair)], axis=0)
            s = lax.dot_general(qs, kbd, (((1,), (1,)), ((), ())), preferred_element_type=F32)
            es = []
            sink_terms = []
            for hp in range(2):
                sinks = [sink_ref[head0 + kvh * qpk + i * 2 + hp] for i in range(npair)]
                sink_col = jnp.concatenate([jnp.full((blk, 1), v, F32) for v in sinks], axis=0)
                sink_all = jnp.concatenate([jnp.full((blk, LANES), v, F32) for v in sinks], axis=0)
                sh = s[:, hp * 2 * blk:(hp + 1) * 2 * blk] + bias_ref[kvh, :, hp * 2 * blk:(hp + 1) * 2 * blk]
                if first_block:
                    sh = jnp.where(before_start, -jnp.inf, sh)
                m = jnp.maximum(jnp.max(sh, axis=-1, keepdims=True), sink_col)
                es.append(jnp.exp(sh - m).astype(BF16))
                sink_terms.append(jnp.exp(sink_all - jnp.broadcast_to(m, sink_all.shape)))
            e = jnp.concatenate(es, axis=1)
            ov = jnp.dot(e, jnp.concatenate([vbd, ones_bd], axis=1), preferred_element_type=F32)
            denom = ov[:, LANES:2 * LANES] + jnp.where(low_out, sink_terms[0], sink_terms[1])
            o = (ov[:, 0:LANES] * (1.0 / denom)).astype(o_ref.dtype)
            for i in range(npair):
                o_ref[:, (base + i) * LANES:(base + i + 1) * LANES] = o[i * blk:(i + 1) * blk, :]

    @pl.when(n == 0)
    def _():
        run(True)

    @pl.when(n != 0)
    def _():
        run(False)


def _attention(q, kv, sinks, rel_bias):
    s, qd = q.shape
    n_q_heads = qd // ATTN_HEAD_DIM
    qpk = n_q_heads // N_KV_HEADS
    assert qpk % 2 == 0 and N_KV_HEADS % 2 == 0 and 2 * ATTN_HEAD_DIM == LANES and s % ATTN_BLOCK == 0
    blk = ATTN_BLOCK
    nb = s // blk
    ngp = N_KV_HEADS // 2
    qw = 2 * qpk * ATTN_HEAD_DIM
    voff = N_KV_HEADS * ATTN_HEAD_DIM // LANES
    bucket = jnp.asarray(_t5_bucket_table())
    est = 2 * (2 * blk * qw * 2 + 4 * blk * LANES * 4 + blk * 2 * blk * 4) + 2 * qpk * blk * 2 * blk * 4 \
        + 24 * blk * 4 * blk * 4
    smem = pl.BlockSpec(memory_space=pltpu.SMEM)
    return pl.pallas_call(
        functools.partial(_attn_kernel, qpk=qpk, n_q_heads=n_q_heads),
        grid=(ngp, nb),
        in_specs=[
            smem, smem,
            pl.BlockSpec((blk, 2 * blk), lambda g, n: (0, 0)),
            pl.BlockSpec((blk, qw), lambda g, n: (n, g)),
            pl.BlockSpec((blk, LANES), lambda g, n: (jnp.maximum(n - 1, 0), g)),
            pl.BlockSpec((blk, LANES), lambda g, n: (n, g)),
            pl.BlockSpec((blk, LANES), lambda g, n: (jnp.maximum(n - 1, 0), voff + g)),
            pl.BlockSpec((blk, LANES), lambda g, n: (n, voff + g)),
        ],
        out_specs=pl.BlockSpec((blk, qw), lambda g, n: (n, g)),
        out_shape=jax.ShapeDtypeStruct((s, qd), BF16),
        scratch_shapes=[pltpu.VMEM((2, (qpk // 2) * blk, 4 * blk), F32)],
        compiler_params=_params(("parallel", "arbitrary"), est),
        name="swa_attention",
    )(rel_bias.astype(F32).reshape(-1), sinks.astype(F32).reshape(-1), bucket, q, kv, kv, kv, kv)


def kernel(x, norm_mix_pre, norm_mix_post, norm_ffn_pre, norm_ffn_post, ssm_w_in, ssm_conv_w, ssm_conv_b, ssm_dt_bias, ssm_a_log, ssm_d, ssm_norm, ssm_w_out, kv_norm, w_kv, b_kv, attn_w_q, attn_b_q, attn_sinks, attn_w_o, attn_b_o, rel_bias, ffn_w_up, ffn_conv_w, ffn_conv_b, ffn_w_down):
    bsz, s, d = x.shape
    assert bsz == 1 and norm_mix_pre.shape[0] == 2
    d_inner = ssm_norm.shape[-1]
    n_heads = ssm_dt_bias.shape[-1]
    zxbc_dim = ssm_w_in.shape[-1] - n_heads

    def mixer_out_ffn(resid, mix, layer):
        d_ff = ffn_w_up.shape[-1] // 2
        tn = _tile(d_ff, 256, LANES)
        tm = _tile(s, 1024, (d // MXU_K_CHUNK) * BF16_TILE_ROWS)
        h_head, u_head = _resnorm(resid, mix, norm_mix_post[layer], [norm_ffn_pre[layer]], rows=tm)
        hff, h = _conv_matmul(u_head, ffn_w_up, layer, d_ff, [0, d_ff // tn], ffn_conv_w[layer],
                              ffn_conv_b[layer].reshape(1, -1), tn, True, BF16, "ffn_up",
                              norm=(resid, mix, norm_mix_post[layer], norm_ffn_pre[layer], h_head))
        return h, _matmul(hff, ffn_w_down, None, F32, 1024, 256, "ffn_down", a_buffers=1, layer=layer)

    h0 = x.reshape(s, d)

    u = _prenorm(h0, norm_mix_pre[0])
    conv_dim = zxbc_dim - d_inner
    tn_in = _tile(math.gcd(d_inner, conv_dim), 512, LANES)
    zs = _matmul(u, ssm_w_in, None, F32, 1024, 512, "in_proj_z", n=d_inner, layer=0, silu=True)
    xbc = _conv_matmul(u, ssm_w_in, 0, conv_dim, [d_inner // tn_in], ssm_conv_w[0].astype(F32),
                       ssm_conv_b[0].astype(F32).reshape(1, -1), tn_in, False, F32, "in_proj_xbc",
                       conv_col_blocks=[0])
    dt_raw = _matmul(u, ssm_w_in, None, F32, 1024, 128, "dt_proj", col0=zxbc_dim, n=n_heads, layer=0)
    y = _ssd(zs, xbc, dt_raw, ssm_dt_bias[0], ssm_a_log[0], ssm_d[0], ssm_norm[0], d_inner, n_heads)
    mix = _matmul(y, ssm_w_out, None, F32, 1024, 256, "out_proj", layer=0)
    h1, f = mixer_out_ffn(h0, mix, 0)

    h2, ukv, uq = _resnorm(h1, f, norm_ffn_post[0], [kv_norm, norm_mix_pre[1]])
    kv = _matmul(ukv, w_kv, b_kv, F32, 1024, 512, "kv_proj", cols_outer=True)
    q = _matmul(uq, attn_w_q, attn_b_q[0], BF16, 1024, 512, "q_proj", layer=0)
    o = _attention(q, kv, attn_sinks[0], rel_bias)
    mix = _matmul(o, attn_w_o, attn_b_o[0], F32, 1024, 512, "o_proj", layer=0)
    h3, f = mixer_out_ffn(h2, mix, 1)
    (h4,) = _resnorm(h3, f, norm_ffn_post[1], [])
    return h4.reshape(bsz, s, d)
```

```python
import functools
import math

import numpy as np
import jax
import jax.numpy as jnp
from jax import lax
from jax.experimental import pallas as pl
from jax.experimental.pallas import tpu as pltpu

EPS = 1e-6
SSM_NORM_EPS = 1e-5
SSM_HEAD_DIM = 64
SSM_GROUPS = 8
D_STATE = 128
SSM_CONV = 4
SSM_CHUNK = 128
SSD_PAIRS_PER_DOT = 2
ATTN_HEAD_DIM = 64
N_KV_HEADS = 8
WINDOW = 128
ATTN_BLOCK = 128
N_BUCKETS = 32
MAX_DISTANCE = 128
FFN_CONV = 3

LANES = 128
SUBLANES = 8
BF16_TILE_ROWS = 16
MXU_K_CHUNK = 256
VMEM_CAP_BYTES = 60 * 1024 * 1024

F32 = jnp.float32
BF16 = jnp.bfloat16


def _vmem_limit(est_bytes):
    return int(min(VMEM_CAP_BYTES, max(32 * 1024 * 1024, est_bytes * 5 // 4 + (4 << 20))))


def _params(semantics, est_bytes, flags=None):
    return pltpu.CompilerParams(dimension_semantics=semantics, vmem_limit_bytes=_vmem_limit(est_bytes), flags=flags)


def _tile(dim, pref, align):
    if dim <= pref:
        return dim
    t = (pref // align) * align
    while t >= align:
        if dim % t == 0:
            return t
        t -= align
    raise ValueError(f"no tile for {dim} (pref {pref}, align {align})")


def _sigmoid(x):
    return 0.5 * jnp.tanh(0.5 * x) + 0.5


def _softplus(x):
    return jnp.maximum(x, 0.0) + jnp.log1p(jnp.exp(-jnp.abs(x)))


def _split3(v):
    hi = v.astype(BF16)
    r1 = v - hi.astype(F32)
    mid = r1.astype(BF16)
    lo = (r1 - mid.astype(F32)).astype(BF16)
    return [hi, mid, lo]


def _rms(x, g, eps):
    ms = jnp.mean(x * x, axis=-1, keepdims=True)
    return (x * lax.rsqrt(ms + eps)) * g


def _prenorm_kernel(x_ref, g_ref, u_ref):
    u_ref[...] = _rms(x_ref[...], g_ref[...], EPS).astype(u_ref.dtype)


def _prenorm(x, g):
    s, d = x.shape
    tr = _tile(s, 256, SUBLANES)
    est = 2 * tr * d * (4 + 2)
    return pl.pallas_call(
        _prenorm_kernel,
        grid=(s // tr,),
        in_specs=[pl.BlockSpec((tr, d), lambda i: (i, 0)), pl.BlockSpec((1, d), lambda i: (0, 0))],
        out_specs=pl.BlockSpec((tr, d), lambda i: (i, 0)),
        out_shape=jax.ShapeDtypeStruct((s, d), BF16),
        compiler_params=_params(("parallel",), est),
        name="prenorm",
    )(x, g.reshape(1, d))


def _resnorm_kernel(r_ref, m_ref, gpost_ref, *rest, n_u):
    g_refs = rest[:n_u]
    h_ref = rest[n_u]
    u_refs = rest[n_u + 1:]
    h = r_ref[...] + _rms(m_ref[...], gpost_ref[...], EPS)
    h_ref[...] = h
    if n_u:
        ms = jnp.mean(h * h, axis=-1, keepdims=True)
        hn = h * lax.rsqrt(ms + EPS)
        for g_ref, u_ref in zip(g_refs, u_refs):
            u_ref[...] = (hn * g_ref[...]).astype(u_ref.dtype)


def _resnorm(r, m, gpost, gains, rows=None):
    s, d = r.shape
    n_u = len(gains)
    s = s if rows is None else rows
    tr = _tile(s, 256, SUBLANES)
    est = 2 * tr * d * (4 * 3 + 2 * n_u)
    row = pl.BlockSpec((tr, d), lambda i: (i, 0))
    vec = pl.BlockSpec((1, d), lambda i: (0, 0))
    outs = pl.pallas_call(
        functools.partial(_resnorm_kernel, n_u=n_u),
        grid=(s // tr,),
        in_specs=[row, row, vec] + [vec] * n_u,
        out_specs=[row] + [row] * n_u,
        out_shape=[jax.ShapeDtypeStruct((s, d), F32)] + [jax.ShapeDtypeStruct((s, d), BF16)] * n_u,
        compiler_params=_params(("parallel",), est),
        name="resnorm",
    )(r, m, gpost.reshape(1, d), *[g.reshape(1, d) for g in gains])
    return outs


def _matmul_kernel(a_ref, w_ref, *rest, has_bias, silu):
    o_ref = rest[-1]
    acc = jnp.dot(a_ref[...], w_ref[...].astype(BF16), preferred_element_type=F32)
    if has_bias:
        acc = acc + rest[0][...]
    if silu:
        acc = acc * _sigmoid(acc)
    o_ref[...] = acc.astype(o_ref.dtype)


def _matmul(a, w, bias, out_dtype, tm_pref, tn_pref, name, a_buffers=2, col0=0, n=None, layer=None, silu=False,
            cols_outer=False):
    m, k = a.shape
    assert (w.ndim == 3) == (layer is not None)
    n = w.shape[-1] if n is None else n
    tm = _tile(m, tm_pref, SUBLANES)
    tn = _tile(math.gcd(n, col0) if col0 else n, tn_pref, LANES)
    assert n % tn == 0 and col0 % tn == 0
    joff = col0 // tn
    osz = jnp.dtype(out_dtype).itemsize
    wsz = jnp.dtype(w.dtype).itemsize
    est = a_buffers * tm * k * 2 + 2 * (k * tn * wsz + tm * tn * osz) + tm * tn * 4 + (k * tn * 2 if wsz != 2 else 0)
    a_mode = {} if a_buffers == 2 else {"pipeline_mode": pl.Buffered(a_buffers)}
    ij = (lambda p, q: (q, p)) if cols_outer else (lambda p, q: (p, q))
    if layer is None:
        w_spec = pl.BlockSpec((k, tn), lambda p, q: (0, ij(p, q)[1] + joff))
    else:
        w_spec = pl.BlockSpec((None, k, tn), lambda p, q: (layer, 0, ij(p, q)[1] + joff))
    in_specs = [pl.BlockSpec((tm, k), lambda p, q: (ij(p, q)[0], 0), **a_mode), w_spec]
    args = [a, w]
    if bias is not None:
        in_specs.append(pl.BlockSpec((1, tn), lambda p, q: (0, ij(p, q)[1])))
        args.append(bias.reshape(1, n).astype(F32))
    return pl.pallas_call(
        functools.partial(_matmul_kernel, has_bias=bias is not None, silu=silu),
        grid=(n // tn, m // tm) if cols_outer else (m // tm, n // tn),
        in_specs=in_specs,
        out_specs=pl.BlockSpec((tm, tn), lambda p, q: ij(p, q)),
        out_shape=jax.ShapeDtypeStruct((m, n), out_dtype),
        compiler_params=_params(("parallel", "parallel"), est),
        name=name,
    )(*args)


def _zero_after(x):
    w = pltpu.bitcast(x, jnp.uint32)
    zero = (w >> 16) >> 16
    return zero if zero.shape[0] == BF16_TILE_ROWS else jnp.concatenate([zero, zero], axis=0)


def _conv_mm_kernel(*refs, nj, n_w, taps, gate, n_tiles, norm_rows):
    if norm_rows:
        u_ref, r_ref, m_ref, gpost_ref, gpre_ref, hhead_ref = refs[0:6]
        refs = refs[6:]
    else:
        u_ref = refs[0]
        refs = refs[1:]
    w_refs = refs[0:n_w]
    cw_refs = refs[n_w:2 * n_w]
    cb_refs = refs[2 * n_w:3 * n_w]
    if norm_rows:
        o_ref, h_ref, halo, ext_a, ext_b, wcat, u_cur, u_next = refs[3 * n_w:]
    else:
        o_ref, halo, ext_a, ext_b, wcat = refs[3 * n_w:]
    t = pl.program_id(0)
    tm = u_ref.shape[0]
    d = u_ref.shape[1]
    tn = w_refs[0].shape[1]
    nch = d // MXU_K_CHUNK
    rows = tm // nch
    jp = lax.rem(jnp.maximum(t - 1, 0), nj)

    @pl.when(t == 0)
    def _():
        halo[...] = jnp.zeros(halo.shape, F32)
        ext_b[...] = jnp.zeros(ext_b.shape, F32)

    if norm_rows:
        tile = jnp.minimum(t, n_tiles - 1)
        jm = lax.rem(tile, nj)
        new_row_tile = (jm == 0) & (t < n_tiles)

        @pl.when(new_row_tile & (tile == 0))
        def _():
            u_cur[...] = u_ref[...]

        @pl.when(new_row_tile & (tile > 0))
        def _():
            u_cur[...] = u_next[...]

        lhs_ref = u_cur
    else:
        lhs_ref = u_ref

    def step(ext_mm, ext_ep):
        norm_zero = None
        if norm_rows:
            h = r_ref[...] + _rms(m_ref[...], gpost_ref[...], EPS)
            last_row_tile = tile // nj + 1 >= n_tiles // nj
            h_ref[...] = jnp.where(last_row_tile, hhead_ref[...], h)
            u_new = _rms(h, gpre_ref[...], EPS).astype(BF16)
            row0 = pl.multiple_of(jnp.minimum(jm, tm // norm_rows - 1) * norm_rows, norm_rows)
            u_next[pl.ds(row0, norm_rows), :] = u_new
            norm_zero = _zero_after(u_new[0:BF16_TILE_ROWS, 0:tn])
        cw = jnp.concatenate([r[...] for r in cw_refs], axis=1)
        cb = jnp.concatenate([r[...] for r in cb_refs], axis=1)
        prev = halo[jp]
        for c in range(nch):
            cur = ext_ep[c * rows:(c + 1) * rows, :]
            both = jnp.concatenate([prev, cur], axis=0)
            acc = cb
            for k in range(taps - 1):
                r0 = SUBLANES - (taps - 1) + k
                acc = acc + both[r0:r0 + rows, :] * cw[k:k + 1, :]
            acc = acc + cur * cw[taps - 1:taps, :]
            if gate:
                out = (acc[:, 0:tn] * _sigmoid(acc[:, 0:tn])) * acc[:, tn:2 * tn]
            else:
                out = acc * _sigmoid(acc)
            out = out.astype(o_ref.dtype)
            o_ref[c * rows:(c + 1) * rows, :] = out
            prev = cur[rows - SUBLANES:rows, :]

            zero = _zero_after(out[0:BF16_TILE_ROWS, 0:tn])
            if norm_zero is not None and c == nch - 1:
                zero = zero | norm_zero
            k0 = c * MXU_K_CHUNK
            for idx, w_ref in enumerate(w_refs):
                top = pltpu.bitcast(pltpu.bitcast(w_ref[k0:k0 + BF16_TILE_ROWS, :], jnp.uint32) | zero, F32)
                wcat[k0:k0 + BF16_TILE_ROWS, idx * tn:(idx + 1) * tn] = top.astype(BF16)
                wcat[k0 + BF16_TILE_ROWS:k0 + MXU_K_CHUNK, idx * tn:(idx + 1) * tn] = (
                    w_ref[k0 + BF16_TILE_ROWS:k0 + MXU_K_CHUNK, :].astype(BF16))
        halo[jp] = prev
        ext_mm[...] = jnp.dot(lhs_ref[...], wcat[...], preferred_element_type=F32)

    @pl.when(lax.rem(t, 2) == 0)
    def _():
        step(ext_a, ext_b)

    @pl.when(lax.rem(t, 2) == 1)
    def _():
        step(ext_b, ext_a)


def _conv_matmul(u, w, layer, n_cols, col_blocks, conv_w, conv_b, tn, gate, out_dtype, name, conv_col_blocks=None,
                 norm=None):
    d = u.shape[1]
    s = u.shape[0] if norm is None else norm[0].shape[0]
    n_w = len(col_blocks)
    assert n_w == (2 if gate else 1) and d % MXU_K_CHUNK == 0
    taps = conv_w.shape[0]
    assert n_cols % tn == 0
    nj = n_cols // tn
    nch = d // MXU_K_CHUNK
    tm = _tile(s, 1024, nch * BF16_TILE_ROWS)
    wsz = jnp.dtype(w.dtype).itemsize
    osz = jnp.dtype(out_dtype).itemsize
    wt = n_w * tn
    wo = tn if gate else wt
    est = 2 * (tm * d * 2 + d * wt * wsz + tm * wo * osz) + d * wt * 2 + nj * SUBLANES * wt * 4 + 4 * tm * wt * 4
    n_tiles = (s // tm) * nj

    def mm_tile(t):
        return jnp.minimum(t, n_tiles - 1)

    def ep_tile(t):
        return jnp.maximum(t - 1, 0)

    if norm is None:
        norm_rows = 0
        lead_specs = [pl.BlockSpec((tm, d), lambda t: (mm_tile(t) // nj, 0))]
        lead_args = [u]
        extra_out_specs, extra_out_shapes, extra_scratch = [], [], []
    else:
        r, m, gpost, gpre, h_head = norm
        assert u.shape == (tm, d)
        norm_rows = next(c for c in range(BF16_TILE_ROWS, tm + 1, BF16_TILE_ROWS) if tm % c == 0 and tm // c <= nj)
        per_tile = tm // norm_rows
        n_row_tiles = s // tm

        def head_block(t):
            return jnp.minimum(mm_tile(t) % nj, per_tile - 1)

        def norm_block(t):
            ahead = mm_tile(t) // nj + 1
            return jnp.where(ahead < n_row_tiles, ahead, 0) * per_tile + head_block(t)

        rows_spec = pl.BlockSpec((norm_rows, d), lambda t: (norm_block(t), 0))
        vec_spec = pl.BlockSpec((1, d), lambda t: (0, 0))
        lead_specs = [pl.BlockSpec((tm, d), lambda t: (0, 0), pipeline_mode=pl.Buffered(1)), rows_spec, rows_spec,
                      vec_spec, vec_spec, pl.BlockSpec((norm_rows, d), lambda t: (head_block(t), 0))]
        lead_args = [u, r, m, gpost.reshape(1, d), gpre.reshape(1, d), h_head]
        extra_out_specs = [rows_spec]
        extra_out_shapes = [jax.ShapeDtypeStruct((s, d), F32)]
        extra_scratch = [pltpu.VMEM((tm, d), BF16), pltpu.VMEM((tm, d), BF16)]
        est += tm * d * 2 + 8 * norm_rows * d * 4
    w_specs = [pl.BlockSpec((None, d, tn), lambda t, o=o: (layer, 0, mm_tile(t) % nj + o)) for o in col_blocks]
    conv_col_blocks = col_blocks if conv_col_blocks is None else conv_col_blocks
    cw_specs = [pl.BlockSpec((taps, tn), lambda t, o=o: (0, ep_tile(t) % nj + o)) for o in conv_col_blocks]
    cb_specs = [pl.BlockSpec((1, tn), lambda t, o=o: (0, ep_tile(t) % nj + o)) for o in conv_col_blocks]
    outs = pl.pallas_call(
        functools.partial(_conv_mm_kernel, nj=nj, n_w=n_w, taps=taps, gate=gate, n_tiles=n_tiles,
                          norm_rows=norm_rows),
        grid=(n_tiles + 1,),
        in_specs=lead_specs + w_specs + cw_specs + cb_specs,
        out_specs=[pl.BlockSpec((tm, wo), lambda t: (ep_tile(t) // nj, ep_tile(t) % nj))] + extra_out_specs,
        out_shape=[jax.ShapeDtypeStruct((s, nj * wo), out_dtype)] + extra_out_shapes,
        scratch_shapes=[
            pltpu.VMEM((nj, SUBLANES, wt), F32),
            pltpu.VMEM((tm, wt), F32),
            pltpu.VMEM((tm, wt), F32),
            pltpu.VMEM((d, wt), BF16),
        ] + extra_scratch,
        compiler_params=_params(("arbitrary",), est),
        name=name,
    )(*lead_args, *([w] * n_w), *([conv_w] * n_w), *([conv_b] * n_w))
    return outs[0] if norm is None else outs


def _ssd_kernel(z_ref, x_ref, b_ref, c_ref, dt_ref, dtb_ref, alog_ref, dskip_ref, gnorm_ref, expand_ref,
                y_ref, state_ref, *, heads, groups):
    chunk = pl.program_id(0)

    @pl.when(chunk == 0)
    def _():
        state_ref[...] = jnp.zeros(state_ref.shape, F32)

    dtv_all = _softplus(dt_ref[...] + dtb_ref[...])
    a_all = dtv_all * (-jnp.exp(alog_ref[...]))
    a_all_t = jnp.transpose(a_all)
    n_all = dtv_all.shape[1]

    width = heads * SSM_HEAD_DIM
    for gi in range(groups):
        cols = slice(gi * width, (gi + 1) * width)
        ncols = slice(gi * D_STATE, (gi + 1) * D_STATE)
        shift = (n_all - gi * heads) % n_all
        dtv = pltpu.roll(dtv_all, shift, axis=1) if shift else dtv_all
        a = pltpu.roll(a_all, shift, axis=1) if shift else a_all
        y_ref[:, cols] = _ssd_group_chunk(
            z_ref[:, cols], x_ref[:, cols], b_ref[:, ncols], c_ref[:, ncols], dtv, a,
            a_all_t[gi * heads:(gi + 1) * heads, :], dskip_ref[:, cols], gnorm_ref[:, cols], expand_ref,
            state_ref.at[gi], heads).astype(y_ref.dtype)


def _ssd_group_chunk(zs, xs, bc, cc, dtv, a, a_t, dskip, gnorm, expand_ref, state_ref, heads):
    cl = SSM_CHUNK
    width = heads * SSM_HEAD_DIM

    li = lax.broadcasted_iota(jnp.int32, (cl, cl), 0)
    si = lax.broadcasted_iota(jnp.int32, (cl, cl), 1)
    tril = li >= si

    ones_lo = tril.astype(F32).astype(BF16)
    ones_up = (li <= si).astype(F32).astype(BF16)
    acum = jnp.dot(jnp.concatenate([ones_lo] * 3, axis=1), jnp.concatenate(_split3(a), axis=0),
                   preferred_element_type=F32)
    acum_t = jnp.dot(jnp.concatenate(_split3(a_t), axis=1), jnp.concatenate([ones_up] * 3, axis=0),
                     preferred_element_type=F32)
    alast = acum[cl - 1:cl, :]
    exp_acum = jnp.exp(acum)
    decay_end = jnp.exp(alast - acum)

    stacked = jnp.concatenate([dtv, exp_acum, decay_end], axis=0)
    wide = jnp.dot(jnp.concatenate(_split3(stacked)[0:2], axis=1), expand_ref[...], preferred_element_type=F32)
    dt_w = wide[0:cl, :]
    exp_acum_w = wide[cl:2 * cl, :]
    decay_end_w = wide[2 * cl:3 * cl, :]

    xdt = xs * dt_w
    state = state_ref[...]
    bc_t = jnp.transpose(bc).astype(BF16)
    c_prod = jnp.dot(cc.astype(BF16), jnp.concatenate([state.astype(BF16), bc_t], axis=1),
                     preferred_element_type=F32)
    y_off = c_prod[:, 0:width] * exp_acum_w
    cb = jnp.where(tril, c_prod[:, width:width + cl], 0.0)

    lane = lax.broadcasted_iota(jnp.int32, (cl, LANES), 1)
    low = lane < SSM_HEAD_DIM
    npairs = heads // 2
    ppd = min(SSD_PAIRS_PER_DOT, npairs)
    zero_blk = jnp.zeros((cl, LANES), BF16)
    y_diag = []
    for q in range(npairs // ppd):
        ms = []
        rhs_rows = []
        for pi in range(ppd):
            p = q * ppd + pi
            for hp in range(2):
                r = 2 * p + hp
                seg = acum[:, r:r + 1] - acum_t[r:r + 1, :]
                decay = jnp.exp(jnp.minimum(seg, 0.0))
                ms.append((cb * decay).astype(BF16))
            xp = xdt[:, p * LANES:(p + 1) * LANES]
            for blk in (jnp.where(low, xp, 0.0).astype(BF16), jnp.where(low, 0.0, xp).astype(BF16)):
                rhs_rows.append(jnp.concatenate([zero_blk] * pi + [blk] + [zero_blk] * (ppd - 1 - pi), axis=1))
        y_diag.append(jnp.dot(jnp.concatenate(ms, axis=1), jnp.concatenate(rhs_rows, axis=0),
                              preferred_element_type=F32))
    y = jnp.concatenate(y_diag, axis=1) + y_off + dskip * xs

    xw = (xdt * decay_end_w).astype(BF16)
    upd = jnp.dot(bc_t, xw, preferred_element_type=F32)
    state_ref[...] = state * exp_acum_w[cl - 1:cl, :] + upd

    yz = y * zs
    ms2 = jnp.mean(yz * yz, axis=-1, keepdims=True)
    return (yz * lax.rsqrt(ms2 + SSM_NORM_EPS)) * gnorm


def _ssd(zs, xbc, dt_raw, dt_bias, a_log, d_skip, g_norm, d_inner, n_heads):
    s = zs.shape[0]
    g = SSM_GROUPS
    heads = n_heads // g
    width = d_inner // g
    assert width == heads * SSM_HEAD_DIM and heads % 2 == 0 and heads <= LANES and width % LANES == 0
    assert (heads // 2) % min(SSD_PAIRS_PER_DOT, heads // 2) == 0
    assert D_STATE == LANES and s % SSM_CHUNK == 0
    dskip_w = jnp.repeat(d_skip.astype(F32), SSM_HEAD_DIM).reshape(1, d_inner)
    nc = s // SSM_CHUNK
    cl = SSM_CHUNK
    expand2 = np.tile(np.arange(n_heads)[:, None] == (np.arange(width)[None, :] // SSM_HEAD_DIM), (2, 1))

    gn = g * D_STATE
    assert d_inner % gn == 0
    est = 2 * (2 * cl * d_inner * 4 + 2 * cl * gn * 4 + cl * n_heads * 4 + cl * d_inner * 2) \
        + g * D_STATE * width * 4 + 24 * cl * width * 4
    return pl.pallas_call(
        functools.partial(_ssd_kernel, heads=heads, groups=g),
        grid=(nc,),
        in_specs=[
            pl.BlockSpec((cl, d_inner), lambda c: (c, 0)),
            pl.BlockSpec((cl, d_inner), lambda c: (c, 0)),
            pl.BlockSpec((cl, gn), lambda c: (c, d_inner // gn)),
            pl.BlockSpec((cl, gn), lambda c: (c, d_inner // gn + 1)),
            pl.BlockSpec((cl, n_heads), lambda c: (c, 0)),
            pl.BlockSpec((1, n_heads), lambda c: (0, 0)),
            pl.BlockSpec((1, n_heads), lambda c: (0, 0)),
            pl.BlockSpec((1, d_inner), lambda c: (0, 0)),
            pl.BlockSpec((1, d_inner), lambda c: (0, 0)),
            pl.BlockSpec((2 * n_heads, width), lambda c: (0, 0)),
        ],
        out_specs=pl.BlockSpec((cl, d_inner), lambda c: (c, 0)),
        out_shape=jax.ShapeDtypeStruct((s, d_inner), BF16),
        scratch_shapes=[
            pltpu.VMEM((g, D_STATE, width), F32),
        ],
        compiler_params=_params(("arbitrary",), est),
        name="ssd_scan",
    )(zs, xbc, xbc, xbc, dt_raw, dt_bias.astype(F32).reshape(1, n_heads), a_log.astype(F32).reshape(1, n_heads),
      dskip_w, g_norm.reshape(1, d_inner), jnp.asarray(expand2, BF16))


def _t5_bucket_table():
    q = np.arange(ATTN_BLOCK)[:, None]
    k = np.arange(2 * ATTN_BLOCK)[None, :]
    rel = np.maximum(q - k + ATTN_BLOCK, 0)
    max_exact = N_BUCKETS // 2
    relf = np.maximum(rel, 1).astype(np.float32)
    large = max_exact + (np.log(relf / np.float32(max_exact)) / np.float32(math.log(MAX_DISTANCE / max_exact))
                         * np.float32(N_BUCKETS - max_exact)).astype(np.int32)
    large = np.minimum(large, N_BUCKETS - 1)
    return np.where(rel < max_exact, rel, large).astype(np.int32)


def _attn_kernel(relb_ref, sink_ref, bucket_ref, q_ref, kp_ref, kc_ref, vp_ref, vc_ref, o_ref, bias_ref,
                 *, qpk, n_q_heads):
    gp = pl.program_id(0)
    n = pl.program_id(1)
    blk = ATTN_BLOCK
    npair = qpk // 2
    head0 = gp * 2 * qpk

    @pl.when(n == 0)
    def _():
        bucket = bucket_ref[...]
        qi = lax.broadcasted_iota(jnp.int32, (blk, 2 * blk), 0)
        ci = lax.broadcasted_iota(jnp.int32, (blk, 2 * blk), 1)
        rel = qi - ci + blk
        in_window = (rel >= 0) & (rel < WINDOW)

        def body(it, carry):
            kvh = it // npair
            pair = it - kvh * npair
            row0 = pl.multiple_of(pair * blk, blk)
            for hp in range(2):
                head = head0 + kvh * qpk + pair * 2 + hp
                acc = jnp.zeros((blk, 2 * blk), F32)
                for b in range(N_BUCKETS):
                    acc = jnp.where(bucket == b, relb_ref[b * n_q_heads + head], acc)
                bias_ref[kvh, pl.ds(row0, blk), hp * 2 * blk:(hp + 1) * 2 * blk] = jnp.where(in_window, acc, -jnp.inf)
            return carry

        lax.fori_loop(0, 2 * npair, body, 0)

    kk = jnp.concatenate([kp_ref[...], kc_ref[...]], axis=0)
    vv = jnp.concatenate([vp_ref[...], vc_ref[...]], axis=0)
    lane = lax.broadcasted_iota(jnp.int32, (2 * blk, LANES), 1)
    low = lane < ATTN_HEAD_DIM
    scale = ATTN_HEAD_DIM ** -0.5

    def pair_operand(t, kvh, mult):
        sel = jnp.where(low if kvh == 0 else jnp.logical_not(low), t, 0.0)
        dup = sel + pltpu.roll(sel, ATTN_HEAD_DIM, axis=1)
        if mult is not None:
            dup = dup * mult
        return jnp.concatenate([jnp.where(low, dup, 0.0), jnp.where(low, 0.0, dup)], axis=0).astype(BF16)

    row_bd = lax.broadcasted_iota(jnp.int32, (4 * blk, LANES), 0)
    lane_bd = lax.broadcasted_iota(jnp.int32, (4 * blk, LANES), 1)
    ones_bd = ((row_bd < 2 * blk) == (lane_bd < ATTN_HEAD_DIM)).astype(F32).astype(BF16)
    low_out = lax.broadcasted_iota(jnp.int32, (npair * blk, LANES), 1) < ATTN_HEAD_DIM

    def run(first_block):
        if first_block:
            ci = lax.broadcasted_iota(jnp.int32, (npair * blk, 2 * blk), 1)
            before_start = ci < blk
        for kvh in range(2):
            kbd = pair_operand(kk, kvh, scale)
            vbd = pair_operand(vv, kvh, None)
            base = kvh * npair
            qs = jnp.concatenate([q_ref[:, (base + i) * LANES:(base + i + 1) * LANES] for i in range(npair)], axis=0)
            s = lax.dot_general(qs, kbd, (((1,), (1,)), ((), ())), preferred_element_type=F32)
            es = []
            sink_terms = []
            for hp in range(2):
                sinks = [sink_ref[head0 + kvh * qpk + i * 2 + hp] for i in range(npair)]
                sink_col = jnp.concatenate([jnp.full((blk, 1), v, F32) for v in sinks], axis=0)
                sink_all = jnp.concatenate([jnp.full((blk, LANES), v, F32) for v in sinks], axis=0)
                sh = s[:, hp * 2 * blk:(hp + 1) * 2 * blk] + bias_ref[kvh, :, hp * 2 * blk:(hp + 1) * 2 * blk]
                if first_block:
                    sh = jnp.where(before_start, -jnp.inf, sh)
                m = jnp.maximum(jnp.max(sh, axis=-1, keepdims=True), sink_col)
                es.append(jnp.exp(sh - m).astype(BF16))
                sink_terms.append(jnp.exp(sink_all - jnp.broadcast_to(m, sink_all.shape)))
            e = jnp.concatenate(es, axis=1)
            ov = jnp.dot(e, jnp.concatenate([vbd, ones_bd], axis=1), preferred_element_type=F32)
            denom = ov[:, LANES:2 * LANES] + jnp.where(low_out, sink_terms[0], sink_terms[1])
            o = (ov[:, 0:LANES] * (1.0 / denom)).astype(o_ref.dtype)
            for i in range(npair):
                o_ref[:, (base + i) * LANES:(base + i + 1) * LANES] = o[i * blk:(i + 1) * blk, :]

    @pl.when(n == 0)
    def _():
        run(True)

    @pl.when(n != 0)
    def _():
        run(False)


def _attention(q, kv, sinks, rel_bias):
    s, qd = q.shape
    n_q_heads = qd // ATTN_HEAD_DIM
    qpk = n_q_heads // N_KV_HEADS
    assert qpk % 2 == 0 and N_KV_HEADS % 2 == 0 and 2 * ATTN_HEAD_DIM == LANES and s % ATTN_BLOCK == 0
    blk = ATTN_BLOCK
    nb = s // blk
    ngp = N_KV_HEADS // 2
    qw = 2 * qpk * ATTN_HEAD_DIM
    voff = N_KV_HEADS * ATTN_HEAD_DIM // LANES
    bucket = jnp.asarray(_t5_bucket_table())
    est = 2 * (2 * blk * qw * 2 + 4 * blk * LANES * 4 + blk * 2 * blk * 4) + 2 * qpk * blk * 2 * blk * 4 \
        + 24 * blk * 4 * blk * 4
    smem = pl.BlockSpec(memory_space=pltpu.SMEM)
    return pl.pallas_call(
        functools.partial(_attn_kernel, qpk=qpk, n_q_heads=n_q_heads),
        grid=(ngp, nb),
        in_specs=[
            smem, smem,
            pl.BlockSpec((blk, 2 * blk), lambda g, n: (0, 0)),
            pl.BlockSpec((blk, qw), lambda g, n: (n, g)),
            pl.BlockSpec((blk, LANES), lambda g, n: (jnp.maximum(n - 1, 0), g)),
            pl.BlockSpec((blk, LANES), lambda g, n: (n, g)),
            pl.BlockSpec((blk, LANES), lambda g, n: (jnp.maximum(n - 1, 0), voff + g)),
            pl.BlockSpec((blk, LANES), lambda g, n: (n, voff + g)),
        ],
        out_specs=pl.BlockSpec((blk, qw), lambda g, n: (n, g)),
        out_shape=jax.ShapeDtypeStruct((s, qd), BF16),
        scratch_shapes=[pltpu.VMEM((2, (qpk // 2) * blk, 4 * blk), F32)],
        compiler_params=_params(("parallel", "arbitrary"), est),
        name="swa_attention",
    )(rel_bias.astype(F32).reshape(-1), sinks.astype(F32).reshape(-1), bucket, q, kv, kv, kv, kv)


def kernel(x, norm_mix_pre, norm_mix_post, norm_ffn_pre, norm_ffn_post, ssm_w_in, ssm_conv_w, ssm_conv_b, ssm_dt_bias, ssm_a_log, ssm_d, ssm_norm, ssm_w_out, kv_norm, w_kv, b_kv, attn_w_q, attn_b_q, attn_sinks, attn_w_o, attn_b_o, rel_bias, ffn_w_up, ffn_conv_w, ffn_conv_b, ffn_w_down):
    bsz, s, d = x.shape
    assert bsz == 1 and norm_mix_pre.shape[0] == 2
    d_inner = ssm_norm.shape[-1]
    n_heads = ssm_dt_bias.shape[-1]
    zxbc_dim = ssm_w_in.shape[-1] - n_heads

    def mixer_out_ffn(resid, mix, layer):
        d_ff = ffn_w_up.shape[-1] // 2
        tn = _tile(d_ff, 256, LANES)
        tm = _tile(s, 1024, (d // MXU_K_CHUNK) * BF16_TILE_ROWS)
        h_head, u_head = _resnorm(resid, mix, norm_mix_post[layer], [norm_ffn_pre[layer]], rows=tm)
        hff, h = _conv_matmul(u_head, ffn_w_up, layer, d_ff, [0, d_ff // tn], ffn_conv_w[layer],
                              ffn_conv_b[layer].reshape(1, -1), tn, True, BF16, "ffn_up",
                              norm=(resid, mix, norm_mix_post[layer], norm_ffn_pre[layer], h_head))
        return h, _matmul(hff, ffn_w_down, None, F32, 1024, 256, "ffn_down", a_buffers=1, layer=layer)

    h0 = x.reshape(s, d)

    u = _prenorm(h0, norm_mix_pre[0])
    conv_dim = zxbc_dim - d_inner
    tn_in = _tile(math.gcd(d_inner, conv_dim), 512, LANES)
    zs = _matmul(u, ssm_w_in, None, F32, 1024, 512, "in_proj_z", n=d_inner, layer=0, silu=True)
    xbc = _conv_matmul(u, ssm_w_in, 0, conv_dim, [d_inner // tn_in], ssm_conv_w[0].astype(F32),
                       ssm_conv_b[0].astype(F32).reshape(1, -1), tn_in, False, F32, "in_proj_xbc",
                       conv_col_blocks=[0])
    dt_raw = _matmul(u, ssm_w_in, None, F32, 1024, 128, "dt_proj", col0=zxbc_dim, n=n_heads, layer=0)
    y = _ssd(zs, xbc, dt_raw, ssm_dt_bias[0], ssm_a_log[0], ssm_d[0], ssm_norm[0], d_inner, n_heads)
    mix = _matmul(y, ssm_w_out, None, F32, 1024, 256, "out_proj", layer=0)
    h1, f = mixer_out_ffn(h0, mix, 0)

    h2, ukv, uq = _resnorm(h1, f, norm_ffn_post[0], [kv_norm, norm_mix_pre[1]])
    kv = _matmul(ukv, w_kv, b_kv, F32, 1024, 512, "kv_proj", cols_outer=True)
    q = _matmul(uq, attn_w_q, attn_b_q[0], BF16, 1024, 512, "q_proj", layer=0)
    o = _attention(q, kv, attn_sinks[0], rel_bias)
    mix = _matmul(o, attn_w_o, attn_b_o[0], F32, 1024, 512, "o_proj", layer=0)
    h3, f = mixer_out_ffn(h2, mix, 1)
    (h4,) = _resnorm(h3, f, norm_ffn_post[1], [])
    return h4.reshape(bsz, s, d)
```

```python
import functools
import math

import numpy as np
import jax
import jax.numpy as jnp
from jax import lax
from jax.experimental import pallas as pl
from jax.experimental.pallas import tpu as pltpu

EPS = 1e-6
SSM_NORM_EPS = 1e-5
SSM_HEAD_DIM = 64
SSM_GROUPS = 8
D_STATE = 128
SSM_CONV = 4
SSM_CHUNK = 128
SSD_PAIRS_PER_DOT = 2
ATTN_HEAD_DIM = 64
N_KV_HEADS = 8
WINDOW = 128
ATTN_BLOCK = 128
N_BUCKETS = 32
MAX_DISTANCE = 128
FFN_CONV = 3

LANES = 128
SUBLANES = 8
BF16_TILE_ROWS = 16
MXU_K_CHUNK = 256
VMEM_CAP_BYTES = 60 * 1024 * 1024

F32 = jnp.float32
BF16 = jnp.bfloat16


def _vmem_limit(est_bytes):
    return int(min(VMEM_CAP_BYTES, max(32 * 1024 * 1024, est_bytes * 5 // 4 + (4 << 20))))


def _params(semantics, est_bytes, flags=None):
    return pltpu.CompilerParams(dimension_semantics=semantics, vmem_limit_bytes=_vmem_limit(est_bytes), flags=flags)


def _tile(dim, pref, align):
    if dim <= pref:
        return dim
    t = (pref // align) * align
    while t >= align:
        if dim % t == 0:
            return t
        t -= align
    raise ValueError(f"no tile for {dim} (pref {pref}, align {align})")


def _sigmoid(x):
    return 0.5 * jnp.tanh(0.5 * x) + 0.5


def _softplus(x):
    return jnp.maximum(x, 0.0) + jnp.log1p(jnp.exp(-jnp.abs(x)))


def _split3(v):
    hi = v.astype(BF16)
    r1 = v - hi.astype(F32)
    mid = r1.astype(BF16)
    lo = (r1 - mid.astype(F32)).astype(BF16)
    return [hi, mid, lo]


def _rms(x, g, eps):
    ms = jnp.mean(x * x, axis=-1, keepdims=True)
    return (x * lax.rsqrt(ms + eps)) * g


def _prenorm_kernel(x_ref, g_ref, u_ref):
    u_ref[...] = _rms(x_ref[...], g_ref[...], EPS).astype(u_ref.dtype)


def _prenorm(x, g, rows=None):
    s, d = x.shape
    s = s if rows is None else rows
    tr = _tile(s, 256, SUBLANES)
    est = 2 * tr * d * (4 + 2)
    return pl.pallas_call(
        _prenorm_kernel,
        grid=(s // tr,),
        in_specs=[pl.BlockSpec((tr, d), lambda i: (i, 0)), pl.BlockSpec((1, d), lambda i: (0, 0))],
        out_specs=pl.BlockSpec((tr, d), lambda i: (i, 0)),
        out_shape=jax.ShapeDtypeStruct((s, d), BF16),
        compiler_params=_params(("parallel",), est),
        name="prenorm",
    )(x, g.reshape(1, d))


def _resnorm_kernel(r_ref, m_ref, gpost_ref, *rest, n_u):
    g_refs = rest[:n_u]
    h_ref = rest[n_u]
    u_refs = rest[n_u + 1:]
    h = r_ref[...] + _rms(m_ref[...], gpost_ref[...], EPS)
    h_ref[...] = h
    if n_u:
        ms = jnp.mean(h * h, axis=-1, keepdims=True)
        hn = h * lax.rsqrt(ms + EPS)
        for g_ref, u_ref in zip(g_refs, u_refs):
            u_ref[...] = (hn * g_ref[...]).astype(u_ref.dtype)


def _resnorm(r, m, gpost, gains, rows=None):
    s, d = r.shape
    n_u = len(gains)
    s = s if rows is None else rows
    tr = _tile(s, 256, SUBLANES)
    est = 2 * tr * d * (4 * 3 + 2 * n_u)
    row = pl.BlockSpec((tr, d), lambda i: (i, 0))
    vec = pl.BlockSpec((1, d), lambda i: (0, 0))
    outs = pl.pallas_call(
        functools.partial(_resnorm_kernel, n_u=n_u),
        grid=(s // tr,),
        in_specs=[row, row, vec] + [vec] * n_u,
        out_specs=[row] + [row] * n_u,
        out_shape=[jax.ShapeDtypeStruct((s, d), F32)] + [jax.ShapeDtypeStruct((s, d), BF16)] * n_u,
        compiler_params=_params(("parallel",), est),
        name="resnorm",
    )(r, m, gpost.reshape(1, d), *[g.reshape(1, d) for g in gains])
    return outs


def _matmul_kernel(a_ref, w_ref, *rest, has_bias, silu):
    o_ref = rest[-1]
    acc = jnp.dot(a_ref[...], w_ref[...].astype(BF16), preferred_element_type=F32)
    if has_bias:
        acc = acc + rest[0][...]
    if silu:
        acc = acc * _sigmoid(acc)
    o_ref[...] = acc.astype(o_ref.dtype)


def _matmul(a, w, bias, out_dtype, tm_pref, tn_pref, name, a_buffers=2, col0=0, n=None, layer=None, silu=False,
            cols_outer=False):
    m, k = a.shape
    assert (w.ndim == 3) == (layer is not None)
    n = w.shape[-1] if n is None else n
    tm = _tile(m, tm_pref, SUBLANES)
    tn = _tile(math.gcd(n, col0) if col0 else n, tn_pref, LANES)
    assert n % tn == 0 and col0 % tn == 0
    joff = col0 // tn
    osz = jnp.dtype(out_dtype).itemsize
    wsz = jnp.dtype(w.dtype).itemsize
    est = a_buffers * tm * k * 2 + 2 * (k * tn * wsz + tm * tn * osz) + tm * tn * 4 + (k * tn * 2 if wsz != 2 else 0)
    a_mode = {} if a_buffers == 2 else {"pipeline_mode": pl.Buffered(a_buffers)}
    ij = (lambda p, q: (q, p)) if cols_outer else (lambda p, q: (p, q))
    if layer is None:
        w_spec = pl.BlockSpec((k, tn), lambda p, q: (0, ij(p, q)[1] + joff))
    else:
        w_spec = pl.BlockSpec((None, k, tn), lambda p, q: (layer, 0, ij(p, q)[1] + joff))
    in_specs = [pl.BlockSpec((tm, k), lambda p, q: (ij(p, q)[0], 0), **a_mode), w_spec]
    args = [a, w]
    if bias is not None:
        in_specs.append(pl.BlockSpec((1, tn), lambda p, q: (0, ij(p, q)[1])))
        args.append(bias.reshape(1, n).astype(F32))
    return pl.pallas_call(
        functools.partial(_matmul_kernel, has_bias=bias is not None, silu=silu),
        grid=(n // tn, m // tm) if cols_outer else (m // tm, n // tn),
        in_specs=in_specs,
        out_specs=pl.BlockSpec((tm, tn), lambda p, q: ij(p, q)),
        out_shape=jax.ShapeDtypeStruct((m, n), out_dtype),
        compiler_params=_params(("parallel", "parallel"), est),
        name=name,
    )(*args)


def _zero_after(x):
    w = pltpu.bitcast(x, jnp.uint32)
    zero = (w >> 16) >> 16
    return zero if zero.shape[0] == BF16_TILE_ROWS else jnp.concatenate([zero, zero], axis=0)


def _conv_mm_kernel(*refs, nj, n_w, taps, gate, n_tiles, norm_rows, residual):
    if norm_rows and residual:
        u_ref, r_ref, m_ref, gpost_ref, gpre_ref, hhead_ref = refs[0:6]
        refs = refs[6:]
    elif norm_rows:
        u_ref, r_ref, gpre_ref = refs[0:3]
        refs = refs[3:]
    else:
        u_ref = refs[0]
        refs = refs[1:]
    w_refs = refs[0:n_w]
    cw_refs = refs[n_w:2 * n_w]
    cb_refs = refs[2 * n_w:3 * n_w]
    if norm_rows:
        o_ref, h_ref, halo, ext_a, ext_b, wcat, u_cur, u_next = refs[3 * n_w:]
    else:
        o_ref, halo, ext_a, ext_b, wcat = refs[3 * n_w:]
    t = pl.program_id(0)
    tm = u_ref.shape[0]
    d = u_ref.shape[1]
    tn = w_refs[0].shape[1]
    nch = d // MXU_K_CHUNK
    rows = tm // nch
    jp = lax.rem(jnp.maximum(t - 1, 0), nj)

    @pl.when(t == 0)
    def _():
        halo[...] = jnp.zeros(halo.shape, F32)
        ext_b[...] = jnp.zeros(ext_b.shape, F32)

    if norm_rows:
        tile = jnp.minimum(t, n_tiles - 1)
        jm = lax.rem(tile, nj)
        new_row_tile = (jm == 0) & (t < n_tiles)

        @pl.when(new_row_tile & (tile == 0))
        def _():
            u_cur[...] = u_ref[...]

        @pl.when(new_row_tile & (tile > 0))
        def _():
            u_cur[...] = u_next[...]

        lhs_ref = u_cur
    else:
        lhs_ref = u_ref

    def step(ext_mm, ext_ep):
        norm_zero = None
        if norm_rows:
            last_row_tile = tile // nj + 1 >= n_tiles // nj
            row0 = pl.multiple_of(jnp.minimum(jm, tm // norm_rows - 1) * norm_rows, norm_rows)
            if residual:
                h = r_ref[...] + _rms(m_ref[...], gpost_ref[...], EPS)
                h_ref[...] = jnp.where(last_row_tile, hhead_ref[...], h)
                u_new = _rms(h, gpre_ref[...], EPS).astype(BF16)
            else:
                u_new = _rms(r_ref[...], gpre_ref[...], EPS).astype(BF16)
                h_ref[...] = jnp.where(last_row_tile, u_ref[pl.ds(row0, norm_rows), :], u_new)
            u_next[pl.ds(row0, norm_rows), :] = u_new
            norm_zero = _zero_after(u_new[0:BF16_TILE_ROWS, 0:tn])
        cw = jnp.concatenate([r[...] for r in cw_refs], axis=1)
        cb = jnp.concatenate([r[...] for r in cb_refs], axis=1)
        prev = halo[jp]
        for c in range(nch):
            cur = ext_ep[c * rows:(c + 1) * rows, :]
            both = jnp.concatenate([prev, cur], axis=0)
            acc = cb
            for k in range(taps - 1):
                r0 = SUBLANES - (taps - 1) + k
                acc = acc + both[r0:r0 + rows, :] * cw[k:k + 1, :]
            acc = acc + cur * cw[taps - 1:taps, :]
            if gate:
                out = (acc[:, 0:tn] * _sigmoid(acc[:, 0:tn])) * acc[:, tn:2 * tn]
            else:
                out = acc * _sigmoid(acc)
            out = out.astype(o_ref.dtype)
            o_ref[c * rows:(c + 1) * rows, :] = out
            prev = cur[rows - SUBLANES:rows, :]

            zero = _zero_after(out[0:BF16_TILE_ROWS, 0:tn])
            if norm_zero is not None and c == nch - 1:
                zero = zero | norm_zero
            k0 = c * MXU_K_CHUNK
            for idx, w_ref in enumerate(w_refs):
                top = pltpu.bitcast(pltpu.bitcast(w_ref[k0:k0 + BF16_TILE_ROWS, :], jnp.uint32) | zero, F32)
                wcat[k0:k0 + BF16_TILE_ROWS, idx * tn:(idx + 1) * tn] = top.astype(BF16)
                wcat[k0 + BF16_TILE_ROWS:k0 + MXU_K_CHUNK, idx * tn:(idx + 1) * tn] = (
                    w_ref[k0 + BF16_TILE_ROWS:k0 + MXU_K_CHUNK, :].astype(BF16))
        halo[jp] = prev
        ext_mm[...] = jnp.dot(lhs_ref[...], wcat[...], preferred_element_type=F32)

    @pl.when(lax.rem(t, 2) == 0)
    def _():
        step(ext_a, ext_b)

    @pl.when(lax.rem(t, 2) == 1)
    def _():
        step(ext_b, ext_a)


def _conv_matmul(u, w, layer, n_cols, col_blocks, conv_w, conv_b, tn, gate, out_dtype, name, conv_col_blocks=None,
                 norm=None):
    d = u.shape[1]
    s = u.shape[0] if norm is None else norm[0].shape[0]
    n_w = len(col_blocks)
    assert n_w == (2 if gate else 1) and d % MXU_K_CHUNK == 0
    taps = conv_w.shape[0]
    assert n_cols % tn == 0
    nj = n_cols // tn
    nch = d // MXU_K_CHUNK
    tm = _tile(s, 1024, nch * BF16_TILE_ROWS)
    wsz = jnp.dtype(w.dtype).itemsize
    osz = jnp.dtype(out_dtype).itemsize
    wt = n_w * tn
    wo = tn if gate else wt
    est = 2 * (tm * d * 2 + d * wt * wsz + tm * wo * osz) + d * wt * 2 + nj * SUBLANES * wt * 4 + 4 * tm * wt * 4
    n_tiles = (s // tm) * nj

    def mm_tile(t):
        return jnp.minimum(t, n_tiles - 1)

    def ep_tile(t):
        return jnp.maximum(t - 1, 0)

    residual = False
    if norm is None:
        norm_rows = 0
        lead_specs = [pl.BlockSpec((tm, d), lambda t: (mm_tile(t) // nj, 0))]
        lead_args = [u]
        extra_out_specs, extra_out_shapes, extra_scratch = [], [], []
    else:
        residual = len(norm) == 5
        assert u.shape == (tm, d)
        norm_rows = next(c for c in range(BF16_TILE_ROWS, tm + 1, BF16_TILE_ROWS) if tm % c == 0 and tm // c <= nj)
        per_tile = tm // norm_rows
        n_row_tiles = s // tm

        def head_block(t):
            return jnp.minimum(mm_tile(t) % nj, per_tile - 1)

        def norm_block(t):
            ahead = mm_tile(t) // nj + 1
            return jnp.where(ahead < n_row_tiles, ahead, 0) * per_tile + head_block(t)

        rows_spec = pl.BlockSpec((norm_rows, d), lambda t: (norm_block(t), 0))
        vec_spec = pl.BlockSpec((1, d), lambda t: (0, 0))
        u_spec = pl.BlockSpec((tm, d), lambda t: (0, 0), pipeline_mode=pl.Buffered(1))
        if residual:
            r, m, gpost, gpre, h_head = norm
            lead_specs = [u_spec, rows_spec, rows_spec, vec_spec, vec_spec,
                          pl.BlockSpec((norm_rows, d), lambda t: (head_block(t), 0))]
            lead_args = [u, r, m, gpost.reshape(1, d), gpre.reshape(1, d), h_head]
            extra_out_shapes = [jax.ShapeDtypeStruct((s, d), F32)]
        else:
            r, gpre = norm
            lead_specs = [u_spec, rows_spec, vec_spec]
            lead_args = [u, r, gpre.reshape(1, d)]
            extra_out_shapes = [jax.ShapeDtypeStruct((s, d), BF16)]
        extra_out_specs = [rows_spec]
        extra_scratch = [pltpu.VMEM((tm, d), BF16), pltpu.VMEM((tm, d), BF16)]
        est += tm * d * 2 + 8 * norm_rows * d * 4
    w_specs = [pl.BlockSpec((None, d, tn), lambda t, o=o: (layer, 0, mm_tile(t) % nj + o)) for o in col_blocks]
    conv_col_blocks = col_blocks if conv_col_blocks is None else conv_col_blocks
    cw_specs = [pl.BlockSpec((taps, tn), lambda t, o=o: (0, ep_tile(t) % nj + o)) for o in conv_col_blocks]
    cb_specs = [pl.BlockSpec((1, tn), lambda t, o=o: (0, ep_tile(t) % nj + o)) for o in conv_col_blocks]
    outs = pl.pallas_call(
        functools.partial(_conv_mm_kernel, nj=nj, n_w=n_w, taps=taps, gate=gate, n_tiles=n_tiles,
                          norm_rows=norm_rows, residual=residual),
        grid=(n_tiles + 1,),
        in_specs=lead_specs + w_specs + cw_specs + cb_specs,
        out_specs=[pl.BlockSpec((tm, wo), lambda t: (ep_tile(t) // nj, ep_tile(t) % nj))] + extra_out_specs,
        out_shape=[jax.ShapeDtypeStruct((s, nj * wo), out_dtype)] + extra_out_shapes,
        scratch_shapes=[
            pltpu.VMEM((nj, SUBLANES, wt), F32),
            pltpu.VMEM((tm, wt), F32),
            pltpu.VMEM((tm, wt), F32),
            pltpu.VMEM((d, wt), BF16),
        ] + extra_scratch,
        compiler_params=_params(("arbitrary",), est),
        name=name,
    )(*lead_args, *([w] * n_w), *([conv_w] * n_w), *([conv_b] * n_w))
    return outs[0] if norm is None else outs


def _ssd_kernel(z_ref, x_ref, b_ref, c_ref, dt_ref, dtb_ref, alog_ref, dskip_ref, gnorm_ref, expand_ref,
                y_ref, state_ref, *, heads, groups):
    chunk = pl.program_id(0)

    @pl.when(chunk == 0)
    def _():
        state_ref[...] = jnp.zeros(state_ref.shape, F32)

    dtv_all = _softplus(dt_ref[...] + dtb_ref[...])
    a_all = dtv_all * (-jnp.exp(alog_ref[...]))
    a_all_t = jnp.transpose(a_all)
    n_all = dtv_all.shape[1]

    width = heads * SSM_HEAD_DIM
    for gi in range(groups):
        cols = slice(gi * width, (gi + 1) * width)
        ncols = slice(gi * D_STATE, (gi + 1) * D_STATE)
        shift = (n_all - gi * heads) % n_all
        dtv = pltpu.roll(dtv_all, shift, axis=1) if shift else dtv_all
        a = pltpu.roll(a_all, shift, axis=1) if shift else a_all
        y_ref[:, cols] = _ssd_group_chunk(
            z_ref[:, cols], x_ref[:, cols], b_ref[:, ncols], c_ref[:, ncols], dtv, a,
            a_all_t[gi * heads:(gi + 1) * heads, :], dskip_ref[:, cols], gnorm_ref[:, cols], expand_ref,
            state_ref.at[gi], heads).astype(y_ref.dtype)


def _ssd_group_chunk(zs, xs, bc, cc, dtv, a, a_t, dskip, gnorm, expand_ref, state_ref, heads):
    cl = SSM_CHUNK
    width = heads * SSM_HEAD_DIM

    li = lax.broadcasted_iota(jnp.int32, (cl, cl), 0)
    si = lax.broadcasted_iota(jnp.int32, (cl, cl), 1)
    tril = li >= si

    ones_lo = tril.astype(F32).astype(BF16)
    ones_up = (li <= si).astype(F32).astype(BF16)
    acum = jnp.dot(jnp.concatenate([ones_lo] * 3, axis=1), jnp.concatenate(_split3(a), axis=0),
                   preferred_element_type=F32)
    acum_t = jnp.dot(jnp.concatenate(_split3(a_t), axis=1), jnp.concatenate([ones_up] * 3, axis=0),
                     preferred_element_type=F32)
    alast = acum[cl - 1:cl, :]
    exp_acum = jnp.exp(acum)
    decay_end = jnp.exp(alast - acum)

    stacked = jnp.concatenate([dtv, exp_acum, decay_end], axis=0)
    wide = jnp.dot(jnp.concatenate(_split3(stacked)[0:2], axis=1), expand_ref[...], preferred_element_type=F32)
    dt_w = wide[0:cl, :]
    exp_acum_w = wide[cl:2 * cl, :]
    decay_end_w = wide[2 * cl:3 * cl, :]

    xdt = xs * dt_w
    state = state_ref[...]
    bc_t = jnp.transpose(bc).astype(BF16)
    c_prod = jnp.dot(cc.astype(BF16), jnp.concatenate([state.astype(BF16), bc_t], axis=1),
                     preferred_element_type=F32)
    y_off = c_prod[:, 0:width] * exp_acum_w
    cb = jnp.where(tril, c_prod[:, width:width + cl], 0.0)

    lane = lax.broadcasted_iota(jnp.int32, (cl, LANES), 1)
    low = lane < SSM_HEAD_DIM
    npairs = heads // 2
    ppd = min(SSD_PAIRS_PER_DOT, npairs)
    zero_blk = jnp.zeros((cl, LANES), BF16)
    y_diag = []
    for q in range(npairs // ppd):
        ms = []
        rhs_rows = []
        for pi in range(ppd):
            p = q * ppd + pi
            for hp in range(2):
                r = 2 * p + hp
                seg = acum[:, r:r + 1] - acum_t[r:r + 1, :]
                decay = jnp.exp(jnp.minimum(seg, 0.0))
                ms.append((cb * decay).astype(BF16))
            xp = xdt[:, p * LANES:(p + 1) * LANES]
            for blk in (jnp.where(low, xp, 0.0).astype(BF16), jnp.where(low, 0.0, xp).astype(BF16)):
                rhs_rows.append(jnp.concatenate([zero_blk] * pi + [blk] + [zero_blk] * (ppd - 1 - pi), axis=1))
        y_diag.append(jnp.dot(jnp.concatenate(ms, axis=1), jnp.concatenate(rhs_rows, axis=0),
                              preferred_element_type=F32))
    y = jnp.concatenate(y_diag, axis=1) + y_off + dskip * xs

    xw = (xdt * decay_end_w).astype(BF16)
    upd = jnp.dot(bc_t, xw, preferred_element_type=F32)
    state_ref[...] = state * exp_acum_w[cl - 1:cl, :] + upd

    yz = y * zs
    ms2 = jnp.mean(yz * yz, axis=-1, keepdims=True)
    return (yz * lax.rsqrt(ms2 + SSM_NORM_EPS)) * gnorm


def _ssd(zs, xbc, dt_raw, dt_bias, a_log, d_skip, g_norm, d_inner, n_heads):
    s = zs.shape[0]
    g = SSM_GROUPS
    heads = n_heads // g
    width = d_inner // g
    assert width == heads * SSM_HEAD_DIM and heads % 2 == 0 and heads <= LANES and width % LANES == 0
    assert (heads // 2) % min(SSD_PAIRS_PER_DOT, heads // 2) == 0
    assert D_STATE == LANES and s % SSM_CHUNK == 0
    dskip_w = jnp.repeat(d_skip.astype(F32), SSM_HEAD_DIM).reshape(1, d_inner)
    nc = s // SSM_CHUNK
    cl = SSM_CHUNK
    expand2 = np.tile(np.arange(n_heads)[:, None] == (np.arange(width)[None, :] // SSM_HEAD_DIM), (2, 1))

    gn = g * D_STATE
    assert d_inner % gn == 0
    est = 2 * (2 * cl * d_inner * 4 + 2 * cl * gn * 4 + cl * n_heads * 4 + cl * d_inner * 2) \
        + g * D_STATE * width * 4 + 24 * cl * width * 4
    return pl.pallas_call(
        functools.partial(_ssd_kernel, heads=heads, groups=g),
        grid=(nc,),
        in_specs=[
            pl.BlockSpec((cl, d_inner), lambda c: (c, 0)),
            pl.BlockSpec((cl, d_inner), lambda c: (c, 0)),
            pl.BlockSpec((cl, gn), lambda c: (c, d_inner // gn)),
            pl.BlockSpec((cl, gn), lambda c: (c, d_inner // gn + 1)),
            pl.BlockSpec((cl, n_heads), lambda c: (c, 0)),
            pl.BlockSpec((1, n_heads), lambda c: (0, 0)),
            pl.BlockSpec((1, n_heads), lambda c: (0, 0)),
            pl.BlockSpec((1, d_inner), lambda c: (0, 0)),
            pl.BlockSpec((1, d_inner), lambda c: (0, 0)),
            pl.BlockSpec((2 * n_heads, width), lambda c: (0, 0)),
        ],
        out_specs=pl.BlockSpec((cl, d_inner), lambda c: (c, 0)),
        out_shape=jax.ShapeDtypeStruct((s, d_inner), BF16),
        scratch_shapes=[
            pltpu.VMEM((g, D_STATE, width), F32),
        ],
        compiler_params=_params(("arbitrary",), est),
        name="ssd_scan",
    )(zs, xbc, xbc, xbc, dt_raw, dt_bias.astype(F32).reshape(1, n_heads), a_log.astype(F32).reshape(1, n_heads),
      dskip_w, g_norm.reshape(1, d_inner), jnp.asarray(expand2, BF16))


def _t5_bucket_table():
    q = np.arange(ATTN_BLOCK)[:, None]
    k = np.arange(2 * ATTN_BLOCK)[None, :]
    rel = np.maximum(q - k + ATTN_BLOCK, 0)
    max_exact = N_BUCKETS // 2
    relf = np.maximum(rel, 1).astype(np.float32)
    large = max_exact + (np.log(relf / np.float32(max_exact)) / np.float32(math.log(MAX_DISTANCE / max_exact))
                         * np.float32(N_BUCKETS - max_exact)).astype(np.int32)
    large = np.minimum(large, N_BUCKETS - 1)
    return np.where(rel < max_exact, rel, large).astype(np.int32)


def _attn_kernel(relb_ref, sink_ref, bucket_ref, q_ref, kp_ref, kc_ref, vp_ref, vc_ref, o_ref, bias_ref,
                 *, qpk, n_q_heads):
    gp = pl.program_id(0)
    n = pl.program_id(1)
    blk = ATTN_BLOCK
    npair = qpk // 2
    head0 = gp * 2 * qpk

    @pl.when(n == 0)
    def _():
        bucket = bucket_ref[...]
        qi = lax.broadcasted_iota(jnp.int32, (blk, 2 * blk), 0)
        ci = lax.broadcasted_iota(jnp.int32, (blk, 2 * blk), 1)
        rel = qi - ci + blk
        in_window = (rel >= 0) & (rel < WINDOW)

        def body(it, carry):
            kvh = it // npair
            pair = it - kvh * npair
            row0 = pl.multiple_of(pair * blk, blk)
            for hp in range(2):
                head = head0 + kvh * qpk + pair * 2 + hp
                acc = jnp.zeros((blk, 2 * blk), F32)
                for b in range(N_BUCKETS):
                    acc = jnp.where(bucket == b, relb_ref[b * n_q_heads + head], acc)
                bias_ref[kvh, pl.ds(row0, blk), hp * 2 * blk:(hp + 1) * 2 * blk] = jnp.where(in_window, acc, -jnp.inf)
            return carry

        lax.fori_loop(0, 2 * npair, body, 0)

    kk = jnp.concatenate([kp_ref[...], kc_ref[...]], axis=0)
    vv = jnp.concatenate([vp_ref[...], vc_ref[...]], axis=0)
    lane = lax.broadcasted_iota(jnp.int32, (2 * blk, LANES), 1)
    low = lane < ATTN_HEAD_DIM
    scale = ATTN_HEAD_DIM ** -0.5

    def pair_operand(t, kvh, mult):
        sel = jnp.where(low if kvh == 0 else jnp.logical_not(low), t, 0.0)
        dup = sel + pltpu.roll(sel, ATTN_HEAD_DIM, axis=1)
        if mult is not None:
            dup = dup * mult
        return jnp.concatenate([jnp.where(low, dup, 0.0), jnp.where(low, 0.0, dup)], axis=0).astype(BF16)

    row_bd = lax.broadcasted_iota(jnp.int32, (4 * blk, LANES), 0)
    lane_bd = lax.broadcasted_iota(jnp.int32, (4 * blk, LANES), 1)
    ones_bd = ((row_bd < 2 * blk) == (lane_bd < ATTN_HEAD_DIM)).astype(F32).astype(BF16)
    low_out = lax.broadcasted_iota(jnp.int32, (npair * blk, LANES), 1) < ATTN_HEAD_DIM

    def run(first_block):
        if first_block:
            ci = lax.broadcasted_iota(jnp.int32, (npair * blk, 2 * blk), 1)
            before_start = ci < blk
        for kvh in range(2):
            kbd = pair_operand(kk, kvh, scale)
            vbd = pair_operand(vv, kvh, None)
            base = kvh * npair
            qs = jnp.concatenate([q_ref[:, (base + i) * LANES:(base + i + 1) * LANES] for i in range(npair)], axis=0)
            s = lax.dot_general(qs, kbd, (((1,), (1,)), ((), ())), preferred_element_type=F32)
            es = []
            sink_terms = []
            for hp in range(2):
                sinks = [sink_ref[head0 + kvh * qpk + i * 2 + hp] for i in range(npair)]
                sink_col = jnp.concatenate([jnp.full((blk, 1), v, F32) for v in sinks], axis=0)
                sink_all = jnp.concatenate([jnp.full((blk, LANES), v, F32) for v in sinks], axis=0)
                sh = s[:, hp * 2 * blk:(hp + 1) * 2 * blk] + bias_ref[kvh, :, hp * 2 * blk:(hp + 1) * 2 * blk]
                if first_block:
                    sh = jnp.where(before_start, -jnp.inf, sh)
                m = jnp.maximum(jnp.max(sh, axis=-1, keepdims=True), sink_col)
                es.append(jnp.exp(sh - m).astype(BF16))
                sink_terms.append(jnp.exp(sink_all - jnp.broadcast_to(m, sink_all.shape)))
            e = jnp.concatenate(es, axis=1)
            ov = jnp.dot(e, jnp.concatenate([vbd, ones_bd], axis=1), preferred_element_type=F32)
            denom = ov[:, LANES:2 * LANES] + jnp.where(low_out, sink_terms[0], sink_terms[1])
            o = (ov[:, 0:LANES] * (1.0 / denom)).astype(o_ref.dtype)
            for i in range(npair):
                o_ref[:, (base + i) * LANES:(base + i + 1) * LANES] = o[i * blk:(i + 1) * blk, :]

    @pl.when(n == 0)
    def _():
        run(True)

    @pl.when(n != 0)
    def _():
        run(False)


def _attention(q, kv, sinks, rel_bias):
    s, qd = q.shape
    n_q_heads = qd // ATTN_HEAD_DIM
    qpk = n_q_heads // N_KV_HEADS
    assert qpk % 2 == 0 and N_KV_HEADS % 2 == 0 and 2 * ATTN_HEAD_DIM == LANES and s % ATTN_BLOCK == 0
    blk = ATTN_BLOCK
    nb = s // blk
    ngp = N_KV_HEADS // 2
    qw = 2 * qpk * ATTN_HEAD_DIM
    voff = N_KV_HEADS * ATTN_HEAD_DIM // LANES
    bucket = jnp.asarray(_t5_bucket_table())
    est = 2 * (2 * blk * qw * 2 + 4 * blk * LANES * 4 + blk * 2 * blk * 4) + 2 * qpk * blk * 2 * blk * 4 \
        + 24 * blk * 4 * blk * 4
    smem = pl.BlockSpec(memory_space=pltpu.SMEM)
    return pl.pallas_call(
        functools.partial(_attn_kernel, qpk=qpk, n_q_heads=n_q_heads),
        grid=(ngp, nb),
        in_specs=[
            smem, smem,
            pl.BlockSpec((blk, 2 * blk), lambda g, n: (0, 0)),
            pl.BlockSpec((blk, qw), lambda g, n: (n, g)),
            pl.BlockSpec((blk, LANES), lambda g, n: (jnp.maximum(n - 1, 0), g)),
            pl.BlockSpec((blk, LANES), lambda g, n: (n, g)),
            pl.BlockSpec((blk, LANES), lambda g, n: (jnp.maximum(n - 1, 0), voff + g)),
            pl.BlockSpec((blk, LANES), lambda g, n: (n, voff + g)),
        ],
        out_specs=pl.BlockSpec((blk, qw), lambda g, n: (n, g)),
        out_shape=jax.ShapeDtypeStruct((s, qd), BF16),
        scratch_shapes=[pltpu.VMEM((2, (qpk // 2) * blk, 4 * blk), F32)],
        compiler_params=_params(("parallel", "arbitrary"), est),
        name="swa_attention",
    )(rel_bias.astype(F32).reshape(-1), sinks.astype(F32).reshape(-1), bucket, q, kv, kv, kv, kv)


def kernel(x, norm_mix_pre, norm_mix_post, norm_ffn_pre, norm_ffn_post, ssm_w_in, ssm_conv_w, ssm_conv_b, ssm_dt_bias, ssm_a_log, ssm_d, ssm_norm, ssm_w_out, kv_norm, w_kv, b_kv, attn_w_q, attn_b_q, attn_sinks, attn_w_o, attn_b_o, rel_bias, ffn_w_up, ffn_conv_w, ffn_conv_b, ffn_w_down):
    bsz, s, d = x.shape
    assert bsz == 1 and norm_mix_pre.shape[0] == 2
    d_inner = ssm_norm.shape[-1]
    n_heads = ssm_dt_bias.shape[-1]
    zxbc_dim = ssm_w_in.shape[-1] - n_heads

    def mixer_out_ffn(resid, mix, layer):
        d_ff = ffn_w_up.shape[-1] // 2
        tn = _tile(d_ff, 256, LANES)
        tm = _tile(s, 1024, (d // MXU_K_CHUNK) * BF16_TILE_ROWS)
        h_head, u_head = _resnorm(resid, mix, norm_mix_post[layer], [norm_ffn_pre[layer]], rows=tm)
        hff, h = _conv_matmul(u_head, ffn_w_up, layer, d_ff, [0, d_ff // tn], ffn_conv_w[layer],
                              ffn_conv_b[layer].reshape(1, -1), tn, True, BF16, "ffn_up",
                              norm=(resid, mix, norm_mix_post[layer], norm_ffn_pre[layer], h_head))
        return h, _matmul(hff, ffn_w_down, None, F32, 1024, 256, "ffn_down", a_buffers=1, layer=layer)

    h0 = x.reshape(s, d)

    tm_in = _tile(s, 1024, (d // MXU_K_CHUNK) * BF16_TILE_ROWS)
    u_head = _prenorm(h0, norm_mix_pre[0], rows=tm_in)
    conv_dim = zxbc_dim - d_inner
    tn_in = _tile(math.gcd(d_inner, conv_dim), 512, LANES)
    xbc, u = _conv_matmul(u_head, ssm_w_in, 0, conv_dim, [d_inner // tn_in], ssm_conv_w[0].astype(F32),
                          ssm_conv_b[0].astype(F32).reshape(1, -1), tn_in, False, F32, "in_proj_xbc",
                          conv_col_blocks=[0], norm=(h0, norm_mix_pre[0]))
    zs = _matmul(u, ssm_w_in, None, F32, 1024, 512, "in_proj_z", n=d_inner, layer=0, silu=True)
    dt_raw = _matmul(u, ssm_w_in, None, F32, 1024, 128, "dt_proj", col0=zxbc_dim, n=n_heads, layer=0)
    y = _ssd(zs, xbc, dt_raw, ssm_dt_bias[0], ssm_a_log[0], ssm_d[0], ssm_norm[0], d_inner, n_heads)
    mix = _matmul(y, ssm_w_out, None, F32, 1024, 256, "out_proj", layer=0)
    h1, f = mixer_out_ffn(h0, mix, 0)

    h2, ukv, uq = _resnorm(h1, f, norm_ffn_post[0], [kv_norm, norm_mix_pre[1]])
    kv = _matmul(ukv, w_kv, b_kv, F32, 1024, 512, "kv_proj", cols_outer=True)
    q = _matmul(uq, attn_w_q, attn_b_q[0], BF16, 1024, 512, "q_proj", layer=0)
    o = _attention(q, kv, attn_sinks[0], rel_bias)
    mix = _matmul(o, attn_w_o, attn_b_o[0], F32, 1024, 512, "o_proj", layer=0)
    h3, f = mixer_out_ffn(h2, mix, 1)
    (h4,) = _resnorm(h3, f, norm_ffn_post[1], [])
    return h4.reshape(bsz, s, d)
```

```python
import functools
import math

import numpy as np
import jax
import jax.numpy as jnp
from jax import lax
from jax.experimental import pallas as pl
from jax.experimental.pallas import tpu as pltpu

EPS = 1e-6
SSM_NORM_EPS = 1e-5
SSM_HEAD_DIM = 64
SSM_GROUPS = 8
D_STATE = 128
SSM_CONV = 4
SSM_CHUNK = 128
SSD_PAIRS_PER_DOT = 2
ATTN_HEAD_DIM = 64
N_KV_HEADS = 8
WINDOW = 128
ATTN_BLOCK = 128
N_BUCKETS = 32
MAX_DISTANCE = 128
FFN_CONV = 3

LANES = 128
SUBLANES = 8
BF16_TILE_ROWS = 16
MXU_K_CHUNK = 256
VMEM_CAP_BYTES = 60 * 1024 * 1024

F32 = jnp.float32
BF16 = jnp.bfloat16


def _vmem_limit(est_bytes):
    return int(min(VMEM_CAP_BYTES, max(32 * 1024 * 1024, est_bytes * 5 // 4 + (4 << 20))))


def _params(semantics, est_bytes, flags=None):
    return pltpu.CompilerParams(dimension_semantics=semantics, vmem_limit_bytes=_vmem_limit(est_bytes), flags=flags)


def _tile(dim, pref, align):
    if dim <= pref:
        return dim
    t = (pref // align) * align
    while t >= align:
        if dim % t == 0:
            return t
        t -= align
    raise ValueError(f"no tile for {dim} (pref {pref}, align {align})")


def _sigmoid(x):
    return 0.5 * jnp.tanh(0.5 * x) + 0.5


def _softplus(x):
    return jnp.maximum(x, 0.0) + jnp.log1p(jnp.exp(-jnp.abs(x)))


def _split3(v):
    hi = v.astype(BF16)
    r1 = v - hi.astype(F32)
    mid = r1.astype(BF16)
    lo = (r1 - mid.astype(F32)).astype(BF16)
    return [hi, mid, lo]


def _rms(x, g, eps):
    ms = jnp.mean(x * x, axis=-1, keepdims=True)
    return (x * lax.rsqrt(ms + eps)) * g


def _prenorm_kernel(x_ref, g_ref, u_ref):
    u_ref[...] = _rms(x_ref[...], g_ref[...], EPS).astype(u_ref.dtype)


def _prenorm(x, g, rows=None):
    s, d = x.shape
    s = s if rows is None else rows
    tr = _tile(s, 256, SUBLANES)
    est = 2 * tr * d * (4 + 2)
    return pl.pallas_call(
        _prenorm_kernel,
        grid=(s // tr,),
        in_specs=[pl.BlockSpec((tr, d), lambda i: (i, 0)), pl.BlockSpec((1, d), lambda i: (0, 0))],
        out_specs=pl.BlockSpec((tr, d), lambda i: (i, 0)),
        out_shape=jax.ShapeDtypeStruct((s, d), BF16),
        compiler_params=_params(("parallel",), est),
        name="prenorm",
    )(x, g.reshape(1, d))


def _resnorm_kernel(r_ref, m_ref, gpost_ref, *rest, n_u):
    g_refs = rest[:n_u]
    h_ref = rest[n_u]
    u_refs = rest[n_u + 1:]
    h = r_ref[...] + _rms(m_ref[...], gpost_ref[...], EPS)
    h_ref[...] = h
    if n_u:
        ms = jnp.mean(h * h, axis=-1, keepdims=True)
        hn = h * lax.rsqrt(ms + EPS)
        for g_ref, u_ref in zip(g_refs, u_refs):
            u_ref[...] = (hn * g_ref[...]).astype(u_ref.dtype)


def _resnorm(r, m, gpost, gains, rows=None):
    s, d = r.shape
    n_u = len(gains)
    s = s if rows is None else rows
    tr = _tile(s, 256, SUBLANES)
    est = 2 * tr * d * (4 * 3 + 2 * n_u)
    row = pl.BlockSpec((tr, d), lambda i: (i, 0))
    vec = pl.BlockSpec((1, d), lambda i: (0, 0))
    outs = pl.pallas_call(
        functools.partial(_resnorm_kernel, n_u=n_u),
        grid=(s // tr,),
        in_specs=[row, row, vec] + [vec] * n_u,
        out_specs=[row] + [row] * n_u,
        out_shape=[jax.ShapeDtypeStruct((s, d), F32)] + [jax.ShapeDtypeStruct((s, d), BF16)] * n_u,
        compiler_params=_params(("parallel",), est),
        name="resnorm",
    )(r, m, gpost.reshape(1, d), *[g.reshape(1, d) for g in gains])
    return outs


def _matmul_kernel(a_ref, w_ref, *rest, has_bias, silu):
    o_ref = rest[-1]
    acc = jnp.dot(a_ref[...], w_ref[...].astype(BF16), preferred_element_type=F32)
    if has_bias:
        acc = acc + rest[0][...]
    if silu:
        acc = acc * _sigmoid(acc)
    o_ref[...] = acc.astype(o_ref.dtype)


def _matmul(a, w, bias, out_dtype, tm_pref, tn_pref, name, a_buffers=2, col0=0, n=None, layer=None, silu=False,
            cols_outer=False):
    m, k = a.shape
    assert (w.ndim == 3) == (layer is not None)
    n = w.shape[-1] if n is None else n
    tm = _tile(m, tm_pref, SUBLANES)
    tn = _tile(math.gcd(n, col0) if col0 else n, tn_pref, LANES)
    assert n % tn == 0 and col0 % tn == 0
    joff = col0 // tn
    osz = jnp.dtype(out_dtype).itemsize
    wsz = jnp.dtype(w.dtype).itemsize
    est = a_buffers * tm * k * 2 + 2 * (k * tn * wsz + tm * tn * osz) + tm * tn * 4 + (k * tn * 2 if wsz != 2 else 0)
    a_mode = {} if a_buffers == 2 else {"pipeline_mode": pl.Buffered(a_buffers)}
    ij = (lambda p, q: (q, p)) if cols_outer else (lambda p, q: (p, q))
    if layer is None:
        w_spec = pl.BlockSpec((k, tn), lambda p, q: (0, ij(p, q)[1] + joff))
    else:
        w_spec = pl.BlockSpec((None, k, tn), lambda p, q: (layer, 0, ij(p, q)[1] + joff))
    in_specs = [pl.BlockSpec((tm, k), lambda p, q: (ij(p, q)[0], 0), **a_mode), w_spec]
    args = [a, w]
    if bias is not None:
        in_specs.append(pl.BlockSpec((1, tn), lambda p, q: (0, ij(p, q)[1])))
        args.append(bias.reshape(1, n).astype(F32))
    return pl.pallas_call(
        functools.partial(_matmul_kernel, has_bias=bias is not None, silu=silu),
        grid=(n // tn, m // tm) if cols_outer else (m // tm, n // tn),
        in_specs=in_specs,
        out_specs=pl.BlockSpec((tm, tn), lambda p, q: ij(p, q)),
        out_shape=jax.ShapeDtypeStruct((m, n), out_dtype),
        compiler_params=_params(("parallel", "parallel"), est),
        name=name,
    )(*args)


def _zero_after(x):
    w = pltpu.bitcast(x, jnp.uint32)
    zero = (w >> 16) >> 16
    return zero if zero.shape[0] == BF16_TILE_ROWS else jnp.concatenate([zero, zero], axis=0)


def _conv_mm_kernel(*refs, nj, n_w, taps, gate, silu, n_tiles, norm_rows, residual, second_gain):
    if norm_rows and residual:
        u_ref, r_ref, m_ref, gpost_ref, gpre_ref, hhead_ref = refs[0:6]
        refs = refs[6:]
        if second_gain:
            gpre2_ref, u2head_ref = refs[0:2]
            refs = refs[2:]
    elif norm_rows:
        u_ref, r_ref, gpre_ref = refs[0:3]
        refs = refs[3:]
    else:
        u_ref = refs[0]
        refs = refs[1:]
    w_refs = refs[0:n_w]
    cw_refs = refs[n_w:2 * n_w]
    cb_refs = refs[2 * n_w:3 * n_w]
    if norm_rows and second_gain:
        o_ref, h_ref, u2_ref, halo, ext_a, ext_b, wcat, u_cur, u_next = refs[3 * n_w:]
    elif norm_rows:
        o_ref, h_ref, halo, ext_a, ext_b, wcat, u_cur, u_next = refs[3 * n_w:]
    else:
        o_ref, halo, ext_a, ext_b, wcat = refs[3 * n_w:]
    t = pl.program_id(0)
    tm = u_ref.shape[0]
    d = u_ref.shape[1]
    tn = w_refs[0].shape[1]
    nch = d // MXU_K_CHUNK
    rows = tm // nch
    jp = lax.rem(jnp.maximum(t - 1, 0), nj)

    @pl.when(t == 0)
    def _():
        halo[...] = jnp.zeros(halo.shape, F32)
        ext_b[...] = jnp.zeros(ext_b.shape, F32)

    if norm_rows:
        tile = jnp.minimum(t, n_tiles - 1)
        jm = lax.rem(tile, nj)
        new_row_tile = (jm == 0) & (t < n_tiles)

        @pl.when(new_row_tile & (tile == 0))
        def _():
            u_cur[...] = u_ref[...]

        @pl.when(new_row_tile & (tile > 0))
        def _():
            u_cur[...] = u_next[...]

        lhs_ref = u_cur
    else:
        lhs_ref = u_ref

    def step(ext_mm, ext_ep):
        norm_zero = None
        if norm_rows:
            last_row_tile = tile // nj + 1 >= n_tiles // nj
            row0 = pl.multiple_of(jnp.minimum(jm, tm // norm_rows - 1) * norm_rows, norm_rows)
            if residual:
                h = r_ref[...] + _rms(m_ref[...], gpost_ref[...], EPS)
                h_ref[...] = jnp.where(last_row_tile, hhead_ref[...], h)
                hn = h * lax.rsqrt(jnp.mean(h * h, axis=-1, keepdims=True) + EPS)
                u_new = (hn * gpre_ref[...]).astype(BF16)
                if second_gain:
                    u2_ref[...] = jnp.where(last_row_tile, u2head_ref[...], (hn * gpre2_ref[...]).astype(BF16))
            else:
                u_new = _rms(r_ref[...], gpre_ref[...], EPS).astype(BF16)
                h_ref[...] = jnp.where(last_row_tile, u_ref[pl.ds(row0, norm_rows), :], u_new)
            u_next[pl.ds(row0, norm_rows), :] = u_new
            norm_zero = _zero_after(u_new[0:BF16_TILE_ROWS, 0:tn])
        cw = jnp.concatenate([r[...] for r in cw_refs], axis=1)
        cb = jnp.concatenate([r[...] for r in cb_refs], axis=1)
        prev = halo[jp]
        for c in range(nch):
            cur = ext_ep[c * rows:(c + 1) * rows, :]
            both = jnp.concatenate([prev, cur], axis=0)
            acc = cb
            for k in range(taps - 1):
                r0 = SUBLANES - (taps - 1) + k
                acc = acc + both[r0:r0 + rows, :] * cw[k:k + 1, :]
            acc = acc + cur * cw[taps - 1:taps, :]
            if gate:
                out = (acc[:, 0:tn] * _sigmoid(acc[:, 0:tn])) * acc[:, tn:2 * tn]
            elif silu:
                out = acc * _sigmoid(acc)
            else:
                out = acc
            out = out.astype(o_ref.dtype)
            o_ref[c * rows:(c + 1) * rows, :] = out
            prev = cur[rows - SUBLANES:rows, :]

            zero = _zero_after(out[0:BF16_TILE_ROWS, 0:tn])
            if norm_zero is not None and c == nch - 1:
                zero = zero | norm_zero
            k0 = c * MXU_K_CHUNK
            for idx, w_ref in enumerate(w_refs):
                top = pltpu.bitcast(pltpu.bitcast(w_ref[k0:k0 + BF16_TILE_ROWS, :], jnp.uint32) | zero, F32)
                wcat[k0:k0 + BF16_TILE_ROWS, idx * tn:(idx + 1) * tn] = top.astype(BF16)
                wcat[k0 + BF16_TILE_ROWS:k0 + MXU_K_CHUNK, idx * tn:(idx + 1) * tn] = (
                    w_ref[k0 + BF16_TILE_ROWS:k0 + MXU_K_CHUNK, :].astype(BF16))
        halo[jp] = prev
        ext_mm[...] = jnp.dot(lhs_ref[...], wcat[...], preferred_element_type=F32)

    @pl.when(lax.rem(t, 2) == 0)
    def _():
        step(ext_a, ext_b)

    @pl.when(lax.rem(t, 2) == 1)
    def _():
        step(ext_b, ext_a)


def _conv_matmul(u, w, layer, n_cols, col_blocks, conv_w, conv_b, tn, gate, out_dtype, name, conv_col_blocks=None,
                 norm=None, silu=True):
    d = u.shape[1]
    s = u.shape[0] if norm is None else norm[0].shape[0]
    n_w = len(col_blocks)
    assert n_w == (2 if gate else 1) and d % MXU_K_CHUNK == 0
    taps = conv_w.shape[0]
    assert n_cols % tn == 0
    nj = n_cols // tn
    nch = d // MXU_K_CHUNK
    tm = _tile(s, 1024, nch * BF16_TILE_ROWS)
    wsz = jnp.dtype(w.dtype).itemsize
    osz = jnp.dtype(out_dtype).itemsize
    wt = n_w * tn
    wo = tn if gate else wt
    est = 2 * (tm * d * 2 + d * wt * wsz + tm * wo * osz) + d * wt * 2 + nj * SUBLANES * wt * 4 + 4 * tm * wt * 4
    n_tiles = (s // tm) * nj

    def mm_tile(t):
        return jnp.minimum(t, n_tiles - 1)

    def ep_tile(t):
        return jnp.maximum(t - 1, 0)

    residual = second_gain = False
    if norm is None:
        norm_rows = 0
        lead_specs = [pl.BlockSpec((tm, d), lambda t: (mm_tile(t) // nj, 0))]
        lead_args = [u]
        extra_out_specs, extra_out_shapes, extra_scratch = [], [], []
    else:
        residual = len(norm) >= 5
        second_gain = len(norm) == 7
        assert u.shape == (tm, d)
        norm_rows = next(c for c in range(BF16_TILE_ROWS, tm + 1, BF16_TILE_ROWS) if tm % c == 0 and tm // c <= nj)
        per_tile = tm // norm_rows
        n_row_tiles = s // tm

        def head_block(t):
            return jnp.minimum(mm_tile(t) % nj, per_tile - 1)

        def norm_block(t):
            ahead = mm_tile(t) // nj + 1
            return jnp.where(ahead < n_row_tiles, ahead, 0) * per_tile + head_block(t)

        rows_spec = pl.BlockSpec((norm_rows, d), lambda t: (norm_block(t), 0))
        vec_spec = pl.BlockSpec((1, d), lambda t: (0, 0))
        u_spec = pl.BlockSpec((tm, d), lambda t: (0, 0), pipeline_mode=pl.Buffered(1))
        if residual:
            r, m, gpost, gpre, h_head = norm[0:5]
            head_spec = pl.BlockSpec((norm_rows, d), lambda t: (head_block(t), 0))
            lead_specs = [u_spec, rows_spec, rows_spec, vec_spec, vec_spec, head_spec]
            lead_args = [u, r, m, gpost.reshape(1, d), gpre.reshape(1, d), h_head]
            extra_out_shapes = [jax.ShapeDtypeStruct((s, d), F32)]
            if second_gain:
                lead_specs += [vec_spec, head_spec]
                lead_args += [norm[5].reshape(1, d), norm[6]]
                extra_out_shapes.append(jax.ShapeDtypeStruct((s, d), BF16))
        else:
            r, gpre = norm
            lead_specs = [u_spec, rows_spec, vec_spec]
            lead_args = [u, r, gpre.reshape(1, d)]
            extra_out_shapes = [jax.ShapeDtypeStruct((s, d), BF16)]
        extra_out_specs = [rows_spec] * len(extra_out_shapes)
        extra_scratch = [pltpu.VMEM((tm, d), BF16), pltpu.VMEM((tm, d), BF16)]
        est += tm * d * 2 + 12 * norm_rows * d * 4
    w_specs = [pl.BlockSpec((None, d, tn), lambda t, o=o: (layer, 0, mm_tile(t) % nj + o)) for o in col_blocks]
    conv_col_blocks = col_blocks if conv_col_blocks is None else conv_col_blocks
    cw_specs = [pl.BlockSpec((taps, tn), lambda t, o=o: (0, ep_tile(t) % nj + o)) for o in conv_col_blocks]
    cb_specs = [pl.BlockSpec((1, tn), lambda t, o=o: (0, ep_tile(t) % nj + o)) for o in conv_col_blocks]
    outs = pl.pallas_call(
        functools.partial(_conv_mm_kernel, nj=nj, n_w=n_w, taps=taps, gate=gate, silu=silu, n_tiles=n_tiles,
                          norm_rows=norm_rows, residual=residual, second_gain=second_gain),
        grid=(n_tiles + 1,),
        in_specs=lead_specs + w_specs + cw_specs + cb_specs,
        out_specs=[pl.BlockSpec((tm, wo), lambda t: (ep_tile(t) // nj, ep_tile(t) % nj))] + extra_out_specs,
        out_shape=[jax.ShapeDtypeStruct((s, nj * wo), out_dtype)] + extra_out_shapes,
        scratch_shapes=[
            pltpu.VMEM((nj, SUBLANES, wt), F32),
            pltpu.VMEM((tm, wt), F32),
            pltpu.VMEM((tm, wt), F32),
            pltpu.VMEM((d, wt), BF16),
        ] + extra_scratch,
        compiler_params=_params(("arbitrary",), est),
        name=name,
    )(*lead_args, *([w] * n_w), *([conv_w] * n_w), *([conv_b] * n_w))
    return outs[0] if norm is None else outs


def _ssd_kernel(z_ref, x_ref, b_ref, c_ref, dt_ref, dtb_ref, alog_ref, dskip_ref, gnorm_ref, expand_ref,
                y_ref, state_ref, *, heads, groups):
    chunk = pl.program_id(0)

    @pl.when(chunk == 0)
    def _():
        state_ref[...] = jnp.zeros(state_ref.shape, F32)

    dtv_all = _softplus(dt_ref[...] + dtb_ref[...])
    a_all = dtv_all * (-jnp.exp(alog_ref[...]))
    a_all_t = jnp.transpose(a_all)
    n_all = dtv_all.shape[1]

    width = heads * SSM_HEAD_DIM
    for gi in range(groups):
        cols = slice(gi * width, (gi + 1) * width)
        ncols = slice(gi * D_STATE, (gi + 1) * D_STATE)
        shift = (n_all - gi * heads) % n_all
        dtv = pltpu.roll(dtv_all, shift, axis=1) if shift else dtv_all
        a = pltpu.roll(a_all, shift, axis=1) if shift else a_all
        y_ref[:, cols] = _ssd_group_chunk(
            z_ref[:, cols], x_ref[:, cols], b_ref[:, ncols], c_ref[:, ncols], dtv, a,
            a_all_t[gi * heads:(gi + 1) * heads, :], dskip_ref[:, cols], gnorm_ref[:, cols], expand_ref,
            state_ref.at[gi], heads).astype(y_ref.dtype)


def _ssd_group_chunk(zs, xs, bc, cc, dtv, a, a_t, dskip, gnorm, expand_ref, state_ref, heads):
    cl = SSM_CHUNK
    width = heads * SSM_HEAD_DIM

    li = lax.broadcasted_iota(jnp.int32, (cl, cl), 0)
    si = lax.broadcasted_iota(jnp.int32, (cl, cl), 1)
    tril = li >= si

    ones_lo = tril.astype(F32).astype(BF16)
    ones_up = (li <= si).astype(F32).astype(BF16)
    acum = jnp.dot(jnp.concatenate([ones_lo] * 3, axis=1), jnp.concatenate(_split3(a), axis=0),
                   preferred_element_type=F32)
    acum_t = jnp.dot(jnp.concatenate(_split3(a_t), axis=1), jnp.concatenate([ones_up] * 3, axis=0),
                     preferred_element_type=F32)
    alast = acum[cl - 1:cl, :]
    exp_acum = jnp.exp(acum)
    decay_end = jnp.exp(alast - acum)

    stacked = jnp.concatenate([dtv, exp_acum, decay_end], axis=0)
    wide = jnp.dot(jnp.concatenate(_split3(stacked)[0:2], axis=1), expand_ref[...], preferred_element_type=F32)
    dt_w = wide[0:cl, :]
    exp_acum_w = wide[cl:2 * cl, :]
    decay_end_w = wide[2 * cl:3 * cl, :]

    xdt = xs * dt_w
    state = state_ref[...]
    bc_t = jnp.transpose(bc).astype(BF16)
    c_prod = jnp.dot(cc.astype(BF16), jnp.concatenate([state.astype(BF16), bc_t], axis=1),
                     preferred_element_type=F32)
    y_off = c_prod[:, 0:width] * exp_acum_w
    cb = jnp.where(tril, c_prod[:, width:width + cl], 0.0)

    lane = lax.broadcasted_iota(jnp.int32, (cl, LANES), 1)
    low = lane < SSM_HEAD_DIM
    npairs = heads // 2
    ppd = min(SSD_PAIRS_PER_DOT, npairs)
    zero_blk = jnp.zeros((cl, LANES), BF16)
    y_diag = []
    for q in range(npairs // ppd):
        ms = []
        rhs_rows = []
        for pi in range(ppd):
            p = q * ppd + pi
            for hp in range(2):
                r = 2 * p + hp
                seg = acum[:, r:r + 1] - acum_t[r:r + 1, :]
                decay = jnp.exp(jnp.minimum(seg, 0.0))
                ms.append((cb * decay).astype(BF16))
            xp = xdt[:, p * LANES:(p + 1) * LANES]
            for blk in (jnp.where(low, xp, 0.0).astype(BF16), jnp.where(low, 0.0, xp).astype(BF16)):
                rhs_rows.append(jnp.concatenate([zero_blk] * pi + [blk] + [zero_blk] * (ppd - 1 - pi), axis=1))
        y_diag.append(jnp.dot(jnp.concatenate(ms, axis=1), jnp.concatenate(rhs_rows, axis=0),
                              preferred_element_type=F32))
    y = jnp.concatenate(y_diag, axis=1) + y_off + dskip * xs

    xw = (xdt * decay_end_w).astype(BF16)
    upd = jnp.dot(bc_t, xw, preferred_element_type=F32)
    state_ref[...] = state * exp_acum_w[cl - 1:cl, :] + upd

    yz = y * zs
    ms2 = jnp.mean(yz * yz, axis=-1, keepdims=True)
    return (yz * lax.rsqrt(ms2 + SSM_NORM_EPS)) * gnorm


def _ssd(zs, xbc, dt_raw, dt_bias, a_log, d_skip, g_norm, d_inner, n_heads):
    s = zs.shape[0]
    g = SSM_GROUPS
    heads = n_heads // g
    width = d_inner // g
    assert width == heads * SSM_HEAD_DIM and heads % 2 == 0 and heads <= LANES and width % LANES == 0
    assert (heads // 2) % min(SSD_PAIRS_PER_DOT, heads // 2) == 0
    assert D_STATE == LANES and s % SSM_CHUNK == 0
    dskip_w = jnp.repeat(d_skip.astype(F32), SSM_HEAD_DIM).reshape(1, d_inner)
    nc = s // SSM_CHUNK
    cl = SSM_CHUNK
    expand2 = np.tile(np.arange(n_heads)[:, None] == (np.arange(width)[None, :] // SSM_HEAD_DIM), (2, 1))

    gn = g * D_STATE
    assert d_inner % gn == 0
    est = 2 * (2 * cl * d_inner * 4 + 2 * cl * gn * 4 + cl * n_heads * 4 + cl * d_inner * 2) \
        + g * D_STATE * width * 4 + 24 * cl * width * 4
    return pl.pallas_call(
        functools.partial(_ssd_kernel, heads=heads, groups=g),
        grid=(nc,),
        in_specs=[
            pl.BlockSpec((cl, d_inner), lambda c: (c, 0)),
            pl.BlockSpec((cl, d_inner), lambda c: (c, 0)),
            pl.BlockSpec((cl, gn), lambda c: (c, d_inner // gn)),
            pl.BlockSpec((cl, gn), lambda c: (c, d_inner // gn + 1)),
            pl.BlockSpec((cl, n_heads), lambda c: (c, 0)),
            pl.BlockSpec((1, n_heads), lambda c: (0, 0)),
            pl.BlockSpec((1, n_heads), lambda c: (0, 0)),
            pl.BlockSpec((1, d_inner), lambda c: (0, 0)),
            pl.BlockSpec((1, d_inner), lambda c: (0, 0)),
            pl.BlockSpec((2 * n_heads, width), lambda c: (0, 0)),
        ],
        out_specs=pl.BlockSpec((cl, d_inner), lambda c: (c, 0)),
        out_shape=jax.ShapeDtypeStruct((s, d_inner), BF16),
        scratch_shapes=[
            pltpu.VMEM((g, D_STATE, width), F32),
        ],
        compiler_params=_params(("arbitrary",), est),
        name="ssd_scan",
    )(zs, xbc, xbc, xbc, dt_raw, dt_bias.astype(F32).reshape(1, n_heads), a_log.astype(F32).reshape(1, n_heads),
      dskip_w, g_norm.reshape(1, d_inner), jnp.asarray(expand2, BF16))


def _t5_bucket_table():
    q = np.arange(ATTN_BLOCK)[:, None]
    k = np.arange(2 * ATTN_BLOCK)[None, :]
    rel = np.maximum(q - k + ATTN_BLOCK, 0)
    max_exact = N_BUCKETS // 2
    relf = np.maximum(rel, 1).astype(np.float32)
    large = max_exact + (np.log(relf / np.float32(max_exact)) / np.float32(math.log(MAX_DISTANCE / max_exact))
                         * np.float32(N_BUCKETS - max_exact)).astype(np.int32)
    large = np.minimum(large, N_BUCKETS - 1)
    return np.where(rel < max_exact, rel, large).astype(np.int32)


def _attn_kernel(relb_ref, sink_ref, bucket_ref, q_ref, kp_ref, kc_ref, vp_ref, vc_ref, o_ref, bias_ref,
                 *, qpk, n_q_heads):
    gp = pl.program_id(0)
    n = pl.program_id(1)
    blk = ATTN_BLOCK
    npair = qpk // 2
    head0 = gp * 2 * qpk

    @pl.when(n == 0)
    def _():
        bucket = bucket_ref[...]
        qi = lax.broadcasted_iota(jnp.int32, (blk, 2 * blk), 0)
        ci = lax.broadcasted_iota(jnp.int32, (blk, 2 * blk), 1)
        rel = qi - ci + blk
        in_window = (rel >= 0) & (rel < WINDOW)

        def body(it, carry):
            kvh = it // npair
            pair = it - kvh * npair
            row0 = pl.multiple_of(pair * blk, blk)
            for hp in range(2):
                head = head0 + kvh * qpk + pair * 2 + hp
                acc = jnp.zeros((blk, 2 * blk), F32)
                for b in range(N_BUCKETS):
                    acc = jnp.where(bucket == b, relb_ref[b * n_q_heads + head], acc)
                bias_ref[kvh, pl.ds(row0, blk), hp * 2 * blk:(hp + 1) * 2 * blk] = jnp.where(in_window, acc, -jnp.inf)
            return carry

        lax.fori_loop(0, 2 * npair, body, 0)

    kk = jnp.concatenate([kp_ref[...], kc_ref[...]], axis=0)
    vv = jnp.concatenate([vp_ref[...], vc_ref[...]], axis=0)
    lane = lax.broadcasted_iota(jnp.int32, (2 * blk, LANES), 1)
    low = lane < ATTN_HEAD_DIM
    scale = ATTN_HEAD_DIM ** -0.5

    def pair_operand(t, kvh, mult):
        sel = jnp.where(low if kvh == 0 else jnp.logical_not(low), t, 0.0)
        dup = sel + pltpu.roll(sel, ATTN_HEAD_DIM, axis=1)
        if mult is not None:
            dup = dup * mult
        return jnp.concatenate([jnp.where(low, dup, 0.0), jnp.where(low, 0.0, dup)], axis=0).astype(BF16)

    row_bd = lax.broadcasted_iota(jnp.int32, (4 * blk, LANES), 0)
    lane_bd = lax.broadcasted_iota(jnp.int32, (4 * blk, LANES), 1)
    ones_bd = ((row_bd < 2 * blk) == (lane_bd < ATTN_HEAD_DIM)).astype(F32).astype(BF16)
    low_out = lax.broadcasted_iota(jnp.int32, (npair * blk, LANES), 1) < ATTN_HEAD_DIM

    def run(first_block):
        if first_block:
            ci = lax.broadcasted_iota(jnp.int32, (npair * blk, 2 * blk), 1)
            before_start = ci < blk
        for kvh in range(2):
            kbd = pair_operand(kk, kvh, scale)
            vbd = pair_operand(vv, kvh, None)
            base = kvh * npair
            qs = jnp.concatenate([q_ref[:, (base + i) * LANES:(base + i + 1) * LANES] for i in range(npair)], axis=0)
            s = lax.dot_general(qs, kbd, (((1,), (1,)), ((), ())), preferred_element_type=F32)
            es = []
            sink_terms = []
            for hp in range(2):
                sinks = [sink_ref[head0 + kvh * qpk + i * 2 + hp] for i in range(npair)]
                sink_col = jnp.concatenate([jnp.full((blk, 1), v, F32) for v in sinks], axis=0)
                sink_all = jnp.concatenate([jnp.full((blk, LANES), v, F32) for v in sinks], axis=0)
                sh = s[:, hp * 2 * blk:(hp + 1) * 2 * blk] + bias_ref[kvh, :, hp * 2 * blk:(hp + 1) * 2 * blk]
                if first_block:
                    sh = jnp.where(before_start, -jnp.inf, sh)
                m = jnp.maximum(jnp.max(sh, axis=-1, keepdims=True), sink_col)
                es.append(jnp.exp(sh - m).astype(BF16))
                sink_terms.append(jnp.exp(sink_all - jnp.broadcast_to(m, sink_all.shape)))
            e = jnp.concatenate(es, axis=1)
            ov = jnp.dot(e, jnp.concatenate([vbd, ones_bd], axis=1), preferred_element_type=F32)
            denom = ov[:, LANES:2 * LANES] + jnp.where(low_out, sink_terms[0], sink_terms[1])
            o = (ov[:, 0:LANES] * (1.0 / denom)).astype(o_ref.dtype)
            for i in range(npair):
                o_ref[:, (base + i) * LANES:(base + i + 1) * LANES] = o[i * blk:(i + 1) * blk, :]

    @pl.when(n == 0)
    def _():
        run(True)

    @pl.when(n != 0)
    def _():
        run(False)


def _attention(q, kv, sinks, rel_bias):
    s, qd = q.shape
    n_q_heads = qd // ATTN_HEAD_DIM
    qpk = n_q_heads // N_KV_HEADS
    assert qpk % 2 == 0 and N_KV_HEADS % 2 == 0 and 2 * ATTN_HEAD_DIM == LANES and s % ATTN_BLOCK == 0
    blk = ATTN_BLOCK
    nb = s // blk
    ngp = N_KV_HEADS // 2
    qw = 2 * qpk * ATTN_HEAD_DIM
    voff = N_KV_HEADS * ATTN_HEAD_DIM // LANES
    bucket = jnp.asarray(_t5_bucket_table())
    est = 2 * (2 * blk * qw * 2 + 4 * blk * LANES * 4 + blk * 2 * blk * 4) + 2 * qpk * blk * 2 * blk * 4 \
        + 24 * blk * 4 * blk * 4
    smem = pl.BlockSpec(memory_space=pltpu.SMEM)
    return pl.pallas_call(
        functools.partial(_attn_kernel, qpk=qpk, n_q_heads=n_q_heads),
        grid=(ngp, nb),
        in_specs=[
            smem, smem,
            pl.BlockSpec((blk, 2 * blk), lambda g, n: (0, 0)),
            pl.BlockSpec((blk, qw), lambda g, n: (n, g)),
            pl.BlockSpec((blk, LANES), lambda g, n: (jnp.maximum(n - 1, 0), g)),
            pl.BlockSpec((blk, LANES), lambda g, n: (n, g)),
            pl.BlockSpec((blk, LANES), lambda g, n: (jnp.maximum(n - 1, 0), voff + g)),
            pl.BlockSpec((blk, LANES), lambda g, n: (n, voff + g)),
        ],
        out_specs=pl.BlockSpec((blk, qw), lambda g, n: (n, g)),
        out_shape=jax.ShapeDtypeStruct((s, qd), BF16),
        scratch_shapes=[pltpu.VMEM((2, (qpk // 2) * blk, 4 * blk), F32)],
        compiler_params=_params(("parallel", "arbitrary"), est),
        name="swa_attention",
    )(rel_bias.astype(F32).reshape(-1), sinks.astype(F32).reshape(-1), bucket, q, kv, kv, kv, kv)


def kernel(x, norm_mix_pre, norm_mix_post, norm_ffn_pre, norm_ffn_post, ssm_w_in, ssm_conv_w, ssm_conv_b, ssm_dt_bias, ssm_a_log, ssm_d, ssm_norm, ssm_w_out, kv_norm, w_kv, b_kv, attn_w_q, attn_b_q, attn_sinks, attn_w_o, attn_b_o, rel_bias, ffn_w_up, ffn_conv_w, ffn_conv_b, ffn_w_down):
    bsz, s, d = x.shape
    assert bsz == 1 and norm_mix_pre.shape[0] == 2
    d_inner = ssm_norm.shape[-1]
    n_heads = ssm_dt_bias.shape[-1]
    zxbc_dim = ssm_w_in.shape[-1] - n_heads

    def mixer_out_ffn(resid, mix, layer):
        d_ff = ffn_w_up.shape[-1] // 2
        tn = _tile(d_ff, 256, LANES)
        tm = _tile(s, 1024, (d // MXU_K_CHUNK) * BF16_TILE_ROWS)
        h_head, u_head = _resnorm(resid, mix, norm_mix_post[layer], [norm_ffn_pre[layer]], rows=tm)
        hff, h = _conv_matmul(u_head, ffn_w_up, layer, d_ff, [0, d_ff // tn], ffn_conv_w[layer],
                              ffn_conv_b[layer].reshape(1, -1), tn, True, BF16, "ffn_up",
                              norm=(resid, mix, norm_mix_post[layer], norm_ffn_pre[layer], h_head))
        return h, _matmul(hff, ffn_w_down, None, F32, 1024, 256, "ffn_down", a_buffers=1, layer=layer)

    h0 = x.reshape(s, d)

    tm_in = _tile(s, 1024, (d // MXU_K_CHUNK) * BF16_TILE_ROWS)
    u_head = _prenorm(h0, norm_mix_pre[0], rows=tm_in)
    conv_dim = zxbc_dim - d_inner
    tn_in = _tile(math.gcd(d_inner, conv_dim), 512, LANES)
    xbc, u = _conv_matmul(u_head, ssm_w_in, 0, conv_dim, [d_inner // tn_in], ssm_conv_w[0].astype(F32),
                          ssm_conv_b[0].astype(F32).reshape(1, -1), tn_in, False, F32, "in_proj_xbc",
                          conv_col_blocks=[0], norm=(h0, norm_mix_pre[0]))
    zs = _matmul(u, ssm_w_in, None, F32, 1024, 512, "in_proj_z", n=d_inner, layer=0, silu=True)
    dt_raw = _matmul(u, ssm_w_in, None, F32, 1024, 128, "dt_proj", col0=zxbc_dim, n=n_heads, layer=0)
    y = _ssd(zs, xbc, dt_raw, ssm_dt_bias[0], ssm_a_log[0], ssm_d[0], ssm_norm[0], d_inner, n_heads)
    mix = _matmul(y, ssm_w_out, None, F32, 1024, 256, "out_proj", layer=0)
    h1, f = mixer_out_ffn(h0, mix, 0)

    q_dim = attn_w_q.shape[-1]
    h2_head, ukv_head, uq_head = _resnorm(h1, f, norm_ffn_post[0], [kv_norm, norm_mix_pre[1]], rows=tm_in)
    q, h2, ukv = _conv_matmul(uq_head, attn_w_q, 0, q_dim, [0], jnp.ones((1, q_dim), F32),
                              attn_b_q[0].astype(F32).reshape(1, -1), _tile(q_dim, 256, LANES), False, BF16, "q_proj",
                              norm=(h1, f, norm_ffn_post[0], norm_mix_pre[1], h2_head, kv_norm, ukv_head), silu=False)
    kv = _matmul(ukv, w_kv, b_kv, F32, 1024, 512, "kv_proj", cols_outer=True)
    o = _attention(q, kv, attn_sinks[0], rel_bias)
    mix = _matmul(o, attn_w_o, attn_b_o[0], F32, 1024, 512, "o_proj", layer=0)
    h3, f = mixer_out_ffn(h2, mix, 1)
    (h4,) = _resnorm(h3, f, norm_ffn_post[1], [])
    return h4.reshape(bsz, s, d)
```

```python
import functools
import math

import numpy as np
import jax
import jax.numpy as jnp
from jax import lax
from jax.experimental import pallas as pl
from jax.experimental.pallas import tpu as pltpu

EPS = 1e-6
SSM_NORM_EPS = 1e-5
SSM_HEAD_DIM = 64
SSM_GROUPS = 8
D_STATE = 128
SSM_CONV = 4
SSM_CHUNK = 128
SSD_PAIRS_PER_DOT = 2
ATTN_HEAD_DIM = 64
N_KV_HEADS = 8
WINDOW = 128
ATTN_BLOCK = 128
N_BUCKETS = 32
MAX_DISTANCE = 128
FFN_CONV = 3

LANES = 128
SUBLANES = 8
BF16_TILE_ROWS = 16
MXU_K_CHUNK = 256
VMEM_CAP_BYTES = 60 * 1024 * 1024
W_RING_BUDGET = 52 * 1024 * 1024

F32 = jnp.float32
BF16 = jnp.bfloat16


def _vmem_limit(est_bytes):
    return int(min(VMEM_CAP_BYTES, max(32 * 1024 * 1024, est_bytes * 5 // 4 + (4 << 20))))


def _params(semantics, est_bytes, flags=None):
    return pltpu.CompilerParams(dimension_semantics=semantics, vmem_limit_bytes=_vmem_limit(est_bytes), flags=flags)


def _tile(dim, pref, align):
    if dim <= pref:
        return dim
    t = (pref // align) * align
    while t >= align:
        if dim % t == 0:
            return t
        t -= align
    raise ValueError(f"no tile for {dim} (pref {pref}, align {align})")


def _sigmoid(x):
    return 0.5 * jnp.tanh(0.5 * x) + 0.5


def _softplus(x):
    return jnp.maximum(x, 0.0) + jnp.log1p(jnp.exp(-jnp.abs(x)))


def _split3(v):
    hi = v.astype(BF16)
    r1 = v - hi.astype(F32)
    mid = r1.astype(BF16)
    lo = (r1 - mid.astype(F32)).astype(BF16)
    return [hi, mid, lo]


def _rms(x, g, eps):
    ms = jnp.mean(x * x, axis=-1, keepdims=True)
    return (x * lax.rsqrt(ms + eps)) * g


def _prenorm_kernel(x_ref, g_ref, u_ref):
    u_ref[...] = _rms(x_ref[...], g_ref[...], EPS).astype(u_ref.dtype)


def _prenorm(x, g, rows=None):
    s, d = x.shape
    s = s if rows is None else rows
    tr = _tile(s, 256, SUBLANES)
    est = 2 * tr * d * (4 + 2)
    return pl.pallas_call(
        _prenorm_kernel,
        grid=(s // tr,),
        in_specs=[pl.BlockSpec((tr, d), lambda i: (i, 0)), pl.BlockSpec((1, d), lambda i: (0, 0))],
        out_specs=pl.BlockSpec((tr, d), lambda i: (i, 0)),
        out_shape=jax.ShapeDtypeStruct((s, d), BF16),
        compiler_params=_params(("parallel",), est),
        name="prenorm",
    )(x, g.reshape(1, d))


def _resnorm_kernel(r_ref, m_ref, gpost_ref, *rest, n_u):
    g_refs = rest[:n_u]
    h_ref = rest[n_u]
    u_refs = rest[n_u + 1:]
    h = r_ref[...] + _rms(m_ref[...], gpost_ref[...], EPS)
    h_ref[...] = h
    if n_u:
        ms = jnp.mean(h * h, axis=-1, keepdims=True)
        hn = h * lax.rsqrt(ms + EPS)
        for g_ref, u_ref in zip(g_refs, u_refs):
            u_ref[...] = (hn * g_ref[...]).astype(u_ref.dtype)


def _resnorm(r, m, gpost, gains, rows=None):
    s, d = r.shape
    n_u = len(gains)
    s = s if rows is None else rows
    tr = _tile(s, 256, SUBLANES)
    est = 2 * tr * d * (4 * 3 + 2 * n_u)
    row = pl.BlockSpec((tr, d), lambda i: (i, 0))
    vec = pl.BlockSpec((1, d), lambda i: (0, 0))
    outs = pl.pallas_call(
        functools.partial(_resnorm_kernel, n_u=n_u),
        grid=(s // tr,),
        in_specs=[row, row, vec] + [vec] * n_u,
        out_specs=[row] + [row] * n_u,
        out_shape=[jax.ShapeDtypeStruct((s, d), F32)] + [jax.ShapeDtypeStruct((s, d), BF16)] * n_u,
        compiler_params=_params(("parallel",), est),
        name="resnorm",
    )(r, m, gpost.reshape(1, d), *[g.reshape(1, d) for g in gains])
    return outs


W_RING = 3


def _matmul_kernel(a_ref, w_ref, *rest, has_bias, silu, ring=None):
    if ring is None:
        o_ref = rest[-1]
        w_blk = w_ref[...]
    else:
        layer, joff, nj, n_steps, tn = ring
        o_ref, wbuf, sem = rest[-3:]
        step = pl.program_id(0) * nj + pl.program_id(1)

        def fetch(s):
            col = pl.multiple_of((lax.rem(s, nj) + joff) * tn, tn)
            slot = lax.rem(s, W_RING)
            return pltpu.make_async_copy(w_ref.at[layer, :, pl.ds(col, tn)], wbuf.at[slot], sem.at[slot])

        @pl.when(step == 0)
        def _():
            for s in range(min(W_RING - 1, n_steps)):
                fetch(s).start()

        @pl.when(step + (W_RING - 1) < n_steps)
        def _():
            fetch(step + (W_RING - 1)).start()

        fetch(step).wait()
        w_blk = wbuf[lax.rem(step, W_RING)]
    acc = jnp.dot(a_ref[...], w_blk.astype(BF16), preferred_element_type=F32)
    if has_bias:
        acc = acc + rest[0][...]
    if silu:
        acc = acc * _sigmoid(acc)
    o_ref[...] = acc.astype(o_ref.dtype)


def _matmul(a, w, bias, out_dtype, tm_pref, tn_pref, name, a_buffers=2, col0=0, n=None, layer=None, silu=False,
            cols_outer=False):
    m, k = a.shape
    assert (w.ndim == 3) == (layer is not None)
    n = w.shape[-1] if n is None else n
    tm = _tile(m, tm_pref, SUBLANES)
    tn = _tile(math.gcd(n, col0) if col0 else n, tn_pref, LANES)
    assert n % tn == 0 and col0 % tn == 0
    joff = col0 // tn
    osz = jnp.dtype(out_dtype).itemsize
    wsz = jnp.dtype(w.dtype).itemsize
    est = a_buffers * tm * k * 2 + 2 * (k * tn * wsz + tm * tn * osz) + tm * tn * 4 + (k * tn * 2 if wsz != 2 else 0)
    ring = None
    scratch = []
    if layer is not None and not cols_outer and est + (W_RING - 2) * k * tn * wsz <= W_RING_BUDGET:
        ring = (layer, joff, n // tn, (m // tm) * (n // tn), tn)
        scratch = [pltpu.VMEM((W_RING, k, tn), w.dtype), pltpu.SemaphoreType.DMA((W_RING,))]
        est += (W_RING - 2) * k * tn * wsz
    a_mode = {} if a_buffers == 2 else {"pipeline_mode": pl.Buffered(a_buffers)}
    ij = (lambda p, q: (q, p)) if cols_outer else (lambda p, q: (p, q))
    if ring is not None:
        w_spec = pl.BlockSpec(memory_space=pl.ANY)
    elif layer is None:
        w_spec = pl.BlockSpec((k, tn), lambda p, q: (0, ij(p, q)[1] + joff))
    else:
        w_spec = pl.BlockSpec((None, k, tn), lambda p, q: (layer, 0, ij(p, q)[1] + joff))
    in_specs = [pl.BlockSpec((tm, k), lambda p, q: (ij(p, q)[0], 0), **a_mode), w_spec]
    args = [a, w]
    if bias is not None:
        in_specs.append(pl.BlockSpec((1, tn), lambda p, q: (0, ij(p, q)[1])))
        args.append(bias.reshape(1, n).astype(F32))
    semantics = ("arbitrary", "arbitrary") if ring is not None else ("parallel", "parallel")
    return pl.pallas_call(
        functools.partial(_matmul_kernel, has_bias=bias is not None, silu=silu, ring=ring),
        grid=(n // tn, m // tm) if cols_outer else (m // tm, n // tn),
        in_specs=in_specs,
        out_specs=pl.BlockSpec((tm, tn), lambda p, q: ij(p, q)),
        out_shape=jax.ShapeDtypeStruct((m, n), out_dtype),
        scratch_shapes=scratch,
        compiler_params=_params(semantics, est),
        name=name,
    )(*args)


def _zero_after(x):
    w = pltpu.bitcast(x, jnp.uint32)
    zero = (w >> 16) >> 16
    return zero if zero.shape[0] == BF16_TILE_ROWS else jnp.concatenate([zero, zero], axis=0)


def _conv_mm_kernel(*refs, nj, n_w, taps, gate, n_tiles, norm_rows, residual):
    if norm_rows and residual:
        u_ref, r_ref, m_ref, gpost_ref, gpre_ref, hhead_ref = refs[0:6]
        refs = refs[6:]
    elif norm_rows:
        u_ref, r_ref, gpre_ref = refs[0:3]
        refs = refs[3:]
    else:
        u_ref = refs[0]
        refs = refs[1:]
    w_refs = refs[0:n_w]
    cw_refs = refs[n_w:2 * n_w]
    cb_refs = refs[2 * n_w:3 * n_w]
    if norm_rows:
        o_ref, h_ref, halo, ext_a, ext_b, wcat, u_cur, u_next = refs[3 * n_w:]
    else:
        o_ref, halo, ext_a, ext_b, wcat = refs[3 * n_w:]
    t = pl.program_id(0)
    tm = u_ref.shape[0]
    d = u_ref.shape[1]
    tn = w_refs[0].shape[1]
    nch = d // MXU_K_CHUNK
    rows = tm // nch
    jp = lax.rem(jnp.maximum(t - 1, 0), nj)

    @pl.when(t == 0)
    def _():
        halo[...] = jnp.zeros(halo.shape, F32)
        ext_b[...] = jnp.zeros(ext_b.shape, F32)

    if norm_rows:
        tile = jnp.minimum(t, n_tiles - 1)
        jm = lax.rem(tile, nj)
        new_row_tile = (jm == 0) & (t < n_tiles)

        @pl.when(new_row_tile & (tile == 0))
        def _():
            u_cur[...] = u_ref[...]

        @pl.when(new_row_tile & (tile > 0))
        def _():
            u_cur[...] = u_next[...]

        lhs_ref = u_cur
    else:
        lhs_ref = u_ref

    def step(ext_mm, ext_ep):
        norm_zero = None
        if norm_rows:
            last_row_tile = tile // nj + 1 >= n_tiles // nj
            row0 = pl.multiple_of(jnp.minimum(jm, tm // norm_rows - 1) * norm_rows, norm_rows)
            if residual:
                h = r_ref[...] + _rms(m_ref[...], gpost_ref[...], EPS)
                h_ref[...] = jnp.where(last_row_tile, hhead_ref[...], h)
                u_new = _rms(h, gpre_ref[...], EPS).astype(BF16)
            else:
                u_new = _rms(r_ref[...], gpre_ref[...], EPS).astype(BF16)
                h_ref[...] = jnp.where(last_row_tile, u_ref[pl.ds(row0, norm_rows), :], u_new)
            u_next[pl.ds(row0, norm_rows), :] = u_new
            norm_zero = _zero_after(u_new[0:BF16_TILE_ROWS, 0:tn])
        cw = jnp.concatenate([r[...] for r in cw_refs], axis=1)
        cb = jnp.concatenate([r[...] for r in cb_refs], axis=1)
        prev = halo[jp]
        for c in range(nch):
            cur = ext_ep[c * rows:(c + 1) * rows, :]
            both = jnp.concatenate([prev, cur], axis=0)
            acc = cb
            for k in range(taps - 1):
                r0 = SUBLANES - (taps - 1) + k
                acc = acc + both[r0:r0 + rows, :] * cw[k:k + 1, :]
            acc = acc + cur * cw[taps - 1:taps, :]
            if gate:
                out = (acc[:, 0:tn] * _sigmoid(acc[:, 0:tn])) * acc[:, tn:2 * tn]
            else:
                out = acc * _sigmoid(acc)
            out = out.astype(o_ref.dtype)
            o_ref[c * rows:(c + 1) * rows, :] = out
            prev = cur[rows - SUBLANES:rows, :]

            zero = _zero_after(out[0:BF16_TILE_ROWS, 0:tn])
            if norm_zero is not None and c == nch - 1:
                zero = zero | norm_zero
            k0 = c * MXU_K_CHUNK
            for idx, w_ref in enumerate(w_refs):
                top = pltpu.bitcast(pltpu.bitcast(w_ref[k0:k0 + BF16_TILE_ROWS, :], jnp.uint32) | zero, F32)
                wcat[k0:k0 + BF16_TILE_ROWS, idx * tn:(idx + 1) * tn] = top.astype(BF16)
                wcat[k0 + BF16_TILE_ROWS:k0 + MXU_K_CHUNK, idx * tn:(idx + 1) * tn] = (
                    w_ref[k0 + BF16_TILE_ROWS:k0 + MXU_K_CHUNK, :].astype(BF16))
        halo[jp] = prev
        ext_mm[...] = jnp.dot(lhs_ref[...], wcat[...], preferred_element_type=F32)

    @pl.when(lax.rem(t, 2) == 0)
    def _():
        step(ext_a, ext_b)

    @pl.when(lax.rem(t, 2) == 1)
    def _():
        step(ext_b, ext_a)


def _conv_matmul(u, w, layer, n_cols, col_blocks, conv_w, conv_b, tn, gate, out_dtype, name, conv_col_blocks=None,
                 norm=None):
    d = u.shape[1]
    s = u.shape[0] if norm is None else norm[0].shape[0]
    n_w = len(col_blocks)
    assert n_w == (2 if gate else 1) and d % MXU_K_CHUNK == 0
    taps = conv_w.shape[0]
    assert n_cols % tn == 0
    nj = n_cols // tn
    nch = d // MXU_K_CHUNK
    tm = _tile(s, 1024, nch * BF16_TILE_ROWS)
    wsz = jnp.dtype(w.dtype).itemsize
    osz = jnp.dtype(out_dtype).itemsize
    wt = n_w * tn
    wo = tn if gate else wt
    est = 2 * (tm * d * 2 + d * wt * wsz + tm * wo * osz) + d * wt * 2 + nj * SUBLANES * wt * 4 + 4 * tm * wt * 4
    n_tiles = (s // tm) * nj

    def mm_tile(t):
        return jnp.minimum(t, n_tiles - 1)

    def ep_tile(t):
        return jnp.maximum(t - 1, 0)

    residual = False
    if norm is None:
        norm_rows = 0
        lead_specs = [pl.BlockSpec((tm, d), lambda t: (mm_tile(t) // nj, 0))]
        lead_args = [u]
        extra_out_specs, extra_out_shapes, extra_scratch = [], [], []
    else:
        residual = len(norm) == 5
        assert u.shape == (tm, d)
        norm_rows = next(c for c in range(BF16_TILE_ROWS, tm + 1, BF16_TILE_ROWS) if tm % c == 0 and tm // c <= nj)
        per_tile = tm // norm_rows
        n_row_tiles = s // tm

        def head_block(t):
            return jnp.minimum(mm_tile(t) % nj, per_tile - 1)

        def norm_block(t):
            ahead = mm_tile(t) // nj + 1
            return jnp.where(ahead < n_row_tiles, ahead, 0) * per_tile + head_block(t)

        rows_spec = pl.BlockSpec((norm_rows, d), lambda t: (norm_block(t), 0))
        vec_spec = pl.BlockSpec((1, d), lambda t: (0, 0))
        u_spec = pl.BlockSpec((tm, d), lambda t: (0, 0), pipeline_mode=pl.Buffered(1))
        if residual:
            r, m, gpost, gpre, h_head = norm
            lead_specs = [u_spec, rows_spec, rows_spec, vec_spec, vec_spec,
                          pl.BlockSpec((norm_rows, d), lambda t: (head_block(t), 0))]
            lead_args = [u, r, m, gpost.reshape(1, d), gpre.reshape(1, d), h_head]
            extra_out_shapes = [jax.ShapeDtypeStruct((s, d), F32)]
        else:
            r, gpre = norm
            lead_specs = [u_spec, rows_spec, vec_spec]
            lead_args = [u, r, gpre.reshape(1, d)]
            extra_out_shapes = [jax.ShapeDtypeStruct((s, d), BF16)]
        extra_out_specs = [rows_spec]
        extra_scratch = [pltpu.VMEM((tm, d), BF16), pltpu.VMEM((tm, d), BF16)]
        est += tm * d * 2 + 8 * norm_rows * d * 4
    w_specs = [pl.BlockSpec((None, d, tn), lambda t, o=o: (layer, 0, mm_tile(t) % nj + o)) for o in col_blocks]
    conv_col_blocks = col_blocks if conv_col_blocks is None else conv_col_blocks
    cw_specs = [pl.BlockSpec((taps, tn), lambda t, o=o: (0, ep_tile(t) % nj + o)) for o in conv_col_blocks]
    cb_specs = [pl.BlockSpec((1, tn), lambda t, o=o: (0, ep_tile(t) % nj + o)) for o in conv_col_blocks]
    outs = pl.pallas_call(
        functools.partial(_conv_mm_kernel, nj=nj, n_w=n_w, taps=taps, gate=gate, n_tiles=n_tiles,
                          norm_rows=norm_rows, residual=residual),
        grid=(n_tiles + 1,),
        in_specs=lead_specs + w_specs + cw_specs + cb_specs,
        out_specs=[pl.BlockSpec((tm, wo), lambda t: (ep_tile(t) // nj, ep_tile(t) % nj))] + extra_out_specs,
        out_shape=[jax.ShapeDtypeStruct((s, nj * wo), out_dtype)] + extra_out_shapes,
        scratch_shapes=[
            pltpu.VMEM((nj, SUBLANES, wt), F32),
            pltpu.VMEM((tm, wt), F32),
            pltpu.VMEM((tm, wt), F32),
            pltpu.VMEM((d, wt), BF16),
        ] + extra_scratch,
        compiler_params=_params(("arbitrary",), est),
        name=name,
    )(*lead_args, *([w] * n_w), *([conv_w] * n_w), *([conv_b] * n_w))
    return outs[0] if norm is None else outs


def _ssd_kernel(z_ref, x_ref, b_ref, c_ref, dt_ref, dtb_ref, alog_ref, dskip_ref, gnorm_ref, expand_ref,
                y_ref, state_ref, *, heads, groups):
    chunk = pl.program_id(0)

    @pl.when(chunk == 0)
    def _():
        state_ref[...] = jnp.zeros(state_ref.shape, F32)

    dtv_all = _softplus(dt_ref[...] + dtb_ref[...])
    a_all = dtv_all * (-jnp.exp(alog_ref[...]))
    a_all_t = jnp.transpose(a_all)
    n_all = dtv_all.shape[1]

    width = heads * SSM_HEAD_DIM
    for gi in range(groups):
        cols = slice(gi * width, (gi + 1) * width)
        ncols = slice(gi * D_STATE, (gi + 1) * D_STATE)
        shift = (n_all - gi * heads) % n_all
        dtv = pltpu.roll(dtv_all, shift, axis=1) if shift else dtv_all
        a = pltpu.roll(a_all, shift, axis=1) if shift else a_all
        y_ref[:, cols] = _ssd_group_chunk(
            z_ref[:, cols], x_ref[:, cols], b_ref[:, ncols], c_ref[:, ncols], dtv, a,
            a_all_t[gi * heads:(gi + 1) * heads, :], dskip_ref[:, cols], gnorm_ref[:, cols], expand_ref,
            state_ref.at[gi], heads).astype(y_ref.dtype)


def _ssd_group_chunk(zs, xs, bc, cc, dtv, a, a_t, dskip, gnorm, expand_ref, state_ref, heads):
    cl = SSM_CHUNK
    width = heads * SSM_HEAD_DIM

    li = lax.broadcasted_iota(jnp.int32, (cl, cl), 0)
    si = lax.broadcasted_iota(jnp.int32, (cl, cl), 1)
    tril = li >= si

    ones_lo = tril.astype(F32).astype(BF16)
    ones_up = (li <= si).astype(F32).astype(BF16)
    acum = jnp.dot(jnp.concatenate([ones_lo] * 3, axis=1), jnp.concatenate(_split3(a), axis=0),
                   preferred_element_type=F32)
    acum_t = jnp.dot(jnp.concatenate(_split3(a_t), axis=1), jnp.concatenate([ones_up] * 3, axis=0),
                     preferred_element_type=F32)
    alast = acum[cl - 1:cl, :]
    exp_acum = jnp.exp(acum)
    decay_end = jnp.exp(alast - acum)

    stacked = jnp.concatenate([dtv, exp_acum, decay_end], axis=0)
    wide = jnp.dot(jnp.concatenate(_split3(stacked)[0:2], axis=1), expand_ref[...], preferred_element_type=F32)
    dt_w = wide[0:cl, :]
    exp_acum_w = wide[cl:2 * cl, :]
    decay_end_w = wide[2 * cl:3 * cl, :]

    xdt = xs * dt_w
    state = state_ref[...]
    bc_t = jnp.transpose(bc).astype(BF16)
    c_prod = jnp.dot(cc.astype(BF16), jnp.concatenate([state.astype(BF16), bc_t], axis=1),
                     preferred_element_type=F32)
    y_off = c_prod[:, 0:width] * exp_acum_w
    cb = jnp.where(tril, c_prod[:, width:width + cl], 0.0)

    lane = lax.broadcasted_iota(jnp.int32, (cl, LANES), 1)
    low = lane < SSM_HEAD_DIM
    npairs = heads // 2
    ppd = min(SSD_PAIRS_PER_DOT, npairs)
    zero_blk = jnp.zeros((cl, LANES), BF16)
    y_diag = []
    for q in range(npairs // ppd):
        ms = []
        rhs_rows = []
        for pi in range(ppd):
            p = q * ppd + pi
            for hp in range(2):
                r = 2 * p + hp
                seg = acum[:, r:r + 1] - acum_t[r:r + 1, :]
                decay = jnp.exp(jnp.minimum(seg, 0.0))
                ms.append((cb * decay).astype(BF16))
            xp = xdt[:, p * LANES:(p + 1) * LANES]
            for blk in (jnp.where(low, xp, 0.0).astype(BF16), jnp.where(low, 0.0, xp).astype(BF16)):
                rhs_rows.append(jnp.concatenate([zero_blk] * pi + [blk] + [zero_blk] * (ppd - 1 - pi), axis=1))
        y_diag.append(jnp.dot(jnp.concatenate(ms, axis=1), jnp.concatenate(rhs_rows, axis=0),
                              preferred_element_type=F32))
    y = jnp.concatenate(y_diag, axis=1) + y_off + dskip * xs

    xw = (xdt * decay_end_w).astype(BF16)
    upd = jnp.dot(bc_t, xw, preferred_element_type=F32)
    state_ref[...] = state * exp_acum_w[cl - 1:cl, :] + upd

    yz = y * zs
    ms2 = jnp.mean(yz * yz, axis=-1, keepdims=True)
    return (yz * lax.rsqrt(ms2 + SSM_NORM_EPS)) * gnorm


def _ssd(zs, xbc, dt_raw, dt_bias, a_log, d_skip, g_norm, d_inner, n_heads):
    s = zs.shape[0]
    g = SSM_GROUPS
    heads = n_heads // g
    width = d_inner // g
    assert width == heads * SSM_HEAD_DIM and heads % 2 == 0 and heads <= LANES and width % LANES == 0
    assert (heads // 2) % min(SSD_PAIRS_PER_DOT, heads // 2) == 0
    assert D_STATE == LANES and s % SSM_CHUNK == 0
    dskip_w = jnp.repeat(d_skip.astype(F32), SSM_HEAD_DIM).reshape(1, d_inner)
    nc = s // SSM_CHUNK
    cl = SSM_CHUNK
    expand2 = np.tile(np.arange(n_heads)[:, None] == (np.arange(width)[None, :] // SSM_HEAD_DIM), (2, 1))

    gn = g * D_STATE
    assert d_inner % gn == 0
    est = 2 * (2 * cl * d_inner * 4 + 2 * cl * gn * 4 + cl * n_heads * 4 + cl * d_inner * 2) \
        + g * D_STATE * width * 4 + 24 * cl * width * 4
    return pl.pallas_call(
        functools.partial(_ssd_kernel, heads=heads, groups=g),
        grid=(nc,),
        in_specs=[
            pl.BlockSpec((cl, d_inner), lambda c: (c, 0)),
            pl.BlockSpec((cl, d_inner), lambda c: (c, 0)),
            pl.BlockSpec((cl, gn), lambda c: (c, d_inner // gn)),
            pl.BlockSpec((cl, gn), lambda c: (c, d_inner // gn + 1)),
            pl.BlockSpec((cl, n_heads), lambda c: (c, 0)),
            pl.BlockSpec((1, n_heads), lambda c: (0, 0)),
            pl.BlockSpec((1, n_heads), lambda c: (0, 0)),
            pl.BlockSpec((1, d_inner), lambda c: (0, 0)),
            pl.BlockSpec((1, d_inner), lambda c: (0, 0)),
            pl.BlockSpec((2 * n_heads, width), lambda c: (0, 0)),
        ],
        out_specs=pl.BlockSpec((cl, d_inner), lambda c: (c, 0)),
        out_shape=jax.ShapeDtypeStruct((s, d_inner), BF16),
        scratch_shapes=[
            pltpu.VMEM((g, D_STATE, width), F32),
        ],
        compiler_params=_params(("arbitrary",), est),
        name="ssd_scan",
    )(zs, xbc, xbc, xbc, dt_raw, dt_bias.astype(F32).reshape(1, n_heads), a_log.astype(F32).reshape(1, n_heads),
      dskip_w, g_norm.reshape(1, d_inner), jnp.asarray(expand2, BF16))


def _t5_bucket_table():
    q = np.arange(ATTN_BLOCK)[:, None]
    k = np.arange(2 * ATTN_BLOCK)[None, :]
    rel = np.maximum(q - k + ATTN_BLOCK, 0)
    max_exact = N_BUCKETS // 2
    relf = np.maximum(rel, 1).astype(np.float32)
    large = max_exact + (np.log(relf / np.float32(max_exact)) / np.float32(math.log(MAX_DISTANCE / max_exact))
                         * np.float32(N_BUCKETS - max_exact)).astype(np.int32)
    large = np.minimum(large, N_BUCKETS - 1)
    return np.where(rel < max_exact, rel, large).astype(np.int32)


def _attn_kernel(relb_ref, sink_ref, bucket_ref, q_ref, kp_ref, kc_ref, vp_ref, vc_ref, o_ref, bias_ref,
                 *, qpk, n_q_heads):
    gp = pl.program_id(0)
    n = pl.program_id(1)
    blk = ATTN_BLOCK
    npair = qpk // 2
    head0 = gp * 2 * qpk

    @pl.when(n == 0)
    def _():
        bucket = bucket_ref[...]
        qi = lax.broadcasted_iota(jnp.int32, (blk, 2 * blk), 0)
        ci = lax.broadcasted_iota(jnp.int32, (blk, 2 * blk), 1)
        rel = qi - ci + blk
        in_window = (rel >= 0) & (rel < WINDOW)

        def body(it, carry):
            kvh = it // npair
            pair = it - kvh * npair
            row0 = pl.multiple_of(pair * blk, blk)
            for hp in range(2):
                head = head0 + kvh * qpk + pair * 2 + hp
                acc = jnp.zeros((blk, 2 * blk), F32)
                for b in range(N_BUCKETS):
                    acc = jnp.where(bucket == b, relb_ref[b * n_q_heads + head], acc)
                bias_ref[kvh, pl.ds(row0, blk), hp * 2 * blk:(hp + 1) * 2 * blk] = jnp.where(in_window, acc, -jnp.inf)
            return carry

        lax.fori_loop(0, 2 * npair, body, 0)

    kk = jnp.concatenate([kp_ref[...], kc_ref[...]], axis=0)
    vv = jnp.concatenate([vp_ref[...], vc_ref[...]], axis=0)
    lane = lax.broadcasted_iota(jnp.int32, (2 * blk, LANES), 1)
    low = lane < ATTN_HEAD_DIM
    scale = ATTN_HEAD_DIM ** -0.5

    def pair_operand(t, kvh, mult):
        sel = jnp.where(low if kvh == 0 else jnp.logical_not(low), t, 0.0)
        dup = sel + pltpu.roll(sel, ATTN_HEAD_DIM, axis=1)
        if mult is not None:
            dup = dup * mult
        return jnp.concatenate([jnp.where(low, dup, 0.0), jnp.where(low, 0.0, dup)], axis=0).astype(BF16)

    row_bd = lax.broadcasted_iota(jnp.int32, (4 * blk, LANES), 0)
    lane_bd = lax.broadcasted_iota(jnp.int32, (4 * blk, LANES), 1)
    ones_bd = ((row_bd < 2 * blk) == (lane_bd < ATTN_HEAD_DIM)).astype(F32).astype(BF16)
    low_out = lax.broadcasted_iota(jnp.int32, (npair * blk, LANES), 1) < ATTN_HEAD_DIM

    def run(first_block):
        if first_block:
            ci = lax.broadcasted_iota(jnp.int32, (npair * blk, 2 * blk), 1)
            before_start = ci < blk
        for kvh in range(2):
            kbd = pair_operand(kk, kvh, scale)
            vbd = pair_operand(vv, kvh, None)
            base = kvh * npair
            qs = jnp.concatenate([q_ref[:, (base + i) * LANES:(base + i + 1) * LANES] for i in range(npair)], axis=0)
            s = lax.dot_general(qs, kbd, (((1,), (1,)), ((), ())), preferred_element_type=F32)
            es = []
            sink_terms = []
            for hp in range(2):
                sinks = [sink_ref[head0 + kvh * qpk + i * 2 + hp] for i in range(npair)]
                sink_col = jnp.concatenate([jnp.full((blk, 1), v, F32) for v in sinks], axis=0)
                sink_all = jnp.concatenate([jnp.full((blk, LANES), v, F32) for v in sinks], axis=0)
                sh = s[:, hp * 2 * blk:(hp + 1) * 2 * blk] + bias_ref[kvh, :, hp * 2 * blk:(hp + 1) * 2 * blk]
                if first_block:
                    sh = jnp.where(before_start, -jnp.inf, sh)
                m = jnp.maximum(jnp.max(sh, axis=-1, keepdims=True), sink_col)
                es.append(jnp.exp(sh - m).astype(BF16))
                sink_terms.append(jnp.exp(sink_all - jnp.broadcast_to(m, sink_all.shape)))
            e = jnp.concatenate(es, axis=1)
            ov = jnp.dot(e, jnp.concatenate([vbd, ones_bd], axis=1), preferred_element_type=F32)
            denom = ov[:, LANES:2 * LANES] + jnp.where(low_out, sink_terms[0], sink_terms[1])
            o = (ov[:, 0:LANES] * (1.0 / denom)).astype(o_ref.dtype)
            for i in range(npair):
                o_ref[:, (base + i) * LANES:(base + i + 1) * LANES] = o[i * blk:(i + 1) * blk, :]

    @pl.when(n == 0)
    def _():
        run(True)

    @pl.when(n != 0)
    def _():
        run(False)


def _attention(q, kv, sinks, rel_bias):
    s, qd = q.shape
    n_q_heads = qd // ATTN_HEAD_DIM
    qpk = n_q_heads // N_KV_HEADS
    assert qpk % 2 == 0 and N_KV_HEADS % 2 == 0 and 2 * ATTN_HEAD_DIM == LANES and s % ATTN_BLOCK == 0
    blk = ATTN_BLOCK
    nb = s // blk
    ngp = N_KV_HEADS // 2
    qw = 2 * qpk * ATTN_HEAD_DIM
    voff = N_KV_HEADS * ATTN_HEAD_DIM // LANES
    bucket = jnp.asarray(_t5_bucket_table())
    est = 2 * (2 * blk * qw * 2 + 4 * blk * LANES * 4 + blk * 2 * blk * 4) + 2 * qpk * blk * 2 * blk * 4 \
        + 24 * blk * 4 * blk * 4
    smem = pl.BlockSpec(memory_space=pltpu.SMEM)
    return pl.pallas_call(
        functools.partial(_attn_kernel, qpk=qpk, n_q_heads=n_q_heads),
        grid=(ngp, nb),
        in_specs=[
            smem, smem,
            pl.BlockSpec((blk, 2 * blk), lambda g, n: (0, 0)),
            pl.BlockSpec((blk, qw), lambda g, n: (n, g)),
            pl.BlockSpec((blk, LANES), lambda g, n: (jnp.maximum(n - 1, 0), g)),
            pl.BlockSpec((blk, LANES), lambda g, n: (n, g)),
            pl.BlockSpec((blk, LANES), lambda g, n: (jnp.maximum(n - 1, 0), voff + g)),
            pl.BlockSpec((blk, LANES), lambda g, n: (n, voff + g)),
        ],
        out_specs=pl.BlockSpec((blk, qw), lambda g, n: (n, g)),
        out_shape=jax.ShapeDtypeStruct((s, qd), BF16),
        scratch_shapes=[pltpu.VMEM((2, (qpk // 2) * blk, 4 * blk), F32)],
        compiler_params=_params(("parallel", "arbitrary"), est),
        name="swa_attention",
    )(rel_bias.astype(F32).reshape(-1), sinks.astype(F32).reshape(-1), bucket, q, kv, kv, kv, kv)


def kernel(x, norm_mix_pre, norm_mix_post, norm_ffn_pre, norm_ffn_post, ssm_w_in, ssm_conv_w, ssm_conv_b, ssm_dt_bias, ssm_a_log, ssm_d, ssm_norm, ssm_w_out, kv_norm, w_kv, b_kv, attn_w_q, attn_b_q, attn_sinks, attn_w_o, attn_b_o, rel_bias, ffn_w_up, ffn_conv_w, ffn_conv_b, ffn_w_down):
    bsz, s, d = x.shape
    assert bsz == 1 and norm_mix_pre.shape[0] == 2
    d_inner = ssm_norm.shape[-1]
    n_heads = ssm_dt_bias.shape[-1]
    zxbc_dim = ssm_w_in.shape[-1] - n_heads

    def mixer_out_ffn(resid, mix, layer):
        d_ff = ffn_w_up.shape[-1] // 2
        tn = _tile(d_ff, 256, LANES)
        tm = _tile(s, 1024, (d // MXU_K_CHUNK) * BF16_TILE_ROWS)
        h_head, u_head = _resnorm(resid, mix, norm_mix_post[layer], [norm_ffn_pre[layer]], rows=tm)
        hff, h = _conv_matmul(u_head, ffn_w_up, layer, d_ff, [0, d_ff // tn], ffn_conv_w[layer],
                              ffn_conv_b[layer].reshape(1, -1), tn, True, BF16, "ffn_up",
                              norm=(resid, mix, norm_mix_post[layer], norm_ffn_pre[layer], h_head))
        return h, _matmul(hff, ffn_w_down, None, F32, 1024, 256, "ffn_down", a_buffers=1, layer=layer)

    h0 = x.reshape(s, d)

    tm_in = _tile(s, 1024, (d // MXU_K_CHUNK) * BF16_TILE_ROWS)
    u_head = _prenorm(h0, norm_mix_pre[0], rows=tm_in)
    conv_dim = zxbc_dim - d_inner
    tn_in = _tile(math.gcd(d_inner, conv_dim), 512, LANES)
    xbc, u = _conv_matmul(u_head, ssm_w_in, 0, conv_dim, [d_inner // tn_in], ssm_conv_w[0].astype(F32),
                          ssm_conv_b[0].astype(F32).reshape(1, -1), tn_in, False, F32, "in_proj_xbc",
                          conv_col_blocks=[0], norm=(h0, norm_mix_pre[0]))
    zs = _matmul(u, ssm_w_in, None, F32, 1024, 512, "in_proj_z", n=d_inner, layer=0, silu=True)
    dt_raw = _matmul(u, ssm_w_in, None, F32, 1024, 128, "dt_proj", col0=zxbc_dim, n=n_heads, layer=0)
    y = _ssd(zs, xbc, dt_raw, ssm_dt_bias[0], ssm_a_log[0], ssm_d[0], ssm_norm[0], d_inner, n_heads)
    mix = _matmul(y, ssm_w_out, None, F32, 1024, 256, "out_proj", layer=0)
    h1, f = mixer_out_ffn(h0, mix, 0)

    h2, ukv, uq = _resnorm(h1, f, norm_ffn_post[0], [kv_norm, norm_mix_pre[1]])
    kv = _matmul(ukv, w_kv, b_kv, F32, 1024, 512, "kv_proj", cols_outer=True)
    q = _matmul(uq, attn_w_q, attn_b_q[0], BF16, 1024, 512, "q_proj", layer=0)
    o = _attention(q, kv, attn_sinks[0], rel_bias)
    mix = _matmul(o, attn_w_o, attn_b_o[0], F32, 1024, 512, "o_proj", layer=0)
    h3, f = mixer_out_ffn(h2, mix, 1)
    (h4,) = _resnorm(h3, f, norm_ffn_post[1], [])
    return h4.reshape(bsz, s, d)
```
